```python
import math
import jax, jax.numpy as jnp
from jax import lax
import numpy as np

D_MODEL = 1024
BATCH = 2
SEQ = 8192
DEPTH = 4
DEC_BATCH = 128
DEC_SEQ = 1
PAST_LEN = 8192
PAGE_SIZE = 128

N_MEM = 256
EPS = 1e-6
GLA_HEADS = 4
GLA_DK = 48
GLA_DV = 96
GLA_RANK = 16
GLA_TAU = 16.0
GLA_CHUNK = 64
RG_WIDTH = 384
RG_BLOCKS = 4
RG_BW = RG_WIDTH // RG_BLOCKS
RG_C = 8.0
CONV_W = 4
SWA_HEADS = 4
SWA_KV = 2
SWA_G = SWA_HEADS // SWA_KV
HEAD_DIM = 64
WINDOW = 128
MEM_HEADS = 4
MEM_HD = 64
D_FF = 4 * D_MODEL

GLA_QK = GLA_HEADS * GLA_DK
GLA_V = GLA_HEADS * GLA_DV
SWA_Q = SWA_HEADS * HEAD_DIM
SWA_KVW = SWA_KV * HEAD_DIM
MIX_WIDTH = GLA_V + RG_WIDTH + SWA_Q
IN_SPLITS = (GLA_QK, GLA_QK, GLA_V, GLA_RANK, GLA_V, RG_WIDTH, RG_WIDTH, SWA_Q, SWA_KVW, SWA_KVW)
IN_WIDTH = GLA_QK * 2 + GLA_V * 2 + GLA_RANK + RG_WIDTH * 2 + SWA_Q + SWA_KVW * 2

kernel_name = "hymba_gla_rglru_swa_step"


def rmsnorm(x, g):
    xf = x.astype(jnp.float32)
    y = xf * lax.rsqrt(jnp.mean(xf * xf, axis=-1, keepdims=True) + EPS)
    return (y * g.astype(jnp.float32)).astype(x.dtype)


def split_cols(h):
    outs, start = [], 0
    for w in IN_SPLITS:
        outs.append(h[..., start:start + w])
        start += w
    return outs


def alibi_slopes():
    return jnp.exp2(-8.0 * jnp.arange(1, SWA_HEADS + 1, dtype=jnp.float32) / SWA_HEADS)


def gla_chunk_step(S0, inp):
    q, k, v, g = inp
    C = q.shape[1]
    cum = jnp.cumsum(g, axis=1)
    causal = jnp.tril(jnp.ones((C, C), bool))
    rel = jnp.where(causal[None, :, :, None, None], cum[:, :, None] - cum[:, None, :], -jnp.inf)
    A = jnp.einsum('bthk,bshk,btshk->bhts', q, k, jnp.exp(rel))
    o = jnp.einsum('bthk,bhkv->bthv', q * jnp.exp(cum), S0) + jnp.einsum('bhts,bshv->bthv', A, v)
    last = cum[:, -1]
    S = jnp.exp(last)[..., None] * S0 + jnp.einsum('bshk,bshv->bhkv', k * jnp.exp(last[:, None] - cum), v)
    return S, o


def gla_scan(S0, q, k, v, g):
    B, T = q.shape[:2]
    C = math.gcd(T, GLA_CHUNK)

    def to_chunks(a):
        return jnp.moveaxis(a.reshape(B, T // C, C, *a.shape[2:]), 1, 0)

    S, o = lax.scan(gla_chunk_step, S0, (to_chunks(q), to_chunks(k), to_chunks(v), to_chunks(g)))
    return S, jnp.moveaxis(o, 0, 1).reshape(B, T, GLA_HEADS, GLA_DV)


def causal_dwconv(x, buf, w, b):
    T = x.shape[1]
    xp = jnp.concatenate([buf.astype(x.dtype), x], axis=1)
    y = b
    for j in range(CONV_W):
        y = y + xp[:, j:j + T] * w[j]
    return y, xp[:, -(CONV_W - 1):]


def rglru(xc, h0, w_a, b_a, w_x, b_x, lam):
    f32 = jnp.float32
    B, T, _ = xc.shape
    xf = xc.astype(f32)
    xb = xf.reshape(B, T, RG_BLOCKS, RG_BW)
    r = jax.nn.sigmoid(jnp.einsum('btnc,ncd->btnd', xb, w_a.astype(f32)).reshape(B, T, RG_WIDTH) + b_a.astype(f32))
    i = jax.nn.sigmoid(jnp.einsum('btnc,ncd->btnd', xb, w_x.astype(f32)).reshape(B, T, RG_WIDTH) + b_x.astype(f32))
    log_a = -RG_C * r * jax.nn.softplus(-lam.astype(f32))
    a = jnp.exp(log_a)
    bx = jnp.sqrt(-jnp.expm1(2.0 * log_a)) * (i * xf)
    bx = bx.at[:, 0].add(a[:, 0] * h0)

    def comb(l, rr):
        return (l[0] * rr[0], rr[0] * l[1] + rr[1])

    _, h = lax.associative_scan(comb, (a, bx), axis=1)
    return h, h[:, -1]


def swa_attend(q, k, v, key_valid, slopes, sinks):
    T, S = q.shape[2], k.shape[2]
    dist = (S - T + jnp.arange(T, dtype=jnp.float32))[:, None] - jnp.arange(S, dtype=jnp.float32)[None, :]
    allowed = (dist >= 0) & (dist <= WINDOW) & key_valid[:, None, :]
    s = jnp.einsum('bntkgd,bnskd->bnkgts', q, k).astype(jnp.float32) * HEAD_DIM ** -0.5
    s = s - slopes[:, :, None, None] * dist
    s = jnp.where(allowed[None, :, None, None], s, -jnp.inf)
    sink = jnp.broadcast_to(sinks.astype(jnp.float32)[:, :, None, None], s.shape[:-1] + (1,))
    p = jax.nn.softmax(jnp.concatenate([s, sink], axis=-1), axis=-1)[..., :-1]
    return jnp.einsum('bnkgts,bnskd->bntkgd', p.astype(v.dtype), v)


def mixer_block(x, gla_S0, rg_h0, rg_buf, swa_kbuf, swa_vbuf, g_mix, w_in, gla_w_gate2, gla_b_gate, gla_g_out,
                rg_conv_w, rg_conv_b, rg_w_a, rg_b_a, rg_w_x, rg_b_x, rg_lam, swa_g_q, swa_g_k, swa_sinks, w_out):
    f32 = jnp.float32
    B, T, _ = x.shape
    h = rmsnorm(x, g_mix) @ w_in
    gq, gk, gv, glr, gog, rx, ry, sq, sk, sv = split_cols(h)
    q = gq.reshape(B, T, GLA_HEADS, GLA_DK).astype(f32) * GLA_DK ** -0.5
    k = gk.reshape(B, T, GLA_HEADS, GLA_DK).astype(f32)
    v = gv.reshape(B, T, GLA_HEADS, GLA_DV).astype(f32)
    glog = jax.nn.log_sigmoid((glr @ gla_w_gate2 + gla_b_gate).astype(f32)).reshape(B, T, GLA_HEADS, GLA_DK) / GLA_TAU
    S, o = gla_scan(gla_S0.astype(f32), q, k, v, glog)
    o_gla = (rmsnorm(o, gla_g_out) * jax.nn.silu(gog.astype(f32)).reshape(B, T, GLA_HEADS, GLA_DV))
    o_gla = o_gla.reshape(B, T, GLA_V).astype(x.dtype)
    xc, rg_buf_new = causal_dwconv(rx, rg_buf, rg_conv_w, rg_conv_b)
    hr, h_last = rglru(xc, rg_h0.astype(f32), rg_w_a, rg_b_a, rg_w_x, rg_b_x, rg_lam)
    o_rg = (hr * jax.nn.gelu(ry.astype(f32))).astype(x.dtype)
    q = rmsnorm(sq.reshape(B, T, SWA_HEADS, HEAD_DIM), swa_g_q).reshape(B, T, SWA_KV, SWA_G, HEAD_DIM)
    k = rmsnorm(sk.reshape(B, T, SWA_KV, HEAD_DIM), swa_g_k)
    v = sv.reshape(B, T, SWA_KV, HEAD_DIM)
    slopes = alibi_slopes().reshape(SWA_KV, SWA_G)
    sinks = swa_sinks.reshape(SWA_KV, SWA_G)
    if swa_kbuf is None:
        NB = T // WINDOW
        qb = q.reshape(B, NB, WINDOW, SWA_KV, SWA_G, HEAD_DIM)

        def band(a):
            ab = a.reshape(B, NB, WINDOW, SWA_KV, HEAD_DIM)
            prev = jnp.concatenate([jnp.zeros_like(ab[:, :1]), ab[:, :-1]], axis=1)
            return jnp.concatenate([prev, ab], axis=2)

        valid = (jnp.arange(NB)[:, None] > 0) | (jnp.arange(2 * WINDOW)[None, :] >= WINDOW)
        o = swa_attend(qb, band(k), band(v), valid, slopes, sinks)
        kb_new, vb_new = k[:, -WINDOW:], v[:, -WINDOW:]
    else:
        kk = jnp.concatenate([swa_kbuf.astype(k.dtype), k], axis=1)
        vv = jnp.concatenate([swa_vbuf.astype(v.dtype), v], axis=1)
        valid = jnp.ones((1, WINDOW + T), bool)
        o = swa_attend(q[:, None], kk[:, None], vv[:, None], valid, slopes, sinks)
        kb_new, vb_new = kk[:, -WINDOW:], vv[:, -WINDOW:]
    o_swa = o.reshape(B, T, SWA_Q).astype(x.dtype)
    y = jnp.concatenate([o_gla, o_rg, o_swa], axis=-1) @ w_out
    return y, S.astype(x.dtype), h_last.astype(x.dtype), rg_buf_new, kb_new, vb_new


def mem_kv(mem, g_m, w_kv, g_k):
    B = mem.shape[0]
    kv = rmsnorm(mem, g_m) @ w_kv
    k, v = jnp.split(kv, 2, axis=-1)
    k = rmsnorm(k.reshape(B, -1, MEM_HEADS, MEM_HD), g_k)
    return k, v.reshape(B, -1, MEM_HEADS, MEM_HD)


def mem_attend(x, mk, mv, g_x, w_q, g_q, w_o):
    B, T, _ = x.shape
    q = rmsnorm((rmsnorm(x, g_x) @ w_q).reshape(B, T, MEM_HEADS, MEM_HD), g_q)
    s = jnp.einsum('bthd,bshd->bhts', q, mk.astype(q.dtype)).astype(jnp.float32) * MEM_HD ** -0.5
    p = jax.nn.softmax(s, axis=-1).astype(x.dtype)
    o = jnp.einsum('bhts,bshd->bthd', p, mv.astype(x.dtype)).reshape(B, T, MEM_HEADS * MEM_HD)
    return o @ w_o


def ffn(x, g, w1, w2):
    h = rmsnorm(x, g) @ w1
    return jnp.square(jax.nn.relu(h)) @ w2


def setup_inputs(seed: int = 0) -> dict:
    key = jax.random.key(seed)
    k = jax.random.split(key, 40)
    f32 = jnp.float32

    def nrm(kk, shape, scale):
        return jax.random.normal(kk, shape, f32) * scale

    def gain(kk, shape):
        return 1.0 + nrm(kk, shape, 0.02)

    u = jax.random.uniform(k[21], (DEPTH, RG_WIDTH), f32, minval=0.9, maxval=0.999)
    a_base = u ** (1.0 / RG_C)
    rg_lam = jnp.log(a_base) - jnp.log1p(-a_base)
    return {
        "x_prompt": nrm(k[0], (BATCH, SEQ, D_MODEL), 1.0),
        "x_sample": nrm(k[1], (DEC_BATCH, DEC_SEQ, D_MODEL), 1.0),
        "mem_prompt": nrm(k[2], (BATCH, N_MEM, D_MODEL), 1.0),
        "state_gla": nrm(k[3], (DEPTH, DEC_BATCH, GLA_HEADS, GLA_DK, GLA_DV), 0.3),
        "state_rg_h": nrm(k[4], (DEPTH, DEC_BATCH, RG_WIDTH), 0.5),
        "state_rg_conv": nrm(k[5], (DEPTH, DEC_BATCH, CONV_W - 1, RG_WIDTH), 1.0),
        "cache_swa_k": nrm(k[6], (DEPTH, DEC_BATCH, WINDOW, SWA_KV, HEAD_DIM), 1.0),
        "cache_swa_v": nrm(k[7], (DEPTH, DEC_BATCH, WINDOW, SWA_KV, HEAD_DIM), 1.0),
        "cache_mem_k": nrm(k[8], (DEPTH, DEC_BATCH, N_MEM, MEM_HEADS, MEM_HD), 1.0),
        "cache_mem_v": nrm(k[9], (DEPTH, DEC_BATCH, N_MEM, MEM_HEADS, MEM_HD), 1.0),
        "g_mix": gain(k[10], (DEPTH, D_MODEL)),
        "w_in": nrm(k[11], (DEPTH, D_MODEL, IN_WIDTH), D_MODEL ** -0.5),
        "gla_w_gate2": nrm(k[12], (DEPTH, GLA_RANK, GLA_QK), GLA_RANK ** -0.5),
        "gla_b_gate": nrm(k[13], (DEPTH, GLA_QK), 0.1),
        "gla_g_out": gain(k[14], (DEPTH, GLA_DV)),
        "rg_conv_w": nrm(k[15], (DEPTH, CONV_W, RG_WIDTH), CONV_W ** -0.5),
        "rg_conv_b": nrm(k[16], (DEPTH, RG_WIDTH), 0.02),
        "rg_w_a": nrm(k[17], (DEPTH, RG_BLOCKS, RG_BW, RG_BW), RG_BW ** -0.5),
        "rg_b_a": nrm(k[18], (DEPTH, RG_WIDTH), 0.02),
        "rg_w_x": nrm(k[19], (DEPTH, RG_BLOCKS, RG_BW, RG_BW), RG_BW ** -0.5),
        "rg_b_x": nrm(k[20], (DEPTH, RG_WIDTH), 0.02),
        "rg_lam": rg_lam,
        "swa_g_q": gain(k[22], (DEPTH, HEAD_DIM)),
        "swa_g_k": gain(k[23], (DEPTH, HEAD_DIM)),
        "swa_sinks": nrm(k[24], (DEPTH, SWA_HEADS), 0.5),
        "w_out": nrm(k[25], (DEPTH, MIX_WIDTH, D_MODEL), 0.5 * MIX_WIDTH ** -0.5),
        "g_mem_x": gain(k[26], (DEPTH, D_MODEL)),
        "g_mem_m": gain(k[27], (DEPTH, D_MODEL)),
        "mem_w_q": nrm(k[28], (DEPTH, D_MODEL, MEM_HEADS * MEM_HD), D_MODEL ** -0.5),
        "mem_w_kv": nrm(k[29], (DEPTH, D_MODEL, 2 * MEM_HEADS * MEM_HD), D_MODEL ** -0.5),
        "mem_g_q": gain(k[30], (DEPTH, MEM_HD)),
        "mem_g_k": gain(k[31], (DEPTH, MEM_HD)),
        "mem_w_o": nrm(k[32], (DEPTH, MEM_HEADS * MEM_HD, D_MODEL), 0.5 * (MEM_HEADS * MEM_HD) ** -0.5),
        "g_ffn": gain(k[33], (DEPTH, D_MODEL)),
        "ffn_w1": nrm(k[34], (DEPTH, D_MODEL, D_FF), D_MODEL ** -0.5),
        "ffn_w2": nrm(k[35], (DEPTH, D_FF, D_MODEL), 0.5 * D_FF ** -0.5),
    }


def reference(x_prompt, x_sample, mem_prompt, state_gla, state_rg_h, state_rg_conv, cache_swa_k, cache_swa_v,
              cache_mem_k, cache_mem_v, g_mix, w_in, gla_w_gate2, gla_b_gate, gla_g_out, rg_conv_w, rg_conv_b,
              rg_w_a, rg_b_a, rg_w_x, rg_b_x, rg_lam, swa_g_q, swa_g_k, swa_sinks, w_out, g_mem_x, g_mem_m,
              mem_w_q, mem_w_kv, mem_g_q, mem_g_k, mem_w_o, g_ffn, ffn_w1, ffn_w2):
    xp, xs = x_prompt, x_sample
    Bp = xp.shape[0]
    gla_p, gla_s, rgh_p, rgh_s, rgc_p, rgc_s = [], [], [], [], [], []
    swk_p, swk_s, swv_p, swv_s, memk_p, memv_p = [], [], [], [], [], []
    for l in range(DEPTH):
        mix_w = (g_mix[l], w_in[l], gla_w_gate2[l], gla_b_gate[l], gla_g_out[l], rg_conv_w[l], rg_conv_b[l],
                 rg_w_a[l], rg_b_a[l], rg_w_x[l], rg_b_x[l], rg_lam[l], swa_g_q[l], swa_g_k[l], swa_sinks[l], w_out[l])
        S0 = jnp.zeros((Bp, GLA_HEADS, GLA_DK, GLA_DV), jnp.float32)
        h0 = jnp.zeros((Bp, RG_WIDTH), jnp.float32)
        cb0 = jnp.zeros((Bp, CONV_W - 1, RG_WIDTH), xp.dtype)
        y, S, hl, cb, kb, vb = mixer_block(xp, S0, h0, cb0, None, None, *mix_w)
        xp = xp + y
        mk, mv = mem_kv(mem_prompt, g_mem_m[l], mem_w_kv[l], mem_g_k[l])
        xp = xp + mem_attend(xp, mk, mv, g_mem_x[l], mem_w_q[l], mem_g_q[l], mem_w_o[l])
        xp = xp + ffn(xp, g_ffn[l], ffn_w1[l], ffn_w2[l])
        gla_p.append(S); rgh_p.append(hl); rgc_p.append(cb); swk_p.append(kb); swv_p.append(vb)
        memk_p.append(mk); memv_p.append(mv)
        y, S, hl, cb, kb, vb = mixer_block(xs, state_gla[l], state_rg_h[l], state_rg_conv[l],
                                           cache_swa_k[l], cache_swa_v[l], *mix_w)
        xs = xs + y
        xs = xs + mem_attend(xs, cache_mem_k[l], cache_mem_v[l], g_mem_x[l], mem_w_q[l], mem_g_q[l], mem_w_o[l])
        xs = xs + ffn(xs, g_ffn[l], ffn_w1[l], ffn_w2[l])
        gla_s.append(S); rgh_s.append(hl); rgc_s.append(cb); swk_s.append(kb); swv_s.append(vb)
    return (xp, xs, jnp.stack(gla_p), jnp.stack(gla_s), jnp.stack(rgh_p), jnp.stack(rgh_s),
            jnp.stack(rgc_p), jnp.stack(rgc_s), jnp.stack(swk_p), jnp.stack(swk_s),
            jnp.stack(swv_p), jnp.stack(swv_s), jnp.stack(memk_p), jnp.stack(memv_p))
```

```python
import functools
import math

import jax
import jax.numpy as jnp
from jax import lax
from jax.experimental import pallas as pl
from jax.experimental.pallas import tpu as pltpu

F32 = jnp.float32
BF16 = jnp.bfloat16

D_MODEL = 1024
EPS = 1e-6
GLA_HEADS, GLA_DK, GLA_DV, GLA_RANK, GLA_TAU = 4, 48, 96, 16, 16.0
GLA_QK = GLA_HEADS * GLA_DK
GLA_V = GLA_HEADS * GLA_DV
RG_WIDTH, RG_BLOCKS, RG_C, CONV_W = 384, 4, 8.0, 4
RG_BW = RG_WIDTH // RG_BLOCKS
SWA_HEADS, SWA_KV, HEAD_DIM, WINDOW = 4, 2, 64, 128
SWA_G = SWA_HEADS // SWA_KV
SWA_Q = SWA_HEADS * HEAD_DIM
SWA_KVW = SWA_KV * HEAD_DIM
MEM_HEADS, MEM_HD = 4, 64
MEM_W = MEM_HEADS * MEM_HD
D_FF = 4 * D_MODEL

LANE = 128
GLA_DVP = LANE
GLA_VP = GLA_HEADS * GLA_DVP
GLA_QKP = 2 * LANE

C_RX, C_RY, C_GQ, C_GK, C_SQ, C_GV, C_GOG, C_SK, C_SV = 0, 384, 768, 1024, 1280, 1536, 2048, 2560, 2688
NP = 2816
GLR_LANE = GLA_QK

VMEM_LIMIT = 56 * 1024 * 1024
NEG = -1e30

GLA_T = 256
SWA_T = WINDOW
ROW_T = 512
POST_T = 256
FF_CHUNK = 1024


def _cparams(*sem):
    return pltpu.CompilerParams(dimension_semantics=sem, vmem_limit_bytes=VMEM_LIMIT)


def _rms(x, g):
    ms = jnp.mean(x * x, axis=-1, keepdims=True)
    return (x * lax.rsqrt(ms + EPS)) * g


def _dot(a, b):
    return jnp.dot(a.astype(BF16), b.astype(BF16), preferred_element_type=F32)


def _dot_nt(a, b):
    return lax.dot_general(a.astype(BF16), b.astype(BF16), (((1,), (1,)), ((), ())), preferred_element_type=F32)


def _split2_dot(x, m):
    hi = x.astype(BF16)
    lo = (x - hi.astype(F32)).astype(BF16)
    return jnp.dot(hi, m, preferred_element_type=F32) + jnp.dot(lo, m, preferred_element_type=F32)


def _split3_dot_left(m, x):
    x1 = x.astype(BF16)
    r1 = x - x1.astype(F32)
    x2 = r1.astype(BF16)
    x3 = (r1 - x2.astype(F32)).astype(BF16)
    d = functools.partial(jnp.dot, preferred_element_type=F32)
    return d(m, x1) + d(m, x2) + d(m, x3)


def _seg_rms(x, bones, seg, g):
    ms = _split2_dot(x * x, bones) * (1.0 / seg)
    return (x * lax.rsqrt(ms + EPS)) * g


def _sigmoid(x):
    return 1.0 / (1.0 + jnp.exp(-x))


def _softplus(x):
    return jnp.maximum(x, 0.0) + jnp.log1p(jnp.exp(-jnp.abs(x)))


def _gelu_tanh(x):
    return x * (0.5 * (1.0 + jnp.tanh(math.sqrt(2.0 / math.pi) * (x + 0.044715 * (x * x * x)))))


def _lane_head_mask(width, seg, h):
    lane = lax.broadcasted_iota(jnp.int32, (1, width), 1)
    return ((lane >= seg * h) & (lane < seg * (h + 1))).astype(F32)


def _masked_softmax_rows(s, sink):
    m = jnp.maximum(jnp.max(s, axis=-1, keepdims=True), sink)
    e = jnp.exp(s - m)
    den = jnp.sum(e, axis=-1, keepdims=True) + jnp.exp(sink - m)
    return e / den


def _softmax_rows(s):
    m = jnp.max(s, axis=-1, keepdims=True)
    e = jnp.exp(s - m)
    return e / jnp.sum(e, axis=-1, keepdims=True)


def _in_proj_body(x_ref, g_ref, w_ref, o_ref):
    xn = _rms(x_ref[...], g_ref[...])
    o_ref[...] = jnp.dot(xn.astype(BF16), w_ref[...], preferred_element_type=F32)


def _in_proj(x2d, g3, w3, layer, tm):
    m = x2d.shape[0]
    return pl.pallas_call(
        _in_proj_body,
        grid=(m // tm,),
        in_specs=[
            pl.BlockSpec((tm, D_MODEL), lambda i: (i, 0)),
            pl.BlockSpec((None, 1, D_MODEL), lambda i: (layer, 0, 0)),
            pl.BlockSpec((None, D_MODEL, NP), lambda i: (layer, 0, 0)),
        ],
        out_specs=pl.BlockSpec((tm, NP), lambda i: (i, 0)),
        out_shape=jax.ShapeDtypeStruct((m, NP), F32),
        compiler_params=_cparams("parallel"),
        name="in_proj",
    )(x2d, g3, w3)


def _ffn_body(x_ref, g_ref, w1_ref, w2_ref, o_ref):
    x = x_ref[...]
    xn = _rms(x, g_ref[...]).astype(BF16)
    acc = x
    for c in range(D_FF // FF_CHUNK):
        h = jnp.dot(xn, w1_ref[:, c * FF_CHUNK:(c + 1) * FF_CHUNK], preferred_element_type=F32)
        h = jnp.square(jnp.maximum(h, 0.0))
        acc = acc + jnp.dot(h.astype(BF16), w2_ref[c * FF_CHUNK:(c + 1) * FF_CHUNK, :], preferred_element_type=F32)
    o_ref[...] = acc


def _ffn(x2d, g3, w1, w2, layer, tm):
    m = x2d.shape[0]
    const = dict(pipeline_mode=pl.Buffered(1))
    return pl.pallas_call(
        _ffn_body,
        grid=(m // tm,),
        in_specs=[
            pl.BlockSpec((tm, D_MODEL), lambda i: (i, 0)),
            pl.BlockSpec((None, 1, D_MODEL), lambda i: (layer, 0, 0)),
            pl.BlockSpec((None, D_MODEL, D_FF), lambda i: (layer, 0, 0), **const),
            pl.BlockSpec((None, D_FF, D_MODEL), lambda i: (layer, 0, 0), **const),
        ],
        out_specs=pl.BlockSpec((tm, D_MODEL), lambda i: (i, 0)),
        out_shape=jax.ShapeDtypeStruct((m, D_MODEL), F32),
        compiler_params=_cparams("parallel"),
        name="ffn",
    )(x2d, g3, w1, w2)


def _mem_kv_body(mem_ref, g_ref, w_ref, gk_ref, bones_ref, k_ref, v_ref):
    kv = jnp.dot(_rms(mem_ref[...], g_ref[...]).astype(BF16), w_ref[...], preferred_element_type=F32)
    k_ref[...] = _seg_rms(kv[:, :MEM_W], bones_ref[...], MEM_HD, gk_ref[...])
    v_ref[...] = kv[:, MEM_W:]


def _mem_kv(mem2d, g3, wkv, gk3, bones256):
    depth = wkv.shape[0]
    m = mem2d.shape[0]
    return pl.pallas_call(
        _mem_kv_body,
        grid=(depth,),
        in_specs=[
            pl.BlockSpec((m, D_MODEL), lambda l: (0, 0)),
            pl.BlockSpec((None, 1, D_MODEL), lambda l: (l, 0, 0)),
            pl.BlockSpec((None, D_MODEL, 2 * MEM_W), lambda l: (l, 0, 0)),
            pl.BlockSpec((None, 1, MEM_W), lambda l: (l, 0, 0)),
            pl.BlockSpec((MEM_W, MEM_W), lambda l: (0, 0)),
        ],
        out_specs=[pl.BlockSpec((None, m, MEM_W), lambda l: (l, 0, 0))] * 2,
        out_shape=[jax.ShapeDtypeStruct((depth, m, MEM_W), F32)] * 2,
        compiler_params=_cparams("parallel"),
        name="mem_kv",
    )(mem2d, g3, wkv, gk3, bones256)


def _mix_out(x, og, orr, osw, wo_ref):
    y = _dot(og, wo_ref[0:GLA_VP, :])
    y = y + _dot(orr, wo_ref[GLA_VP:GLA_VP + RG_WIDTH, :])
    y = y + _dot(osw, wo_ref[GLA_VP + RG_WIDTH:GLA_VP + RG_WIDTH + SWA_Q, :])
    return x + y


def _mem_q(x1, gx, wq_ref, gq, bones):
    q = jnp.dot(_rms(x1, gx).astype(BF16), wq_ref[...], preferred_element_type=F32)
    return _seg_rms(q, bones, MEM_HD, gq)


def _post_body(x_ref, og_ref, or_ref, os_ref, wo_ref, gx_ref, wq_ref, gq_ref, bones_ref, mk_ref, mv_ref, wmo_ref,
               o_ref):
    x1 = _mix_out(x_ref[...], og_ref[...], or_ref[...], os_ref[...], wo_ref)
    qn = _mem_q(x1, gx_ref[...], wq_ref, gq_ref[...], bones_ref[...])
    mk = mk_ref[...].astype(BF16)
    mv = mv_ref[...].astype(BF16)
    o = jnp.zeros_like(qn)
    for h in range(MEM_HEADS):
        hm = _lane_head_mask(MEM_W, MEM_HD, h)
        s = _dot_nt(qn * hm, mk) * (MEM_HD ** -0.5)
        p = _softmax_rows(s)
        o = o + hm * jnp.dot(p.astype(BF16), mv, preferred_element_type=F32)
    o_ref[...] = x1 + jnp.dot(o.astype(BF16), wmo_ref[...], preferred_element_type=F32)


def _post_prompt(x2d, og, orr, osw, wo, gx3, wq, gq3, bones256, mk, mv, wmo, layer, nb, tm):
    m = x2d.shape[0]
    per_b = m // nb // tm
    row = lambda i: (i, 0)
    lay3 = lambda i: (layer, 0, 0)
    return pl.pallas_call(
        _post_body,
        grid=(m // tm,),
        in_specs=[
            pl.BlockSpec((tm, D_MODEL), row),
            pl.BlockSpec((tm, GLA_VP), row),
            pl.BlockSpec((tm, RG_WIDTH), row),
            pl.BlockSpec((tm, SWA_Q), row),
            pl.BlockSpec((None, GLA_VP + RG_WIDTH + SWA_Q, D_MODEL), lay3),
            pl.BlockSpec((None, 1, D_MODEL), lay3),
            pl.BlockSpec((None, D_MODEL, MEM_W), lay3),
            pl.BlockSpec((None, 1, MEM_W), lay3),
            pl.BlockSpec((MEM_W, MEM_W), lambda i: (0, 0)),
            pl.BlockSpec((None, None, mk.shape[2], MEM_W), lambda i: (layer, i // per_b, 0, 0)),
            pl.BlockSpec((None, None, mv.shape[2], MEM_W), lambda i: (layer, i // per_b, 0, 0)),
            pl.BlockSpec((None, MEM_W, D_MODEL), lay3),
        ],
        out_specs=pl.BlockSpec((tm, D_MODEL), row),
        out_shape=jax.ShapeDtypeStruct((m, D_MODEL), F32),
        compiler_params=_cparams("parallel"),
        name="post_prompt",
    )(x2d, og, orr, osw, wo, gx3, wq, gq3, bones256, mk, mv, wmo)


def _log_sigmoid(z):
    return -_softplus(-z)


def _gla_body(q_ref, k_ref, v_ref, gog_ref, wg_ref, bg_ref, gout_ref, o_ref, st_ref, st_scr, cum_scr):
    tt = q_ref.shape[0]
    t = pl.program_id(1)

    @pl.when(t == 0)
    def _():
        st_scr[...] = jnp.zeros_like(st_scr)

    qraw = q_ref[...]
    k = k_ref[...]
    v = v_ref[...]
    z = jnp.dot(qraw.astype(BF16), wg_ref[...], preferred_element_type=F32) + bg_ref[...]
    g = _log_sigmoid(z) * (1.0 / GLA_TAU)
    q = qraw * (GLA_DK ** -0.5)

    row = lax.broadcasted_iota(jnp.int32, (tt, tt), 0)
    col = lax.broadcasted_iota(jnp.int32, (tt, tt), 1)
    tri = (row >= col).astype(BF16)
    cum = _split3_dot_left(tri, g)
    cum_scr[...] = cum
    rowi = lax.broadcasted_iota(jnp.int32, (tt, 1), 0)
    coli = lax.broadcasted_iota(jnp.int32, (1, tt), 1)

    levels = [(q, k, 0)]
    rd1 = pltpu.roll(cum, 1, 0)
    rd2 = pltpu.roll(cum, 2, 0)
    ru1 = pltpu.roll(cum, tt - 1, 0)
    half = 1
    shift = 1
    while half < tt:
        if half == 1:
            c = jnp.where((rowi & 1) == 0, cum, rd1)
        elif half == 2:
            m4 = rowi & 3
            c = jnp.where(m4 == 0, ru1, jnp.where(m4 == 1, cum, jnp.where(m4 == 2, rd1, rd2)))
        else:
            grp = 2 * half
            c = jnp.concatenate(
                [jnp.broadcast_to(cum_scr[pl.ds(gi * grp + half - 1, 1), :], (grp, GLA_QKP))
                 for gi in range(tt // grp)], axis=0)
        second = (rowi & (2 * half - 1)) >= half
        qs = q * jnp.exp(jnp.where(second, cum - c, NEG))
        ks = k * jnp.exp(jnp.where(second, NEG, c - cum))
        levels.append((qs, ks, shift))
        half *= 2
        shift += 1

    st = st_scr[...]
    o_inter = _dot_nt(q * jnp.exp(cum), st)
    last = cum_scr[pl.ds(tt - 1, 1), :]
    kd = k * jnp.exp(last - cum)
    upd = jnp.dot(v.T.astype(BF16), kd.astype(BF16), preferred_element_type=F32)
    vrow = lax.broadcasted_iota(jnp.int32, (GLA_VP, 1), 0) // GLA_DVP
    lane = lax.broadcasted_iota(jnp.int32, (1, GLA_QKP), 1)
    khead = ((lane >= GLA_DK).astype(jnp.int32) + (lane >= 2 * GLA_DK).astype(jnp.int32)
             + (lane >= 3 * GLA_DK).astype(jnp.int32) + (lane >= 4 * GLA_DK).astype(jnp.int32))
    st_new = jnp.where(vrow == khead, st * jnp.exp(last) + upd, 0.0)
    st_scr[...] = st_new
    st_ref[...] = st_new

    ks_bf = [lv[1].astype(BF16) for lv in levels]
    masks = [None if lv[2] >= int(math.log2(tt)) else ((rowi >> lv[2]) == (coli >> lv[2])) for lv in levels]
    gout = gout_ref[...]
    for h in range(GLA_HEADS):
        hm = _lane_head_mask(GLA_QKP, GLA_DK, h)
        a = jnp.zeros((tt, tt), F32)
        for (qs, _, _), kb, mk in zip(levels, ks_bf, masks):
            p = lax.dot_general((qs * hm).astype(BF16), kb, (((1,), (1,)), ((), ())), preferred_element_type=F32)
            a = a + (p if mk is None else jnp.where(mk, p, 0.0))
        sl = slice(h * GLA_DVP, (h + 1) * GLA_DVP)
        o = _dot(a, v[:, sl]) + o_inter[:, sl]
        ms = jnp.sum(o * o, axis=-1, keepdims=True) * (1.0 / GLA_DV)
        gate = gog_ref[:, sl]
        o_ref[:, sl] = ((o * lax.rsqrt(ms + EPS)) * gout) * (gate * _sigmoid(gate))


def _gla_prompt(h2d, wg, bg3, gout3, layer, nb, t_len):
    nt = t_len // GLA_T
    blk = lambda w, c: pl.BlockSpec((GLA_T, w), lambda b, t: (b * nt + t, c // w))
    lay3 = lambda b, t: (layer, 0, 0)
    return pl.pallas_call(
        _gla_body,
        grid=(nb, nt),
        in_specs=[
            blk(GLA_QKP, C_GQ), blk(GLA_QKP, C_GK), blk(GLA_VP, C_GV), blk(GLA_VP, C_GOG),
            pl.BlockSpec((None, GLA_QKP, GLA_QKP), lay3),
            pl.BlockSpec((None, 1, GLA_QKP), lay3),
            pl.BlockSpec((None, 1, GLA_DVP), lay3),
        ],
        out_specs=[
            pl.BlockSpec((GLA_T, GLA_VP), lambda b, t: (b * nt + t, 0)),
            pl.BlockSpec((None, GLA_VP, GLA_QKP), lambda b, t: (b, 0, 0)),
        ],
        out_shape=[
            jax.ShapeDtypeStruct((nb * t_len, GLA_VP), F32),
            jax.ShapeDtypeStruct((nb, GLA_VP, GLA_QKP), F32),
        ],
        scratch_shapes=[pltpu.VMEM((GLA_VP, GLA_QKP), F32), pltpu.VMEM((GLA_T, GLA_QKP), F32)],
        compiler_params=_cparams("arbitrary", "arbitrary"),
        name="gla_prompt",
    )(h2d, h2d, h2d, h2d, wg, bg3, gout3)


def _rg_gates(xc, wa_ref, ba, wx_ref, bx, lam):
    r = _sigmoid(jnp.dot(xc.astype(BF16), wa_ref[...], preferred_element_type=F32) + ba)
    i = _sigmoid(jnp.dot(xc.astype(BF16), wx_ref[...], preferred_element_type=F32) + bx)
    log_a = (-RG_C * r) * _softplus(-lam)
    a = jnp.exp(log_a)
    b = jnp.sqrt(1.0 - jnp.exp(2.0 * log_a)) * (i * xc)
    return a, b


def _rg_body(rx_ref, ry_ref, cw_ref, cb_ref, wa_ref, ba_ref, wx_ref, bx_ref, lam_ref, o_ref, hl_ref, cbuf_ref,
             h_scr, c_scr):
    tt = rx_ref.shape[0]
    t = pl.program_id(1)

    @pl.when(t == 0)
    def _():
        h_scr[...] = jnp.zeros_like(h_scr)
        c_scr[...] = jnp.zeros_like(c_scr)

    rx = rx_ref[...]
    rowi = lax.broadcasted_iota(jnp.int32, (tt, 1), 0)
    p0, p1, p2 = c_scr[0:1, :], c_scr[1:2, :], c_scr[2:3, :]
    x1 = jnp.where(rowi == 0, p2, pltpu.roll(rx, 1, 0))
    x2 = jnp.where(rowi == 0, p1, jnp.where(rowi == 1, p2, pltpu.roll(rx, 2, 0)))
    x3 = jnp.where(rowi == 0, p0, jnp.where(rowi == 1, p1, jnp.where(rowi == 2, p2, pltpu.roll(rx, 3, 0))))
    cw = cw_ref[...]
    xc = cb_ref[...] + x3 * cw[0:1, :]
    xc = xc + x2 * cw[1:2, :]
    xc = xc + x1 * cw[2:3, :]
    xc = xc + rx * cw[3:4, :]

    a, b = _rg_gates(xc, wa_ref, ba_ref[...], wx_ref, bx_ref[...], lam_ref[...])
    d = 1
    while d < tt:
        valid = rowi >= d
        b = b + a * jnp.where(valid, pltpu.roll(b, d, 0), 0.0)
        a = a * jnp.where(valid, pltpu.roll(a, d, 0), 1.0)
        d *= 2
    h = b + a * h_scr[0:1, :]
    o_ref[...] = h * _gelu_tanh(ry_ref[...])

    h_last = h[tt - 1:tt, :]
    h_scr[0:1, :] = h_last
    hl_ref[...] = h_last
    tail = rx_ref[pl.ds(tt - (CONV_W - 1), CONV_W - 1), :]
    c_scr[0:CONV_W - 1, :] = tail
    cbuf_ref[...] = tail


def _rg_prompt(h2d, cw, cb3, wa, ba3, wx, bx3, lam3, layer, nb, t_len):
    nt = t_len // GLA_T
    blk = lambda c: pl.BlockSpec((GLA_T, RG_WIDTH), lambda b, t: (b * nt + t, c // RG_WIDTH))
    lay3 = lambda b, t: (layer, 0, 0)
    vec = pl.BlockSpec((None, 1, RG_WIDTH), lay3)
    sq = pl.BlockSpec((None, RG_WIDTH, RG_WIDTH), lay3)
    return pl.pallas_call(
        _rg_body,
        grid=(nb, nt),
        in_specs=[blk(C_RX), blk(C_RY), pl.BlockSpec((None, CONV_W, RG_WIDTH), lay3), vec, sq, vec, sq, vec, vec],
        out_specs=[
            pl.BlockSpec((GLA_T, RG_WIDTH), lambda b, t: (b * nt + t, 0)),
            pl.BlockSpec((None, 1, RG_WIDTH), lambda b, t: (b, 0, 0)),
            pl.BlockSpec((None, CONV_W - 1, RG_WIDTH), lambda b, t: (b, 0, 0)),
        ],
        out_shape=[
            jax.ShapeDtypeStruct((nb * t_len, RG_WIDTH), F32),
            jax.ShapeDtypeStruct((nb, 1, RG_WIDTH), F32),
            jax.ShapeDtypeStruct((nb, CONV_W - 1, RG_WIDTH), F32),
        ],
        scratch_shapes=[pltpu.VMEM((8, RG_WIDTH), F32), pltpu.VMEM((8, RG_WIDTH), F32)],
        compiler_params=_cparams("arbitrary", "arbitrary"),
        name="rg_prompt",
    )(h2d, h2d, cw, cb3, wa, ba3, wx, bx3, lam3)


def _alibi_slope(h):
    return 2.0 ** (-8.0 * (h + 1) / SWA_HEADS)


def _swa_body(sink_ref, q_ref, k_ref, v_ref, gq_ref, gk_ref, b256_ref, b128_ref, dup_ref, o_ref, ko_ref, vo_ref,
              kp_scr, vp_scr, *, layer):
    tt = q_ref.shape[0]
    t = pl.program_id(1)

    @pl.when(t == 0)
    def _():
        kp_scr[...] = jnp.zeros_like(kp_scr)
        vp_scr[...] = jnp.zeros_like(vp_scr)

    qn = _seg_rms(q_ref[...], b256_ref[...], HEAD_DIM, gq_ref[...])
    kn = _seg_rms(k_ref[...], b128_ref[...], HEAD_DIM, gk_ref[...])
    v = v_ref[...]
    dup = dup_ref[...]
    kcat = jnp.concatenate([kp_scr[...], kn], axis=0).astype(BF16)
    vcat = jnp.concatenate([vp_scr[...], v], axis=0).astype(BF16)
    kexp = jnp.dot(kcat, dup, preferred_element_type=F32).astype(BF16)
    vexp = jnp.dot(vcat, dup, preferred_element_type=F32).astype(BF16)

    qi = lax.broadcasted_iota(jnp.int32, (tt, 2 * tt), 0)
    kj = lax.broadcasted_iota(jnp.int32, (tt, 2 * tt), 1)
    dist_i = qi + tt - kj
    allowed = (dist_i >= 0) & (dist_i <= WINDOW) & ((kj >= tt) | (t > 0))
    dist = dist_i.astype(F32)
    o = jnp.zeros((tt, SWA_Q), F32)
    for h in range(SWA_HEADS):
        hm = _lane_head_mask(SWA_Q, HEAD_DIM, h)
        s = lax.dot_general((qn * hm).astype(BF16), kexp, (((1,), (1,)), ((), ())), preferred_element_type=F32)
        s = s * (HEAD_DIM ** -0.5) - _alibi_slope(h) * dist
        s = jnp.where(allowed, s, NEG)
        p = _masked_softmax_rows(s, sink_ref[layer, h])
        o = o + hm * jnp.dot(p.astype(BF16), vexp, preferred_element_type=F32)
    o_ref[...] = o
    kp_scr[...] = kn
    vp_scr[...] = v
    ko_ref[...] = kn
    vo_ref[...] = v


def _swa_prompt(sinks, h2d, gq3, gk3, b256, b128, dup, layer, nb, t_len):
    nt = t_len // SWA_T
    blk = lambda w, c: pl.BlockSpec((SWA_T, w), lambda b, t: (b * nt + t, c // w))
    lay3 = lambda b, t: (layer, 0, 0)
    full = lambda a: pl.BlockSpec(a.shape, lambda b, t: (0, 0))
    return pl.pallas_call(
        functools.partial(_swa_body, layer=layer),
        grid=(nb, nt),
        in_specs=[
            pl.BlockSpec(memory_space=pltpu.SMEM),
            blk(SWA_Q, C_SQ), blk(SWA_KVW, C_SK), blk(SWA_KVW, C_SV),
            pl.BlockSpec((None, 1, SWA_Q), lay3), pl.BlockSpec((None, 1, SWA_KVW), lay3),
            full(b256), full(b128), full(dup),
        ],
        out_specs=[
            pl.BlockSpec((SWA_T, SWA_Q), lambda b, t: (b * nt + t, 0)),
            pl.BlockSpec((None, SWA_T, SWA_KVW), lambda b, t: (b, 0, 0)),
            pl.BlockSpec((None, SWA_T, SWA_KVW), lambda b, t: (b, 0, 0)),
        ],
        out_shape=[
            jax.ShapeDtypeStruct((nb * t_len, SWA_Q), F32),
            jax.ShapeDtypeStruct((nb, SWA_T, SWA_KVW), F32),
            jax.ShapeDtypeStruct((nb, SWA_T, SWA_KVW), F32),
        ],
        scratch_shapes=[pltpu.VMEM((SWA_T, SWA_KVW), F32), pltpu.VMEM((SWA_T, SWA_KVW), F32)],
        compiler_params=_cparams("arbitrary", "arbitrary"),
        name="swa_prompt",
    )(sinks, h2d, h2d, h2d, gq3, gk3, b256, b128, dup)


def _sample_rows_body(h_ref, hst_ref, cbuf_ref, wg_ref, bg_ref, cw_ref, cb_ref, wa_ref, ba_ref, wx_ref, bx_ref,
                      lam_ref, gq_ref, gk_ref, b256_ref, b128_ref, dup_ref,
                      qs_ref, kg_ref, eg_ref, org_ref, hn_ref, cnew_ref, qn_ref, kn_ref, kx_ref, vx_ref):
    qraw = h_ref[:, C_GQ:C_GQ + GLA_QKP]
    z = jnp.dot(qraw.astype(BF16), wg_ref[...], preferred_element_type=F32) + bg_ref[...]
    eg_ref[...] = jnp.exp(_log_sigmoid(z) * (1.0 / GLA_TAU))
    qs_ref[...] = qraw * (GLA_DK ** -0.5)
    kg_ref[...] = h_ref[:, C_GK:C_GK + GLA_QKP]

    rx = h_ref[:, C_RX:C_RX + RG_WIDTH]
    cw = cw_ref[...]
    xc = cb_ref[...] + cbuf_ref[:, 0:RG_WIDTH] * cw[0:1, :]
    xc = xc + cbuf_ref[:, RG_WIDTH:2 * RG_WIDTH] * cw[1:2, :]
    xc = xc + cbuf_ref[:, 2 * RG_WIDTH:3 * RG_WIDTH] * cw[2:3, :]
    xc = xc + rx * cw[3:4, :]
    a, b = _rg_gates(xc, wa_ref, ba_ref[...], wx_ref, bx_ref[...], lam_ref[...])
    hn = a * hst_ref[...] + b
    hn_ref[...] = hn
    org_ref[...] = hn * _gelu_tanh(h_ref[:, C_RY:C_RY + RG_WIDTH])
    cnew_ref[:, 0:2 * RG_WIDTH] = cbuf_ref[:, RG_WIDTH:3 * RG_WIDTH]
    cnew_ref[:, 2 * RG_WIDTH:3 * RG_WIDTH] = rx

    qn_ref[...] = _seg_rms(h_ref[:, C_SQ:C_SQ + SWA_Q], b256_ref[...], HEAD_DIM, gq_ref[...])
    kn = _seg_rms(h_ref[:, C_SK:C_SK + SWA_KVW], b128_ref[...], HEAD_DIM, gk_ref[...])
    kn_ref[...] = kn
    kx_ref[...] = jnp.dot(kn.astype(BF16), dup_ref[...], preferred_element_type=F32)
    vx_ref[...] = jnp.dot(h_ref[:, C_SV:C_SV + SWA_KVW].astype(BF16), dup_ref[...], preferred_element_type=F32)


def _sample_rows(hs, hst, cbuf, wg, bg3, cw, cb3, wa, ba3, wx, bx3, lam3, gq3, gk3, b256, b128, dup, layer):
    n = hs.shape[0]
    lay3 = lambda i: (layer, 0, 0)
    full = lambda a: pl.BlockSpec(a.shape, lambda i: (0,) * a.ndim)
    vec = lambda w: pl.BlockSpec((None, 1, w), lay3)
    sq = lambda w: pl.BlockSpec((None, w, w), lay3)
    outs = [GLA_QKP, GLA_QKP, GLA_QKP, RG_WIDTH, RG_WIDTH, 3 * RG_WIDTH, SWA_Q, SWA_KVW, SWA_Q, SWA_Q]
    return pl.pallas_call(
        _sample_rows_body,
        grid=(1,),
        in_specs=[
            full(hs), pl.BlockSpec((None, n, RG_WIDTH), lay3), pl.BlockSpec((None, n, 3 * RG_WIDTH), lay3),
            sq(GLA_QKP), vec(GLA_QKP), pl.BlockSpec((None, CONV_W, RG_WIDTH), lay3), vec(RG_WIDTH),
            sq(RG_WIDTH), vec(RG_WIDTH), sq(RG_WIDTH), vec(RG_WIDTH), vec(RG_WIDTH),
            vec(SWA_Q), vec(SWA_KVW), full(b256), full(b128), full(dup),
        ],
        out_specs=[pl.BlockSpec((n, w), lambda i: (0, 0)) for w in outs],
        out_shape=[jax.ShapeDtypeStruct((n, w), F32) for w in outs],
        compiler_params=_cparams("arbitrary"),
        name="sample_rows",
    )(hs, hst, cbuf, wg, bg3, cw, cb3, wa, ba3, wx, bx3, lam3, gq3, gk3, b256, b128, dup)


def _sample_gla_body(s0_ref, qs_ref, kg_ref, eg_ref, h_ref, gout_ref, eye_ref, o_ref, sn_ref):
    bt = s0_ref.shape[0]
    eye = eye_ref[...]
    qt = _split3_dot_left_nt(eye, qs_ref[...])
    kt = _split3_dot_left_nt(eye, kg_ref[...])
    et = _split3_dot_left_nt(eye, eg_ref[...])
    gout = gout_ref[:, 0:GLA_DV]
    o_ref[...] = jnp.zeros_like(o_ref)
    for j in range(bt):
        for h in range(GLA_HEADS):
            rs = slice(h * GLA_DK, (h + 1) * GLA_DK)
            vrow = h_ref[j:j + 1, C_GV + h * GLA_DVP:C_GV + h * GLA_DVP + GLA_DV]
            sn = et[rs, j:j + 1] * s0_ref[j, rs, :] + kt[rs, j:j + 1] * vrow
            sn_ref[j, rs, :] = sn
            o = jnp.sum(qt[rs, j:j + 1] * sn, axis=0, keepdims=True)
            ms = jnp.sum(o * o, axis=-1, keepdims=True) * (1.0 / GLA_DV)
            gate = h_ref[j:j + 1, C_GOG + h * GLA_DVP:C_GOG + h * GLA_DVP + GLA_DV]
            o_ref[j:j + 1, h * GLA_DVP:h * GLA_DVP + GLA_DV] = ((o * lax.rsqrt(ms + EPS)) * gout) * (
                gate * _sigmoid(gate))


def _split3_dot_left_nt(m, x):
    x1 = x.astype(BF16)
    r1 = x - x1.astype(F32)
    x2 = r1.astype(BF16)
    x3 = (r1 - x2.astype(F32)).astype(BF16)
    d = functools.partial(lax.dot_general, dimension_numbers=(((1,), (1,)), ((), ())), preferred_element_type=F32)
    return d(m, x1) + d(m, x2) + d(m, x3)


def _sample_gla(s0, qs, kg, eg, hs, gout3, eye256, layer, bt):
    n = hs.shape[0]
    row = lambda w: pl.BlockSpec((bt, w), lambda i: (i, 0))
    return pl.pallas_call(
        _sample_gla_body,
        grid=(n // bt,),
        in_specs=[
            pl.BlockSpec((None, bt, GLA_QK, GLA_DV), lambda i: (layer, i, 0, 0)),
            row(GLA_QKP), row(GLA_QKP), row(GLA_QKP), row(NP),
            pl.BlockSpec((None, 1, GLA_DVP), lambda i: (layer, 0, 0)),
            pl.BlockSpec((GLA_QKP, GLA_QKP), lambda i: (0, 0)),
        ],
        out_specs=[row(GLA_VP), pl.BlockSpec((bt, GLA_QK, GLA_DV), lambda i: (i, 0, 0))],
        out_shape=[jax.ShapeDtypeStruct((n, GLA_VP), F32), jax.ShapeDtypeStruct((n, GLA_QK, GLA_DV), F32)],
        compiler_params=_cparams("parallel"),
        name="sample_gla",
    )(s0, qs, kg, eg, hs, gout3, eye256)


def _sample_swa_body(sink_ref, kc_ref, vc_ref, qn_ref, kn_ref, kx_ref, vx_ref, h_ref, dup_ref, o_ref, ko_ref,
                     vo_ref, *, layer):
    bt = kc_ref.shape[0]
    dup = dup_ref[...]
    hi = lax.broadcasted_iota(jnp.int32, (SWA_HEADS, SWA_Q), 0)
    li = lax.broadcasted_iota(jnp.int32, (SWA_HEADS, SWA_Q), 1)
    hm4 = ((li >= hi * HEAD_DIM) & (li < (hi + 1) * HEAD_DIM)).astype(F32)
    hrow = lax.broadcasted_iota(jnp.int32, (SWA_HEADS, 1), 0)
    slope = jnp.zeros((SWA_HEADS, 1), F32)
    sink = jnp.zeros((SWA_HEADS, 1), F32)
    for h in range(SWA_HEADS):
        slope = jnp.where(hrow == h, _alibi_slope(h), slope)
        sink = jnp.where(hrow == h, sink_ref[layer, h], sink)
    dist = (WINDOW - lax.broadcasted_iota(jnp.int32, (1, WINDOW), 1)).astype(F32)
    for j in range(bt):
        kexp = jnp.dot(kc_ref[j].astype(BF16), dup, preferred_element_type=F32).astype(BF16)
        vexp = jnp.dot(vc_ref[j].astype(BF16), dup, preferred_element_type=F32).astype(BF16)
        q4 = (qn_ref[j:j + 1, :] * hm4).astype(BF16)
        s = lax.dot_general(q4, kexp, (((1,), (1,)), ((), ())), preferred_element_type=F32)
        s = s * (HEAD_DIM ** -0.5) - slope * dist
        s_new = jnp.sum(q4.astype(F32) * kx_ref[j:j + 1, :], axis=-1, keepdims=True) * (HEAD_DIM ** -0.5)
        m = jnp.maximum(jnp.maximum(jnp.max(s, axis=-1, keepdims=True), s_new), sink)
        e = jnp.exp(s - m)
        e_new = jnp.exp(s_new - m)
        den = jnp.sum(e, axis=-1, keepdims=True) + e_new + jnp.exp(sink - m)
        o4 = jnp.dot((e / den).astype(BF16), vexp, preferred_element_type=F32)
        o4 = o4 + (e_new / den).astype(BF16).astype(F32) * vx_ref[j:j + 1, :]
        o_ref[j:j + 1, :] = jnp.sum(o4 * hm4, axis=0, keepdims=True)
        ko_ref[j, 0:WINDOW - 1, :] = kc_ref[j, pl.ds(1, WINDOW - 1), :]
        ko_ref[j, WINDOW - 1:WINDOW, :] = kn_ref[j:j + 1, :]
        vo_ref[j, 0:WINDOW - 1, :] = vc_ref[j, pl.ds(1, WINDOW - 1), :]
        vo_ref[j, WINDOW - 1:WINDOW, :] = h_ref[j:j + 1, C_SV:C_SV + SWA_KVW]


def _sample_swa(sinks, kc, vc, qn, kn, kx, vx, hs, dup, layer, bt):
    n = hs.shape[0]
    row = lambda w: pl.BlockSpec((bt, w), lambda i: (i, 0))
    cache = pl.BlockSpec((None, bt, WINDOW, SWA_KVW), lambda i: (layer, i, 0, 0))
    cache_o = pl.BlockSpec((bt, WINDOW, SWA_KVW), lambda i: (i, 0, 0))
    return pl.pallas_call(
        functools.partial(_sample_swa_body, layer=layer),
        grid=(n // bt,),
        in_specs=[
            pl.BlockSpec(memory_space=pltpu.SMEM), cache, cache,
            row(SWA_Q), row(SWA_KVW), row(SWA_Q), row(SWA_Q), row(NP),
            pl.BlockSpec(dup.shape, lambda i: (0, 0)),
        ],
        out_specs=[row(SWA_Q), cache_o, cache_o],
        out_shape=[
            jax.ShapeDtypeStruct((n, SWA_Q), F32),
            jax.ShapeDtypeStruct((n, WINDOW, SWA_KVW), F32),
            jax.ShapeDtypeStruct((n, WINDOW, SWA_KVW), F32),
        ],
        compiler_params=_cparams("parallel"),
        name="sample_swa",
    )(sinks, kc, vc, qn, kn, kx, vx, hs, dup)


def _sample_post1_body(x_ref, og_ref, or_ref, os_ref, wo_ref, gx_ref, wq_ref, gq_ref, bones_ref, x1_ref, qn_ref):
    x1 = _mix_out(x_ref[...], og_ref[...], or_ref[...], os_ref[...], wo_ref)
    x1_ref[...] = x1
    qn_ref[...] = _mem_q(x1, gx_ref[...], wq_ref, gq_ref[...], bones_ref[...])


def _sample_post1(x2d, og, orr, osw, wo, gx3, wq, gq3, bones256, layer):
    n = x2d.shape[0]
    lay3 = lambda i: (layer, 0, 0)
    full = lambda a: pl.BlockSpec(a.shape, lambda i: (0,) * a.ndim)
    return pl.pallas_call(
        _sample_post1_body,
        grid=(1,),
        in_specs=[
            full(x2d), full(og), full(orr), full(osw),
            pl.BlockSpec((None, GLA_VP + RG_WIDTH + SWA_Q, D_MODEL), lay3),
            pl.BlockSpec((None, 1, D_MODEL), lay3),
            pl.BlockSpec((None, D_MODEL, MEM_W), lay3),
            pl.BlockSpec((None, 1, MEM_W), lay3),
            full(bones256),
        ],
        out_specs=[pl.BlockSpec((n, D_MODEL), lambda i: (0, 0)), pl.BlockSpec((n, MEM_W), lambda i: (0, 0))],
        out_shape=[jax.ShapeDtypeStruct((n, D_MODEL), F32), jax.ShapeDtypeStruct((n, MEM_W), F32)],
        compiler_params=_cparams("arbitrary"),
        name="sample_post1",
    )(x2d, og, orr, osw, wo, gx3, wq, gq3, bones256)


def _sample_mem_body(qn_ref, mk_ref, mv_ref, o_ref):
    bt = mk_ref.shape[0]
    hi = lax.broadcasted_iota(jnp.int32, (MEM_HEADS, MEM_W), 0)
    li = lax.broadcasted_iota(jnp.int32, (MEM_HEADS, MEM_W), 1)
    hm4 = ((li >= hi * MEM_HD) & (li < (hi + 1) * MEM_HD)).astype(F32)
    for j in range(bt):
        q4 = (qn_ref[j:j + 1, :] * hm4).astype(BF16)
        s = lax.dot_general(q4, mk_ref[j].astype(BF16), (((1,), (1,)), ((), ())), preferred_element_type=F32)
        p = _softmax_rows(s * (MEM_HD ** -0.5))
        o4 = jnp.dot(p.astype(BF16), mv_ref[j].astype(BF16), preferred_element_type=F32)
        o_ref[j:j + 1, :] = jnp.sum(o4 * hm4, axis=0, keepdims=True)


def _sample_mem(qn, mk, mv, layer, bt):
    n = qn.shape[0]
    n_mem = mk.shape[2]
    cache = pl.BlockSpec((None, bt, n_mem, MEM_W), lambda i: (layer, i, 0, 0))
    return pl.pallas_call(
        _sample_mem_body,
        grid=(n // bt,),
        in_specs=[pl.BlockSpec((bt, MEM_W), lambda i: (i, 0)), cache, cache],
        out_specs=pl.BlockSpec((bt, MEM_W), lambda i: (i, 0)),
        out_shape=jax.ShapeDtypeStruct((n, MEM_W), F32),
        compiler_params=_cparams("parallel"),
        name="sample_mem",
    )(qn, mk, mv)


def _ffn_res_body(x_ref, o_ref_in, wmo_ref, g_ref, w1_ref, w2_ref, out_ref):
    x = x_ref[...] + jnp.dot(o_ref_in[...].astype(BF16), wmo_ref[...], preferred_element_type=F32)
    xn = _rms(x, g_ref[...]).astype(BF16)
    acc = x
    for c in range(D_FF // FF_CHUNK):
        h = jnp.dot(xn, w1_ref[:, c * FF_CHUNK:(c + 1) * FF_CHUNK], preferred_element_type=F32)
        h = jnp.square(jnp.maximum(h, 0.0))
        acc = acc + jnp.dot(h.astype(BF16), w2_ref[c * FF_CHUNK:(c + 1) * FF_CHUNK, :], preferred_element_type=F32)
    out_ref[...] = acc


def _sample_ffn(x1, o, wmo, g3, w1, w2, layer):
    n = x1.shape[0]
    lay3 = lambda i: (layer, 0, 0)
    const = dict(pipeline_mode=pl.Buffered(1))
    return pl.pallas_call(
        _ffn_res_body,
        grid=(1,),
        in_specs=[
            pl.BlockSpec((n, D_MODEL), lambda i: (0, 0)),
            pl.BlockSpec((n, MEM_W), lambda i: (0, 0)),
            pl.BlockSpec((None, MEM_W, D_MODEL), lay3),
            pl.BlockSpec((None, 1, D_MODEL), lay3),
            pl.BlockSpec((None, D_MODEL, D_FF), lay3, **const),
            pl.BlockSpec((None, D_FF, D_MODEL), lay3, **const),
        ],
        out_specs=pl.BlockSpec((n, D_MODEL), lambda i: (0, 0)),
        out_shape=jax.ShapeDtypeStruct((n, D_MODEL), F32),
        compiler_params=_cparams("arbitrary"),
        name="sample_ffn",
    )(x1, o, wmo, g3, w1, w2)


def _pad_heads(a, axis, heads, width, padded):
    shp = a.shape
    a = a.reshape(shp[:axis] + (heads, width) + shp[axis + 1:])
    pad = [(0, 0)] * a.ndim
    pad[axis + 1] = (0, padded - width)
    a = jnp.pad(a, pad)
    return a.reshape(shp[:axis] + (heads * padded,) + shp[axis + 1:])


def _prep_w_in(w_in):
    offs = [0]
    for w in (GLA_QK, GLA_QK, GLA_V, GLA_RANK, GLA_V, RG_WIDTH, RG_WIDTH, SWA_Q, SWA_KVW, SWA_KVW):
        offs.append(offs[-1] + w)
    gq, gk, gv, glr, gog, rx, ry, sq, sk, sv = [w_in[..., offs[i]:offs[i + 1]] for i in range(10)]
    zeros = lambda n: jnp.zeros(w_in.shape[:-1] + (n,), w_in.dtype)
    cols = [rx, ry,
            gq, glr, zeros(GLA_QKP - GLA_QK - GLA_RANK),
            gk, zeros(GLA_QKP - GLA_QK),
            sq,
            _pad_heads(gv, 2, GLA_HEADS, GLA_DV, GLA_DVP),
            _pad_heads(gog, 2, GLA_HEADS, GLA_DV, GLA_DVP),
            sk, sv]
    out = jnp.concatenate(cols, axis=-1)
    assert out.shape[-1] == NP
    return out.astype(BF16)


def _block_diag(w):
    depth, nb, bw, _ = w.shape
    eye = jnp.eye(nb, dtype=w.dtype)
    return jnp.einsum("lncd,nm->lncmd", w, eye).reshape(depth, nb * bw, nb * bw)


def _block_ones(n, seg):
    i = jnp.arange(n) // seg
    return (i[:, None] == i[None, :]).astype(BF16)


def _dup_matrix():
    src = jnp.arange(SWA_Q)
    src = (src // HEAD_DIM // SWA_G) * HEAD_DIM + src % HEAD_DIM
    return (jnp.arange(SWA_KVW)[:, None] == src[None, :]).astype(BF16)


def kernel(x_prompt, x_sample, mem_prompt, state_gla, state_rg_h, state_rg_conv, cache_swa_k, cache_swa_v, cache_mem_k, cache_mem_v, g_mix, w_in, gla_w_gate2, gla_b_gate, gla_g_out, rg_conv_w, rg_conv_b, rg_w_a, rg_b_a, rg_w_x, rg_b_x, rg_lam, swa_g_q, swa_g_k, swa_sinks, w_out, g_mem_x, g_mem_m, mem_w_q, mem_w_kv, mem_g_q, mem_g_k, mem_w_o, g_ffn, ffn_w1, ffn_w2):
    depth = w_in.shape[0]
    nb, t_len, _ = x_prompt.shape
    ns = x_sample.shape[0]
    n_mem = mem_prompt.shape[1]
    assert t_len % ROW_T == 0 and t_len % GLA_T == 0 and x_sample.shape[1] == 1 and ns % 8 == 0
    row3 = lambda a: a.reshape(depth, 1, a.shape[-1])

    w_in_p = _prep_w_in(w_in)
    wg = jnp.zeros((depth, GLA_QKP, GLA_QKP), F32).at[:, GLR_LANE:GLR_LANE + GLA_RANK, :GLA_QK].set(gla_w_gate2)
    wg = wg.astype(BF16)
    bg3 = row3(jnp.pad(gla_b_gate, ((0, 0), (0, GLA_QKP - GLA_QK))))
    gout3 = row3(jnp.pad(gla_g_out, ((0, 0), (0, GLA_DVP - GLA_DV))))
    wa = _block_diag(rg_w_a).astype(BF16)
    wx = _block_diag(rg_w_x).astype(BF16)
    ba3, bx3, lam3, cb3 = row3(rg_b_a), row3(rg_b_x), row3(rg_lam), row3(rg_conv_b)
    gq3 = row3(jnp.tile(swa_g_q, (1, SWA_HEADS)))
    gk3 = row3(jnp.tile(swa_g_k, (1, SWA_KV)))
    wo = jnp.concatenate([_pad_heads(w_out[:, :GLA_V], 1, GLA_HEADS, GLA_DV, GLA_DVP), w_out[:, GLA_V:]], axis=1)
    wo = wo.astype(BF16)
    gmx3, gmm3, gffn3, gmix3 = row3(g_mem_x), row3(g_mem_m), row3(g_ffn), row3(g_mix)
    mgq3 = row3(jnp.tile(mem_g_q, (1, MEM_HEADS)))
    mgk3 = row3(jnp.tile(mem_g_k, (1, MEM_HEADS)))
    wq = mem_w_q.astype(BF16)
    wkv = mem_w_kv.astype(BF16)
    wmo = mem_w_o.astype(BF16)
    w1 = ffn_w1.astype(BF16)
    w2 = ffn_w2.astype(BF16)
    b256 = _block_ones(SWA_Q, HEAD_DIM)
    b128 = _block_ones(SWA_KVW, HEAD_DIM)
    dup = _dup_matrix()
    eye256 = jnp.eye(GLA_QKP, dtype=BF16)

    mk_all, mv_all = _mem_kv(mem_prompt.reshape(nb * n_mem, D_MODEL), gmm3, wkv, mgk3, b256)
    mk_all = mk_all.reshape(depth, nb, n_mem, MEM_W)
    mv_all = mv_all.reshape(depth, nb, n_mem, MEM_W)

    xp = x_prompt.reshape(nb * t_len, D_MODEL)
    xs = x_sample.reshape(ns, D_MODEL)
    rg_conv_flat = state_rg_conv.reshape(depth, ns, (CONV_W - 1) * RG_WIDTH)
    gla_flat = state_gla.reshape(depth, ns, GLA_QK, GLA_DV)
    swk = cache_swa_k.reshape(depth, ns, WINDOW, SWA_KVW)
    swv = cache_swa_v.reshape(depth, ns, WINDOW, SWA_KVW)
    cmk = cache_mem_k.reshape(depth, ns, n_mem, MEM_W)
    cmv = cache_mem_v.reshape(depth, ns, n_mem, MEM_W)

    gla_p, gla_s, rgh_p, rgh_s, rgc_p, rgc_s = [], [], [], [], [], []
    swk_p, swk_s, swv_p, swv_s = [], [], [], []
    ts = min(ns, 128)
    for l in range(depth):
        hp = _in_proj(xp, gmix3, w_in_p, l, ROW_T)
        og, st = _gla_prompt(hp, wg, bg3, gout3, l, nb, t_len)
        orr, hl, cbuf = _rg_prompt(hp, rg_conv_w, cb3, wa, ba3, wx, bx3, lam3, l, nb, t_len)
        osw, kb, vb = _swa_prompt(swa_sinks, hp, gq3, gk3, b256, b128, dup, l, nb, t_len)
        xp = _post_prompt(xp, og, orr, osw, wo, gmx3, wq, mgq3, b256, mk_all, mv_all, wmo, l, nb, POST_T)
        xp = _ffn(xp, gffn3, w1, w2, l, ROW_T)
        st = st.reshape(nb, GLA_HEADS, GLA_DVP, GLA_QKP)
        gla_p.append(jnp.stack(
            [st[:, h, :GLA_DV, h * GLA_DK:(h + 1) * GLA_DK].transpose(0, 2, 1) for h in range(GLA_HEADS)], axis=1))
        rgh_p.append(hl.reshape(nb, RG_WIDTH))
        rgc_p.append(cbuf)
        swk_p.append(kb.reshape(nb, WINDOW, SWA_KV, HEAD_DIM))
        swv_p.append(vb.reshape(nb, WINDOW, SWA_KV, HEAD_DIM))

        hs = _in_proj(xs, gmix3, w_in_p, l, ts)
        (qs, kg, eg, org_s, hn, cnew, qn, kn, kx, vx) = _sample_rows(
            hs, state_rg_h, rg_conv_flat, wg, bg3, rg_conv_w, cb3, wa, ba3, wx, bx3, lam3, gq3, gk3, b256, b128,
            dup, l)
        og_s, sn = _sample_gla(gla_flat, qs, kg, eg, hs, gout3, eye256, l, 8)
        osw_s, ko, vo = _sample_swa(swa_sinks, swk, swv, qn, kn, kx, vx, hs, dup, l, 8)
        x1, qm = _sample_post1(xs, og_s, org_s, osw_s, wo, gmx3, wq, mgq3, b256, l)
        om = _sample_mem(qm, cmk, cmv, l, 8)
        xs = _sample_ffn(x1, om, wmo, gffn3, w1, w2, l)
        gla_s.append(sn.reshape(ns, GLA_HEADS, GLA_DK, GLA_DV))
        rgh_s.append(hn)
        rgc_s.append(cnew.reshape(ns, CONV_W - 1, RG_WIDTH))
        swk_s.append(ko.reshape(ns, WINDOW, SWA_KV, HEAD_DIM))
        swv_s.append(vo.reshape(ns, WINDOW, SWA_KV, HEAD_DIM))

    return (xp.reshape(nb, t_len, D_MODEL), xs.reshape(ns, 1, D_MODEL),
            jnp.stack(gla_p), jnp.stack(gla_s), jnp.stack(rgh_p), jnp.stack(rgh_s),
            jnp.stack(rgc_p), jnp.stack(rgc_s), jnp.stack(swk_p), jnp.stack(swk_s),
            jnp.stack(swv_p), jnp.stack(swv_s),
            mk_all.reshape(depth, nb, n_mem, MEM_HEADS, MEM_HD), mv_all.reshape(depth, nb, n_mem, MEM_HEADS, MEM_HD))
```

```python
import functools
import math

import jax
import jax.numpy as jnp
from jax import lax
from jax.experimental import pallas as pl
from jax.experimental.pallas import tpu as pltpu

F32 = jnp.float32
BF16 = jnp.bfloat16

D_MODEL = 1024
EPS = 1e-6
GLA_HEADS, GLA_DK, GLA_DV, GLA_RANK, GLA_TAU = 4, 48, 96, 16, 16.0
GLA_QK = GLA_HEADS * GLA_DK
GLA_V = GLA_HEADS * GLA_DV
RG_WIDTH, RG_BLOCKS, RG_C, CONV_W = 384, 4, 8.0, 4
RG_BW = RG_WIDTH // RG_BLOCKS
SWA_HEADS, SWA_KV, HEAD_DIM, WINDOW = 4, 2, 64, 128
SWA_G = SWA_HEADS // SWA_KV
SWA_Q = SWA_HEADS * HEAD_DIM
SWA_KVW = SWA_KV * HEAD_DIM
MEM_HEADS, MEM_HD = 4, 64
MEM_W = MEM_HEADS * MEM_HD
D_FF = 4 * D_MODEL

LANE = 128
SUBLANE = 8
GLA_DVP = LANE
GLA_VP = GLA_HEADS * GLA_DVP
GLA_QKP = 2 * LANE
MIX_W = GLA_VP + RG_WIDTH + SWA_Q

C_RX, C_RY, C_GQ, C_GK, C_SQ, C_GV, C_GOG, C_SK, C_SV = 0, 384, 768, 1024, 1280, 1536, 2048, 2560, 2688
NP = 2816
GLR_LANE = GLA_QK

VMEM_LIMIT = 56 * 1024 * 1024
NEG = -1e30

MIX_T = 256
ROW_T = 512
FF_CHUNK = 1024


def _cparams(*sem):
    return pltpu.CompilerParams(dimension_semantics=sem, vmem_limit_bytes=VMEM_LIMIT)


def _rms(x, g):
    ms = jnp.mean(x * x, axis=-1, keepdims=True)
    return (x * lax.rsqrt(ms + EPS)) * g


def _dot(a, b):
    return jnp.dot(a.astype(BF16), b.astype(BF16), preferred_element_type=F32)


def _dot_nt(a, b):
    return lax.dot_general(a.astype(BF16), b.astype(BF16), (((1,), (1,)), ((), ())), preferred_element_type=F32)


def _split2_dot(x, m):
    hi = x.astype(BF16)
    lo = (x - hi.astype(F32)).astype(BF16)
    return jnp.dot(hi, m, preferred_element_type=F32) + jnp.dot(lo, m, preferred_element_type=F32)


def _split3(x):
    x1 = x.astype(BF16)
    r1 = x - x1.astype(F32)
    x2 = r1.astype(BF16)
    x3 = (r1 - x2.astype(F32)).astype(BF16)
    return x1, x2, x3


def _split3_dot_left(m, x):
    d = functools.partial(jnp.dot, preferred_element_type=F32)
    x1, x2, x3 = _split3(x)
    return d(m, x1) + d(m, x2) + d(m, x3)


def _split3_dot_left_nt(m, x):
    d = functools.partial(lax.dot_general, dimension_numbers=(((1,), (1,)), ((), ())), preferred_element_type=F32)
    x1, x2, x3 = _split3(x)
    return d(m, x1) + d(m, x2) + d(m, x3)


def _seg_rms(x, bones, seg, g):
    ms = _split2_dot(x * x, bones) * (1.0 / seg)
    return (x * lax.rsqrt(ms + EPS)) * g


def _sigmoid(x):
    return 1.0 / (1.0 + jnp.exp(-x))


def _softplus(x):
    return jnp.maximum(x, 0.0) + jnp.log1p(jnp.exp(-jnp.abs(x)))


def _log_sigmoid(z):
    return -_softplus(-z)


def _gelu_tanh(x):
    return x * (0.5 * (1.0 + jnp.tanh(math.sqrt(2.0 / math.pi) * (x + 0.044715 * (x * x * x)))))


def _lane_head_mask(width, seg, h, dtype=F32):
    lane = lax.broadcasted_iota(jnp.int32, (1, width), 1)
    return ((lane >= seg * h) & (lane < seg * (h + 1))).astype(dtype)


def _masked_softmax_rows(s, sink):
    m = jnp.maximum(jnp.max(s, axis=-1, keepdims=True), sink)
    e = jnp.exp(s - m)
    den = jnp.sum(e, axis=-1, keepdims=True) + jnp.exp(sink - m)
    return e / den


def _softmax_rows(s):
    m = jnp.max(s, axis=-1, keepdims=True)
    e = jnp.exp(s - m)
    return e / jnp.sum(e, axis=-1, keepdims=True)


def _alibi_slope(h):
    return 2.0 ** (-8.0 * (h + 1) / SWA_HEADS)


def _rows(x, starts, size):
    parts = [x[s:s + size] for s in starts]
    return parts[0] if len(parts) == 1 else jnp.concatenate(parts, axis=0)


def _in_proj_body(x_ref, g_ref, w_ref, o_ref):
    xn = _rms(x_ref[...], g_ref[...])
    o_ref[...] = jnp.dot(xn.astype(BF16), w_ref[...], preferred_element_type=F32)


def _in_proj(x2d, g3, w3, layer, tm):
    m = x2d.shape[0]
    return pl.pallas_call(
        _in_proj_body,
        grid=(m // tm,),
        in_specs=[
            pl.BlockSpec((tm, D_MODEL), lambda i: (i, 0)),
            pl.BlockSpec((None, 1, D_MODEL), lambda i: (layer, 0, 0)),
            pl.BlockSpec((None, D_MODEL, NP), lambda i: (layer, 0, 0)),
        ],
        out_specs=pl.BlockSpec((tm, NP), lambda i: (i, 0)),
        out_shape=jax.ShapeDtypeStruct((m, NP), F32),
        compiler_params=_cparams("parallel"),
        name="in_proj",
    )(x2d, g3, w3)


def _ffn_body(x_ref, g_ref, w1_ref, w2_ref, o_ref):
    x = x_ref[...]
    xn = _rms(x, g_ref[...]).astype(BF16)
    acc = x
    for c in range(D_FF // FF_CHUNK):
        h = jnp.dot(xn, w1_ref[:, c * FF_CHUNK:(c + 1) * FF_CHUNK], preferred_element_type=F32)
        h = jnp.square(jnp.maximum(h, 0.0))
        acc = acc + jnp.dot(h.astype(BF16), w2_ref[c * FF_CHUNK:(c + 1) * FF_CHUNK, :], preferred_element_type=F32)
    o_ref[...] = acc


def _ffn(x2d, g3, w1, w2, layer, tm):
    m = x2d.shape[0]
    const = dict(pipeline_mode=pl.Buffered(1))
    return pl.pallas_call(
        _ffn_body,
        grid=(m // tm,),
        in_specs=[
            pl.BlockSpec((tm, D_MODEL), lambda i: (i, 0)),
            pl.BlockSpec((None, 1, D_MODEL), lambda i: (layer, 0, 0)),
            pl.BlockSpec((None, D_MODEL, D_FF), lambda i: (layer, 0, 0), **const),
            pl.BlockSpec((None, D_FF, D_MODEL), lambda i: (layer, 0, 0), **const),
        ],
        out_specs=pl.BlockSpec((tm, D_MODEL), lambda i: (i, 0)),
        out_shape=jax.ShapeDtypeStruct((m, D_MODEL), F32),
        compiler_params=_cparams("parallel"),
        name="ffn",
    )(x2d, g3, w1, w2)


def _mem_kv_body(mem_ref, g_ref, w_ref, gk_ref, bones_ref, k_ref, v_ref):
    kv = jnp.dot(_rms(mem_ref[...], g_ref[...]).astype(BF16), w_ref[...], preferred_element_type=F32)
    k_ref[...] = _seg_rms(kv[:, :MEM_W], bones_ref[...], MEM_HD, gk_ref[...])
    v_ref[...] = kv[:, MEM_W:]


def _mem_kv(mem2d, g3, wkv, gk3, bones256):
    depth = wkv.shape[0]
    m = mem2d.shape[0]
    return pl.pallas_call(
        _mem_kv_body,
        grid=(depth,),
        in_specs=[
            pl.BlockSpec((m, D_MODEL), lambda l: (0, 0)),
            pl.BlockSpec((None, 1, D_MODEL), lambda l: (l, 0, 0)),
            pl.BlockSpec((None, D_MODEL, 2 * MEM_W), lambda l: (l, 0, 0)),
            pl.BlockSpec((None, 1, MEM_W), lambda l: (l, 0, 0)),
            pl.BlockSpec((MEM_W, MEM_W), lambda l: (0, 0)),
        ],
        out_specs=[pl.BlockSpec((None, m, MEM_W), lambda l: (l, 0, 0))] * 2,
        out_shape=[jax.ShapeDtypeStruct((depth, m, MEM_W), F32)] * 2,
        compiler_params=_cparams("parallel"),
        name="mem_kv",
    )(mem2d, g3, wkv, gk3, bones256)


def _mix_out(x, og, orr, osw, wo_ref):
    y = _dot(og, wo_ref[0:GLA_VP, :])
    y = y + _dot(orr, wo_ref[GLA_VP:GLA_VP + RG_WIDTH, :])
    y = y + _dot(osw, wo_ref[GLA_VP + RG_WIDTH:MIX_W, :])
    return x + y


def _mem_q(x1, gx, wq_ref, gq, bones):
    q = jnp.dot(_rms(x1, gx).astype(BF16), wq_ref[...], preferred_element_type=F32)
    return _seg_rms(q, bones, MEM_HD, gq)


def _gla_tile(h_scr, wg_ref, bg_ref, gout_ref, st_scr, cum_scr, d_scr, oc_scr):
    tt = h_scr.shape[0]
    hf = tt // 2
    qraw = h_scr[:, C_GQ:C_GQ + GLA_QKP]
    k = h_scr[:, C_GK:C_GK + GLA_QKP]
    z = jnp.dot(qraw.astype(BF16), wg_ref[...], preferred_element_type=F32) + bg_ref[...]
    g = _log_sigmoid(z) * (1.0 / GLA_TAU)
    q = qraw * (GLA_DK ** -0.5)

    row = lax.broadcasted_iota(jnp.int32, (tt, tt), 0)
    col = lax.broadcasted_iota(jnp.int32, (tt, tt), 1)
    cum = _split3_dot_left((row >= col).astype(BF16), g)
    cum_scr[...] = cum
    rowi = lax.broadcasted_iota(jnp.int32, (tt, 1), 0)

    def boundary(b, n):
        return jnp.concatenate(
            [jnp.broadcast_to(cum_scr[pl.ds(gi * 2 * b + b - 1, 1), :], (n, GLA_QKP)) for gi in range(tt // (2 * b))],
            axis=0)

    rd1 = pltpu.roll(cum, 1, 0)
    rd2 = pltpu.roll(cum, 2, 0)
    ru1 = pltpu.roll(cum, tt - 1, 0)
    m4 = rowi & 3
    bounds = {
        1: jnp.where((rowi & 1) == 0, cum, rd1),
        2: jnp.where(m4 == 0, ru1, jnp.where(m4 == 1, cum, jnp.where(m4 == 2, rd1, rd2))),
        4: boundary(4, 2 * 4),
    }
    low = [(q.astype(BF16), k.astype(BF16), 0)]
    for b in (1, 2, 4):
        second = (rowi & (2 * b - 1)) >= b
        c = bounds[b]
        qs = q * jnp.exp(jnp.where(second, cum - c, NEG))
        ks = k * jnp.exp(jnp.where(second, NEG, c - cum))
        low.append((qs.astype(BF16), ks.astype(BF16), int(math.log2(2 * b))))

    mid = []
    b = SUBLANE
    while b <= hf:
        ng = tt // (2 * b)
        c = boundary(b, b)
        firsts = [gi * 2 * b for gi in range(ng)]
        seconds = [gi * 2 * b + b for gi in range(ng)]
        qs2 = _rows(q, seconds, b) * jnp.exp(_rows(cum, seconds, b) - c)
        ks1 = _rows(k, firsts, b) * jnp.exp(c - _rows(cum, firsts, b))
        if b < hf:
            zb = jnp.zeros((b, GLA_QKP), F32)
            ks1 = jnp.concatenate([piece for gi in range(ng) for piece in (ks1[gi * b:(gi + 1) * b], zb)], axis=0)
        mid.append((b, qs2.astype(BF16), ks1.astype(BF16)))
        b *= 2

    st = st_scr[...]
    o_inter = _dot_nt(q * jnp.exp(cum), st)
    last = cum_scr[pl.ds(tt - 1, 1), :]
    kd = k * jnp.exp(last - cum)
    v_all = h_scr[:, C_GV:C_GV + GLA_VP]
    upd = jnp.dot(v_all.T.astype(BF16), kd.astype(BF16), preferred_element_type=F32)
    vrow = lax.broadcasted_iota(jnp.int32, (GLA_VP, 1), 0) // GLA_DVP
    lane = lax.broadcasted_iota(jnp.int32, (1, GLA_QKP), 1)
    khead = ((lane >= GLA_DK).astype(jnp.int32) + (lane >= 2 * GLA_DK).astype(jnp.int32)
             + (lane >= 3 * GLA_DK).astype(jnp.int32) + (lane >= 4 * GLA_DK).astype(jnp.int32))
    st_new = jnp.where(vrow == khead, st * jnp.exp(last) + upd, 0.0)
    st_scr[...] = st_new

    r128 = lax.broadcasted_iota(jnp.int32, (hf, 1), 0)
    c128 = lax.broadcasted_iota(jnp.int32, (1, hf), 1)
    r64 = lax.broadcasted_iota(jnp.int32, (hf // 2, 1), 0)
    low_masks = [(r128 >> s) == (c128 >> s) for (_, _, s) in low]
    mid_masks = {bb: (r64 >> int(math.log2(bb))) == (c128 >> int(math.log2(2 * bb)))
                 for (bb, _, _) in mid if 2 * bb < hf}
    nt = (((1,), (1,)), ((), ()))
    gout = gout_ref[...]
    for h in range(GLA_HEADS):
        hmb = _lane_head_mask(GLA_QKP, GLA_DK, h, BF16)
        for half in range(2):
            rs = slice(half * hf, (half + 1) * hf)
            acc = None
            for (qb, kb, _), mk in zip(low, low_masks):
                p = lax.dot_general(qb[rs] * hmb, kb[rs], nt, preferred_element_type=F32)
                p = jnp.where(mk, p, 0.0)
                acc = p if acc is None else acc + p
            d_scr[h, half] = acc
            for bb, qb2, kb1 in mid:
                if bb == hf:
                    continue
                cs = slice(half * (hf // 2), (half + 1) * (hf // 2))
                p = lax.dot_general(qb2[cs] * hmb, kb1[rs], nt, preferred_element_type=F32)
                if bb in mid_masks:
                    p = jnp.where(mid_masks[bb], p, 0.0)
                for gi in range(hf // (2 * bb)):
                    dst = pl.ds(gi * 2 * bb + bb, bb)
                    d_scr[h, half, dst, :] = d_scr[h, half, dst, :] + p[gi * bb:(gi + 1) * bb, :]
        _, qb2, kb1 = mid[-1]
        off = lax.dot_general(qb2 * hmb, kb1, nt, preferred_element_type=F32)
        sl = slice(h * GLA_DVP, (h + 1) * GLA_DVP)
        v_h = v_all[:, sl].astype(BF16)
        o0 = jnp.dot(d_scr[h, 0].astype(BF16), v_h[0:hf], preferred_element_type=F32)
        a1 = jnp.concatenate([off, d_scr[h, 1]], axis=1).astype(BF16)
        o1 = jnp.dot(a1, v_h, preferred_element_type=F32)
        o = jnp.concatenate([o0, o1], axis=0) + o_inter[:, sl]
        ms = jnp.sum(o * o, axis=-1, keepdims=True) * (1.0 / GLA_DV)
        gate = h_scr[:, C_GOG + h * GLA_DVP:C_GOG + (h + 1) * GLA_DVP]
        oc_scr[:, sl] = ((o * lax.rsqrt(ms + EPS)) * gout) * (gate * _sigmoid(gate))
    return st_new


def _rg_gates(xc, wa_ref, ba, wx_ref, bx, lam):
    r = _sigmoid(jnp.dot(xc.astype(BF16), wa_ref[...], preferred_element_type=F32) + ba)
    i = _sigmoid(jnp.dot(xc.astype(BF16), wx_ref[...], preferred_element_type=F32) + bx)
    log_a = (-RG_C * r) * _softplus(-lam)
    a = jnp.exp(log_a)
    b = jnp.sqrt(1.0 - jnp.exp(2.0 * log_a)) * (i * xc)
    return a, b


def _rg_tile(h_scr, cw_ref, cb_ref, wa_ref, ba_ref, wx_ref, bx_ref, lam_ref, hc_scr, cc_scr, ra_scr, rb_scr, oc_scr):
    tt = h_scr.shape[0]
    rx = h_scr[:, C_RX:C_RX + RG_WIDTH]
    rowi = lax.broadcasted_iota(jnp.int32, (tt, 1), 0)
    p0, p1, p2 = cc_scr[0:1, :], cc_scr[1:2, :], cc_scr[2:3, :]
    x1 = jnp.where(rowi == 0, p2, pltpu.roll(rx, 1, 0))
    x2 = jnp.where(rowi == 0, p1, jnp.where(rowi == 1, p2, pltpu.roll(rx, 2, 0)))
    x3 = jnp.where(rowi == 0, p0, jnp.where(rowi == 1, p1, jnp.where(rowi == 2, p2, pltpu.roll(rx, 3, 0))))
    cw = cw_ref[...]
    xc = cb_ref[...] + x3 * cw[0:1, :]
    xc = xc + x2 * cw[1:2, :]
    xc = xc + x1 * cw[2:3, :]
    xc = xc + rx * cw[3:4, :]
    a, b = _rg_gates(xc, wa_ref, ba_ref[...], wx_ref, bx_ref[...], lam_ref[...])
    nl = RG_WIDTH // LANE
    for c in range(nl):
        ra_scr[c] = a[:, c * LANE:(c + 1) * LANE]
        rb_scr[c] = b[:, c * LANE:(c + 1) * LANE]

    seg = tt // SUBLANE
    hloc = [jnp.zeros((SUBLANE, LANE), F32)] * nl
    pc = [jnp.ones((SUBLANE, LANE), F32)] * nl
    for i in range(seg):
        idx = pl.ds(i, SUBLANE, stride=seg)
        for c in range(nl):
            ai = ra_scr[c, idx, :]
            hloc[c] = ai * hloc[c] + rb_scr[c, idx, :]
            pc[c] = ai * pc[c]
            rb_scr[c, idx, :] = hloc[c]
            ra_scr[c, idx, :] = pc[c]
    hloc = jnp.concatenate(hloc, axis=1)
    pc = jnp.concatenate(pc, axis=1)
    hprev = hc_scr[0:1, :]
    carries = []
    for j in range(SUBLANE):
        carries.append(jnp.broadcast_to(hprev, (seg, RG_WIDTH)))
        hprev = hloc[j:j + 1, :] + pc[j:j + 1, :] * hprev
    hc_scr[0:1, :] = hprev
    carry = jnp.concatenate(carries, axis=0)
    h = jnp.concatenate([rb_scr[c] + ra_scr[c] * carry[:, c * LANE:(c + 1) * LANE] for c in range(nl)], axis=1)
    oc_scr[:, GLA_VP:GLA_VP + RG_WIDTH] = h * _gelu_tanh(h_scr[:, C_RY:C_RY + RG_WIDTH])
    tail = h_scr[pl.ds(tt - (CONV_W - 1), CONV_W - 1), C_RX:C_RX + RG_WIDTH]
    cc_scr[0:CONV_W - 1, :] = tail
    return hprev, tail


def _swa_tile(h_scr, sink_ref, layer, first, gq_ref, gk_ref, b256_ref, b128_ref, dup_ref, kp_scr, vp_scr, oc_scr):
    tt = h_scr.shape[0]
    w = WINDOW
    qn = _seg_rms(h_scr[:, C_SQ:C_SQ + SWA_Q], b256_ref[...], HEAD_DIM, gq_ref[...]).astype(BF16)
    kn = _seg_rms(h_scr[:, C_SK:C_SK + SWA_KVW], b128_ref[...], HEAD_DIM, gk_ref[...])
    v = h_scr[:, C_SV:C_SV + SWA_KVW]
    dup = dup_ref[...]
    kcat = jnp.concatenate([kp_scr[...], kn], axis=0).astype(BF16)
    vcat = jnp.concatenate([vp_scr[...], v], axis=0).astype(BF16)
    kexp = jnp.dot(kcat, dup, preferred_element_type=F32).astype(BF16)
    vexp = jnp.dot(vcat, dup, preferred_element_type=F32).astype(BF16)

    qi = lax.broadcasted_iota(jnp.int32, (w, 2 * w), 0)
    kj = lax.broadcasted_iota(jnp.int32, (w, 2 * w), 1)
    dist_i = qi + w - kj
    in_window = (dist_i >= 0) & (dist_i <= WINDOW)
    dist = dist_i.astype(F32)
    nt = (((1,), (1,)), ((), ()))
    for blk in range(tt // w):
        allowed = in_window & ((kj >= w) | jnp.logical_not(first)) if blk == 0 else in_window
        qb = qn[blk * w:(blk + 1) * w]
        kb = kexp[blk * w:(blk + 2) * w]
        vb = vexp[blk * w:(blk + 2) * w]
        o = jnp.zeros((w, SWA_Q), F32)
        for h in range(SWA_HEADS):
            hm = _lane_head_mask(SWA_Q, HEAD_DIM, h)
            s = lax.dot_general(qb * hm.astype(BF16), kb, nt, preferred_element_type=F32)
            s = s * (HEAD_DIM ** -0.5) - _alibi_slope(h) * dist
            s = jnp.where(allowed, s, NEG)
            p = _masked_softmax_rows(s, sink_ref[layer, h])
            o = o + hm * jnp.dot(p.astype(BF16), vb, preferred_element_type=F32)
        oc_scr[blk * w:(blk + 1) * w, GLA_VP + RG_WIDTH:MIX_W] = o
    k_win = kn[tt - w:tt]
    v_win = v[tt - w:tt]
    kp_scr[...] = k_win
    vp_scr[...] = v_win
    return k_win, v_win


def _mem_attend_rows(x1, gx, wq_ref, gq, bones, mk, mv, wmo_ref):
    qn = _mem_q(x1, gx, wq_ref, gq, bones).astype(BF16)
    o = jnp.zeros(qn.shape, F32)
    nt = (((1,), (1,)), ((), ()))
    for h in range(MEM_HEADS):
        hm = _lane_head_mask(MEM_W, MEM_HD, h)
        s = lax.dot_general(qn * hm.astype(BF16), mk, nt, preferred_element_type=F32) * (MEM_HD ** -0.5)
        p = _softmax_rows(s)
        o = o + hm * jnp.dot(p.astype(BF16), mv, preferred_element_type=F32)
    return x1 + jnp.dot(o.astype(BF16), wmo_ref[...], preferred_element_type=F32)


def _mixer_body(sink_ref, x_ref, gmix_ref, win_ref, wg_ref, bg_ref, gout_ref, cw_ref, cb_ref, wa_ref, ba_ref,
                wx_ref, bx_ref, lam_ref, gq_ref, gk_ref, b256_ref, b128_ref, dup_ref, wo_ref, gmx_ref, wq_ref,
                mgq_ref, mk_ref, mv_ref, wmo_ref,
                o_ref, st_ref, hl_ref, cbuf_ref, ko_ref, vo_ref,
                h_scr, oc_scr, st_scr, cum_scr, d_scr, hc_scr, cc_scr, ra_scr, rb_scr, kp_scr, vp_scr, *, layer):
    nb = x_ref.shape[0]
    first = pl.program_id(0) == 0

    @pl.when(first)
    def _():
        for scr in (st_scr, hc_scr, cc_scr, kp_scr, vp_scr):
            scr[...] = jnp.zeros_like(scr)

    for bi in range(nb):
        h_scr[bi] = jnp.dot(_rms(x_ref[bi], gmix_ref[...]).astype(BF16), win_ref[...], preferred_element_type=F32)
    for bi in range(nb):
        st_ref[bi] = _gla_tile(h_scr.at[bi], wg_ref, bg_ref, gout_ref, st_scr.at[bi], cum_scr.at[bi], d_scr.at[bi],
                               oc_scr.at[bi])
    for bi in range(nb):
        h_last, tail = _rg_tile(h_scr.at[bi], cw_ref, cb_ref, wa_ref, ba_ref, wx_ref, bx_ref, lam_ref, hc_scr.at[bi],
                                cc_scr.at[bi], ra_scr.at[bi], rb_scr.at[bi], oc_scr.at[bi])
        hl_ref[bi] = h_last
        cbuf_ref[bi] = tail
    for bi in range(nb):
        k_win, v_win = _swa_tile(h_scr.at[bi], sink_ref, layer, first, gq_ref, gk_ref, b256_ref, b128_ref, dup_ref,
                                 kp_scr.at[bi], vp_scr.at[bi], oc_scr.at[bi])
        ko_ref[bi] = k_win
        vo_ref[bi] = v_win
    for bi in range(nb):
        x1 = x_ref[bi] + jnp.dot(oc_scr[bi].astype(BF16), wo_ref[...], preferred_element_type=F32)
        o_ref[bi] = _mem_attend_rows(x1, gmx_ref[...], wq_ref, mgq_ref[...], b256_ref[...],
                                     mk_ref[bi].astype(BF16), mv_ref[bi].astype(BF16), wmo_ref)


def _mixer_prompt(sinks, x3d, gmix3, w_in_p, wg, bg3, gout3, cw, cb3, wa, ba3, wx, bx3, lam3, gq3, gk3, b256, b128,
                  dup, wo, gmx3, wq, mgq3, mk, mv, wmo, layer):
    nb, t_len, _ = x3d.shape
    tt = MIX_T
    n_mem = mk.shape[2]
    lay3 = lambda t: (layer, 0, 0)
    once = dict(pipeline_mode=pl.Buffered(1))
    vec = lambda w: pl.BlockSpec((None, 1, w), lay3)
    full = lambda a: pl.BlockSpec(a.shape, lambda t: (0,) * a.ndim)
    state = lambda r, w: pl.BlockSpec((nb, r, w), lambda t: (0, 0, 0))
    mem = pl.BlockSpec((None, nb, n_mem, MEM_W), lambda t: (layer, 0, 0, 0))
    return pl.pallas_call(
        functools.partial(_mixer_body, layer=layer),
        grid=(t_len // tt,),
        in_specs=[
            pl.BlockSpec(memory_space=pltpu.SMEM),
            pl.BlockSpec((nb, tt, D_MODEL), lambda t: (0, t, 0)),
            vec(D_MODEL),
            pl.BlockSpec((None, D_MODEL, NP), lay3, **once),
            pl.BlockSpec((None, GLA_QKP, GLA_QKP), lay3), vec(GLA_QKP), vec(GLA_DVP),
            pl.BlockSpec((None, CONV_W, RG_WIDTH), lay3), vec(RG_WIDTH),
            pl.BlockSpec((None, RG_WIDTH, RG_WIDTH), lay3), vec(RG_WIDTH),
            pl.BlockSpec((None, RG_WIDTH, RG_WIDTH), lay3), vec(RG_WIDTH), vec(RG_WIDTH),
            vec(SWA_Q), vec(SWA_KVW), full(b256), full(b128), full(dup),
            pl.BlockSpec((None, MIX_W, D_MODEL), lay3, **once),
            vec(D_MODEL), pl.BlockSpec((None, D_MODEL, MEM_W), lay3), vec(MEM_W),
            mem, mem,
            pl.BlockSpec((None, MEM_W, D_MODEL), lay3),
        ],
        out_specs=[
            pl.BlockSpec((nb, tt, D_MODEL), lambda t: (0, t, 0)),
            state(GLA_VP, GLA_QKP), state(1, RG_WIDTH), state(CONV_W - 1, RG_WIDTH),
            state(WINDOW, SWA_KVW), state(WINDOW, SWA_KVW),
        ],
        out_shape=[
            jax.ShapeDtypeStruct((nb, t_len, D_MODEL), F32),
            jax.ShapeDtypeStruct((nb, GLA_VP, GLA_QKP), F32),
            jax.ShapeDtypeStruct((nb, 1, RG_WIDTH), F32),
            jax.ShapeDtypeStruct((nb, CONV_W - 1, RG_WIDTH), F32),
            jax.ShapeDtypeStruct((nb, WINDOW, SWA_KVW), F32),
            jax.ShapeDtypeStruct((nb, WINDOW, SWA_KVW), F32),
        ],
        scratch_shapes=[
            pltpu.VMEM((nb, tt, NP), F32), pltpu.VMEM((nb, tt, MIX_W), F32),
            pltpu.VMEM((nb, GLA_VP, GLA_QKP), F32), pltpu.VMEM((nb, tt, GLA_QKP), F32),
            pltpu.VMEM((nb, GLA_HEADS, 2, tt // 2, tt // 2), F32),
            pltpu.VMEM((nb, SUBLANE, RG_WIDTH), F32), pltpu.VMEM((nb, SUBLANE, RG_WIDTH), F32),
            pltpu.VMEM((nb, RG_WIDTH // LANE, tt, LANE), F32), pltpu.VMEM((nb, RG_WIDTH // LANE, tt, LANE), F32),
            pltpu.VMEM((nb, WINDOW, SWA_KVW), F32), pltpu.VMEM((nb, WINDOW, SWA_KVW), F32),
        ],
        compiler_params=_cparams("arbitrary"),
        name="mixer_prompt",
    )(sinks, x3d, gmix3, w_in_p, wg, bg3, gout3, cw, cb3, wa, ba3, wx, bx3, lam3, gq3, gk3, b256, b128, dup, wo,
      gmx3, wq, mgq3, mk, mv, wmo)


def _sample_rows_body(h_ref, hst_ref, cbuf_ref, wg_ref, bg_ref, cw_ref, cb_ref, wa_ref, ba_ref, wx_ref, bx_ref,
                      lam_ref, gq_ref, gk_ref, b256_ref, b128_ref, dup_ref,
                      qs_ref, kg_ref, eg_ref, org_ref, hn_ref, cnew_ref, qn_ref, kn_ref, kx_ref, vx_ref):
    qraw = h_ref[:, C_GQ:C_GQ + GLA_QKP]
    z = jnp.dot(qraw.astype(BF16), wg_ref[...], preferred_element_type=F32) + bg_ref[...]
    eg_ref[...] = jnp.exp(_log_sigmoid(z) * (1.0 / GLA_TAU))
    qs_ref[...] = qraw * (GLA_DK ** -0.5)
    kg_ref[...] = h_ref[:, C_GK:C_GK + GLA_QKP]

    rx = h_ref[:, C_RX:C_RX + RG_WIDTH]
    cw = cw_ref[...]
    xc = cb_ref[...] + cbuf_ref[:, 0:RG_WIDTH] * cw[0:1, :]
    xc = xc + cbuf_ref[:, RG_WIDTH:2 * RG_WIDTH] * cw[1:2, :]
    xc = xc + cbuf_ref[:, 2 * RG_WIDTH:3 * RG_WIDTH] * cw[2:3, :]
    xc = xc + rx * cw[3:4, :]
    a, b = _rg_gates(xc, wa_ref, ba_ref[...], wx_ref, bx_ref[...], lam_ref[...])
    hn = a * hst_ref[...] + b
    hn_ref[...] = hn
    org_ref[...] = hn * _gelu_tanh(h_ref[:, C_RY:C_RY + RG_WIDTH])
    cnew_ref[:, 0:2 * RG_WIDTH] = cbuf_ref[:, RG_WIDTH:3 * RG_WIDTH]
    cnew_ref[:, 2 * RG_WIDTH:3 * RG_WIDTH] = rx

    qn_ref[...] = _seg_rms(h_ref[:, C_SQ:C_SQ + SWA_Q], b256_ref[...], HEAD_DIM, gq_ref[...])
    kn = _seg_rms(h_ref[:, C_SK:C_SK + SWA_KVW], b128_ref[...], HEAD_DIM, gk_ref[...])
    kn_ref[...] = kn
    kx_ref[...] = jnp.dot(kn.astype(BF16), dup_ref[...], preferred_element_type=F32)
    vx_ref[...] = jnp.dot(h_ref[:, C_SV:C_SV + SWA_KVW].astype(BF16), dup_ref[...], preferred_element_type=F32)


def _sample_rows(hs, hst, cbuf, wg, bg3, cw, cb3, wa, ba3, wx, bx3, lam3, gq3, gk3, b256, b128, dup, layer):
    n = hs.shape[0]
    lay3 = lambda i: (layer, 0, 0)
    full = lambda a: pl.BlockSpec(a.shape, lambda i: (0,) * a.ndim)
    vec = lambda w: pl.BlockSpec((None, 1, w), lay3)
    sq = lambda w: pl.BlockSpec((None, w, w), lay3)
    outs = [GLA_QKP, GLA_QKP, GLA_QKP, RG_WIDTH, RG_WIDTH, 3 * RG_WIDTH, SWA_Q, SWA_KVW, SWA_Q, SWA_Q]
    return pl.pallas_call(
        _sample_rows_body,
        grid=(1,),
        in_specs=[
            full(hs), pl.BlockSpec((None, n, RG_WIDTH), lay3), pl.BlockSpec((None, n, 3 * RG_WIDTH), lay3),
            sq(GLA_QKP), vec(GLA_QKP), pl.BlockSpec((None, CONV_W, RG_WIDTH), lay3), vec(RG_WIDTH),
            sq(RG_WIDTH), vec(RG_WIDTH), sq(RG_WIDTH), vec(RG_WIDTH), vec(RG_WIDTH),
            vec(SWA_Q), vec(SWA_KVW), full(b256), full(b128), full(dup),
        ],
        out_specs=[pl.BlockSpec((n, w), lambda i: (0, 0)) for w in outs],
        out_shape=[jax.ShapeDtypeStruct((n, w), F32) for w in outs],
        compiler_params=_cparams("arbitrary"),
        name="sample_rows",
    )(hs, hst, cbuf, wg, bg3, cw, cb3, wa, ba3, wx, bx3, lam3, gq3, gk3, b256, b128, dup)


def _sample_gla_body(s0_ref, qs_ref, kg_ref, eg_ref, h_ref, gout_ref, eye_ref, o_ref, sn_ref):
    bt = s0_ref.shape[0]
    eye = eye_ref[...]
    qt = _split3_dot_left_nt(eye, qs_ref[...])
    kt = _split3_dot_left_nt(eye, kg_ref[...])
    et = _split3_dot_left_nt(eye, eg_ref[...])
    gout = gout_ref[:, 0:GLA_DV]
    o_ref[...] = jnp.zeros_like(o_ref)
    for j in range(bt):
        for h in range(GLA_HEADS):
            rs = slice(h * GLA_DK, (h + 1) * GLA_DK)
            vrow = h_ref[j:j + 1, C_GV + h * GLA_DVP:C_GV + h * GLA_DVP + GLA_DV]
            sn = et[rs, j:j + 1] * s0_ref[j, rs, :] + kt[rs, j:j + 1] * vrow
            sn_ref[j, rs, :] = sn
            o = jnp.sum(qt[rs, j:j + 1] * sn, axis=0, keepdims=True)
            ms = jnp.sum(o * o, axis=-1, keepdims=True) * (1.0 / GLA_DV)
            gate = h_ref[j:j + 1, C_GOG + h * GLA_DVP:C_GOG + h * GLA_DVP + GLA_DV]
            o_ref[j:j + 1, h * GLA_DVP:h * GLA_DVP + GLA_DV] = ((o * lax.rsqrt(ms + EPS)) * gout) * (
                gate * _sigmoid(gate))


def _sample_gla(s0, qs, kg, eg, hs, gout3, eye256, layer, bt):
    n = hs.shape[0]
    row = lambda w: pl.BlockSpec((bt, w), lambda i: (i, 0))
    return pl.pallas_call(
        _sample_gla_body,
        grid=(n // bt,),
        in_specs=[
            pl.BlockSpec((None, bt, GLA_QK, GLA_DV), lambda i: (layer, i, 0, 0)),
            row(GLA_QKP), row(GLA_QKP), row(GLA_QKP), row(NP),
            pl.BlockSpec((None, 1, GLA_DVP), lambda i: (layer, 0, 0)),
            pl.BlockSpec((GLA_QKP, GLA_QKP), lambda i: (0, 0)),
        ],
        out_specs=[row(GLA_VP), pl.BlockSpec((bt, GLA_QK, GLA_DV), lambda i: (i, 0, 0))],
        out_shape=[jax.ShapeDtypeStruct((n, GLA_VP), F32), jax.ShapeDtypeStruct((n, GLA_QK, GLA_DV), F32)],
        compiler_params=_cparams("parallel"),
        name="sample_gla",
    )(s0, qs, kg, eg, hs, gout3, eye256)


def _sample_swa_body(sink_ref, kc_ref, vc_ref, qn_ref, kn_ref, kx_ref, vx_ref, h_ref, dup_ref, o_ref, ko_ref,
                     vo_ref, *, layer):
    bt = kc_ref.shape[0]
    dup = dup_ref[...]
    hi = lax.broadcasted_iota(jnp.int32, (SWA_HEADS, SWA_Q), 0)
    li = lax.broadcasted_iota(jnp.int32, (SWA_HEADS, SWA_Q), 1)
    hm4 = ((li >= hi * HEAD_DIM) & (li < (hi + 1) * HEAD_DIM)).astype(F32)
    hrow = lax.broadcasted_iota(jnp.int32, (SWA_HEADS, 1), 0)
    slope = jnp.zeros((SWA_HEADS, 1), F32)
    sink = jnp.zeros((SWA_HEADS, 1), F32)
    for h in range(SWA_HEADS):
        slope = jnp.where(hrow == h, _alibi_slope(h), slope)
        sink = jnp.where(hrow == h, sink_ref[layer, h], sink)
    dist = (WINDOW - lax.broadcasted_iota(jnp.int32, (1, WINDOW), 1)).astype(F32)
    for j in range(bt):
        kexp = jnp.dot(kc_ref[j].astype(BF16), dup, preferred_element_type=F32).astype(BF16)
        vexp = jnp.dot(vc_ref[j].astype(BF16), dup, preferred_element_type=F32).astype(BF16)
        q4 = (qn_ref[j:j + 1, :] * hm4).astype(BF16)
        s = lax.dot_general(q4, kexp, (((1,), (1,)), ((), ())), preferred_element_type=F32)
        s = s * (HEAD_DIM ** -0.5) - slope * dist
        s_new = jnp.sum(q4.astype(F32) * kx_ref[j:j + 1, :], axis=-1, keepdims=True) * (HEAD_DIM ** -0.5)
        m = jnp.maximum(jnp.maximum(jnp.max(s, axis=-1, keepdims=True), s_new), sink)
        e = jnp.exp(s - m)
        e_new = jnp.exp(s_new - m)
        den = jnp.sum(e, axis=-1, keepdims=True) + e_new + jnp.exp(sink - m)
        o4 = jnp.dot((e / den).astype(BF16), vexp, preferred_element_type=F32)
        o4 = o4 + (e_new / den).astype(BF16).astype(F32) * vx_ref[j:j + 1, :]
        o_ref[j:j + 1, :] = jnp.sum(o4 * hm4, axis=0, keepdims=True)
        ko_ref[j, 0:WINDOW - 1, :] = kc_ref[j, pl.ds(1, WINDOW - 1), :]
        ko_ref[j, WINDOW - 1:WINDOW, :] = kn_ref[j:j + 1, :]
        vo_ref[j, 0:WINDOW - 1, :] = vc_ref[j, pl.ds(1, WINDOW - 1), :]
        vo_ref[j, WINDOW - 1:WINDOW, :] = h_ref[j:j + 1, C_SV:C_SV + SWA_KVW]


def _sample_swa(sinks, kc, vc, qn, kn, kx, vx, hs, dup, layer, bt):
    n = hs.shape[0]
    row = lambda w: pl.BlockSpec((bt, w), lambda i: (i, 0))
    cache = pl.BlockSpec((None, bt, WINDOW, SWA_KVW), lambda i: (layer, i, 0, 0))
    cache_o = pl.BlockSpec((bt, WINDOW, SWA_KVW), lambda i: (i, 0, 0))
    return pl.pallas_call(
        functools.partial(_sample_swa_body, layer=layer),
        grid=(n // bt,),
        in_specs=[
            pl.BlockSpec(memory_space=pltpu.SMEM), cache, cache,
            row(SWA_Q), row(SWA_KVW), row(SWA_Q), row(SWA_Q), row(NP),
            pl.BlockSpec(dup.shape, lambda i: (0, 0)),
        ],
        out_specs=[row(SWA_Q), cache_o, cache_o],
        out_shape=[
            jax.ShapeDtypeStruct((n, SWA_Q), F32),
            jax.ShapeDtypeStruct((n, WINDOW, SWA_KVW), F32),
            jax.ShapeDtypeStruct((n, WINDOW, SWA_KVW), F32),
        ],
        compiler_params=_cparams("parallel"),
        name="sample_swa",
    )(sinks, kc, vc, qn, kn, kx, vx, hs, dup)


def _sample_post1_body(x_ref, og_ref, or_ref, os_ref, wo_ref, gx_ref, wq_ref, gq_ref, bones_ref, x1_ref, qn_ref):
    x1 = _mix_out(x_ref[...], og_ref[...], or_ref[...], os_ref[...], wo_ref)
    x1_ref[...] = x1
    qn_ref[...] = _mem_q(x1, gx_ref[...], wq_ref, gq_ref[...], bones_ref[...])


def _sample_post1(x2d, og, orr, osw, wo, gx3, wq, gq3, bones256, layer):
    n = x2d.shape[0]
    lay3 = lambda i: (layer, 0, 0)
    full = lambda a: pl.BlockSpec(a.shape, lambda i: (0,) * a.ndim)
    return pl.pallas_call(
        _sample_post1_body,
        grid=(1,),
        in_specs=[
            full(x2d), full(og), full(orr), full(osw),
            pl.BlockSpec((None, MIX_W, D_MODEL), lay3),
            pl.BlockSpec((None, 1, D_MODEL), lay3),
            pl.BlockSpec((None, D_MODEL, MEM_W), lay3),
            pl.BlockSpec((None, 1, MEM_W), lay3),
            full(bones256),
        ],
        out_specs=[pl.BlockSpec((n, D_MODEL), lambda i: (0, 0)), pl.BlockSpec((n, MEM_W), lambda i: (0, 0))],
        out_shape=[jax.ShapeDtypeStruct((n, D_MODEL), F32), jax.ShapeDtypeStruct((n, MEM_W), F32)],
        compiler_params=_cparams("arbitrary"),
        name="sample_post1",
    )(x2d, og, orr, osw, wo, gx3, wq, gq3, bones256)


def _sample_mem_body(qn_ref, mk_ref, mv_ref, o_ref):
    bt = mk_ref.shape[0]
    hi = lax.broadcasted_iota(jnp.int32, (MEM_HEADS, MEM_W), 0)
    li = lax.broadcasted_iota(jnp.int32, (MEM_HEADS, MEM_W), 1)
    hm4 = ((li >= hi * MEM_HD) & (li < (hi + 1) * MEM_HD)).astype(F32)
    for j in range(bt):
        q4 = (qn_ref[j:j + 1, :] * hm4).astype(BF16)
        s = lax.dot_general(q4, mk_ref[j].astype(BF16), (((1,), (1,)), ((), ())), preferred_element_type=F32)
        p = _softmax_rows(s * (MEM_HD ** -0.5))
        o4 = jnp.dot(p.astype(BF16), mv_ref[j].astype(BF16), preferred_element_type=F32)
        o_ref[j:j + 1, :] = jnp.sum(o4 * hm4, axis=0, keepdims=True)


def _sample_mem(qn, mk, mv, layer, bt):
    n = qn.shape[0]
    n_mem = mk.shape[2]
    cache = pl.BlockSpec((None, bt, n_mem, MEM_W), lambda i: (layer, i, 0, 0))
    return pl.pallas_call(
        _sample_mem_body,
        grid=(n // bt,),
        in_specs=[pl.BlockSpec((bt, MEM_W), lambda i: (i, 0)), cache, cache],
        out_specs=pl.BlockSpec((bt, MEM_W), lambda i: (i, 0)),
        out_shape=jax.ShapeDtypeStruct((n, MEM_W), F32),
        compiler_params=_cparams("parallel"),
        name="sample_mem",
    )(qn, mk, mv)


def _ffn_res_body(x_ref, o_ref_in, wmo_ref, g_ref, w1_ref, w2_ref, out_ref):
    x = x_ref[...] + jnp.dot(o_ref_in[...].astype(BF16), wmo_ref[...], preferred_element_type=F32)
    xn = _rms(x, g_ref[...]).astype(BF16)
    acc = x
    for c in range(D_FF // FF_CHUNK):
        h = jnp.dot(xn, w1_ref[:, c * FF_CHUNK:(c + 1) * FF_CHUNK], preferred_element_type=F32)
        h = jnp.square(jnp.maximum(h, 0.0))
        acc = acc + jnp.dot(h.astype(BF16), w2_ref[c * FF_CHUNK:(c + 1) * FF_CHUNK, :], preferred_element_type=F32)
    out_ref[...] = acc


def _sample_ffn(x1, o, wmo, g3, w1, w2, layer):
    n = x1.shape[0]
    lay3 = lambda i: (layer, 0, 0)
    const = dict(pipeline_mode=pl.Buffered(1))
    return pl.pallas_call(
        _ffn_res_body,
        grid=(1,),
        in_specs=[
            pl.BlockSpec((n, D_MODEL), lambda i: (0, 0)),
            pl.BlockSpec((n, MEM_W), lambda i: (0, 0)),
            pl.BlockSpec((None, MEM_W, D_MODEL), lay3),
            pl.BlockSpec((None, 1, D_MODEL), lay3),
            pl.BlockSpec((None, D_MODEL, D_FF), lay3, **const),
            pl.BlockSpec((None, D_FF, D_MODEL), lay3, **const),
        ],
        out_specs=pl.BlockSpec((n, D_MODEL), lambda i: (0, 0)),
        out_shape=jax.ShapeDtypeStruct((n, D_MODEL), F32),
        compiler_params=_cparams("arbitrary"),
        name="sample_ffn",
    )(x1, o, wmo, g3, w1, w2)


def _pad_heads(a, axis, heads, width, padded):
    shp = a.shape
    a = a.reshape(shp[:axis] + (heads, width) + shp[axis + 1:])
    pad = [(0, 0)] * a.ndim
    pad[axis + 1] = (0, padded - width)
    a = jnp.pad(a, pad)
    return a.reshape(shp[:axis] + (heads * padded,) + shp[axis + 1:])


def _prep_w_in(w_in):
    offs = [0]
    for w in (GLA_QK, GLA_QK, GLA_V, GLA_RANK, GLA_V, RG_WIDTH, RG_WIDTH, SWA_Q, SWA_KVW, SWA_KVW):
        offs.append(offs[-1] + w)
    gq, gk, gv, glr, gog, rx, ry, sq, sk, sv = [w_in[..., offs[i]:offs[i + 1]] for i in range(10)]
    zeros = lambda n: jnp.zeros(w_in.shape[:-1] + (n,), w_in.dtype)
    cols = [rx, ry,
            gq, glr, zeros(GLA_QKP - GLA_QK - GLA_RANK),
            gk, zeros(GLA_QKP - GLA_QK),
            sq,
            _pad_heads(gv, 2, GLA_HEADS, GLA_DV, GLA_DVP),
            _pad_heads(gog, 2, GLA_HEADS, GLA_DV, GLA_DVP),
            sk, sv]
    out = jnp.concatenate(cols, axis=-1)
    assert out.shape[-1] == NP
    return out.astype(BF16)


def _block_diag(w):
    depth, nb, bw, _ = w.shape
    eye = jnp.eye(nb, dtype=w.dtype)
    return jnp.einsum("lncd,nm->lncmd", w, eye).reshape(depth, nb * bw, nb * bw)


def _block_ones(n, seg):
    i = jnp.arange(n) // seg
    return (i[:, None] == i[None, :]).astype(BF16)


def _dup_matrix():
    src = jnp.arange(SWA_Q)
    src = (src // HEAD_DIM // SWA_G) * HEAD_DIM + src % HEAD_DIM
    return (jnp.arange(SWA_KVW)[:, None] == src[None, :]).astype(BF16)


def kernel(x_prompt, x_sample, mem_prompt, state_gla, state_rg_h, state_rg_conv, cache_swa_k, cache_swa_v, cache_mem_k, cache_mem_v, g_mix, w_in, gla_w_gate2, gla_b_gate, gla_g_out, rg_conv_w, rg_conv_b, rg_w_a, rg_b_a, rg_w_x, rg_b_x, rg_lam, swa_g_q, swa_g_k, swa_sinks, w_out, g_mem_x, g_mem_m, mem_w_q, mem_w_kv, mem_g_q, mem_g_k, mem_w_o, g_ffn, ffn_w1, ffn_w2):
    depth = w_in.shape[0]
    nb, t_len, _ = x_prompt.shape
    ns = x_sample.shape[0]
    n_mem = mem_prompt.shape[1]
    assert t_len % ROW_T == 0 and t_len % MIX_T == 0 and x_sample.shape[1] == 1 and ns % 8 == 0
    row3 = lambda a: a.reshape(depth, 1, a.shape[-1])

    w_in_p = _prep_w_in(w_in)
    wg = jnp.zeros((depth, GLA_QKP, GLA_QKP), F32).at[:, GLR_LANE:GLR_LANE + GLA_RANK, :GLA_QK].set(gla_w_gate2)
    wg = wg.astype(BF16)
    bg3 = row3(jnp.pad(gla_b_gate, ((0, 0), (0, GLA_QKP - GLA_QK))))
    gout3 = row3(jnp.pad(gla_g_out, ((0, 0), (0, GLA_DVP - GLA_DV))))
    wa = _block_diag(rg_w_a).astype(BF16)
    wx = _block_diag(rg_w_x).astype(BF16)
    ba3, bx3, lam3, cb3 = row3(rg_b_a), row3(rg_b_x), row3(rg_lam), row3(rg_conv_b)
    gq3 = row3(jnp.tile(swa_g_q, (1, SWA_HEADS)))
    gk3 = row3(jnp.tile(swa_g_k, (1, SWA_KV)))
    wo = jnp.concatenate([_pad_heads(w_out[:, :GLA_V], 1, GLA_HEADS, GLA_DV, GLA_DVP), w_out[:, GLA_V:]], axis=1)
    wo = wo.astype(BF16)
    gmx3, gmm3, gffn3, gmix3 = row3(g_mem_x), row3(g_mem_m), row3(g_ffn), row3(g_mix)
    mgq3 = row3(jnp.tile(mem_g_q, (1, MEM_HEADS)))
    mgk3 = row3(jnp.tile(mem_g_k, (1, MEM_HEADS)))
    wq = mem_w_q.astype(BF16)
    wkv = mem_w_kv.astype(BF16)
    wmo = mem_w_o.astype(BF16)
    w1 = ffn_w1.astype(BF16)
    w2 = ffn_w2.astype(BF16)
    b256 = _block_ones(SWA_Q, HEAD_DIM)
    b128 = _block_ones(SWA_KVW, HEAD_DIM)
    dup = _dup_matrix()
    eye256 = jnp.eye(GLA_QKP, dtype=BF16)

    mk_all, mv_all = _mem_kv(mem_prompt.reshape(nb * n_mem, D_MODEL), gmm3, wkv, mgk3, b256)
    mk_all = mk_all.reshape(depth, nb, n_mem, MEM_W)
    mv_all = mv_all.reshape(depth, nb, n_mem, MEM_W)

    xp = x_prompt.reshape(nb * t_len, D_MODEL)
    xs = x_sample.reshape(ns, D_MODEL)
    rg_conv_flat = state_rg_conv.reshape(depth, ns, (CONV_W - 1) * RG_WIDTH)
    gla_flat = state_gla.reshape(depth, ns, GLA_QK, GLA_DV)
    swk = cache_swa_k.reshape(depth, ns, WINDOW, SWA_KVW)
    swv = cache_swa_v.reshape(depth, ns, WINDOW, SWA_KVW)
    cmk = cache_mem_k.reshape(depth, ns, n_mem, MEM_W)
    cmv = cache_mem_v.reshape(depth, ns, n_mem, MEM_W)

    gla_p, gla_s, rgh_p, rgh_s, rgc_p, rgc_s = [], [], [], [], [], []
    swk_p, swk_s, swv_p, swv_s = [], [], [], []
    ts = min(ns, 128)
    for l in range(depth):
        xp, st, hl, cbuf, kb, vb = _mixer_prompt(
            swa_sinks, xp.reshape(nb, t_len, D_MODEL), gmix3, w_in_p, wg, bg3, gout3, rg_conv_w, cb3, wa, ba3, wx,
            bx3, lam3, gq3, gk3, b256, b128, dup, wo, gmx3, wq, mgq3, mk_all, mv_all, wmo, l)
        xp = _ffn(xp.reshape(nb * t_len, D_MODEL), gffn3, w1, w2, l, ROW_T)
        st = st.reshape(nb, GLA_HEADS, GLA_DVP, GLA_QKP)
        gla_p.append(jnp.stack(
            [st[:, h, :GLA_DV, h * GLA_DK:(h + 1) * GLA_DK].transpose(0, 2, 1) for h in range(GLA_HEADS)], axis=1))
        rgh_p.append(hl.reshape(nb, RG_WIDTH))
        rgc_p.append(cbuf)
        swk_p.append(kb.reshape(nb, WINDOW, SWA_KV, HEAD_DIM))
        swv_p.append(vb.reshape(nb, WINDOW, SWA_KV, HEAD_DIM))

        hs = _in_proj(xs, gmix3, w_in_p, l, ts)
        (qs, kg, eg, org_s, hn, cnew, qn, kn, kx, vx) = _sample_rows(
            hs, state_rg_h, rg_conv_flat, wg, bg3, rg_conv_w, cb3, wa, ba3, wx, bx3, lam3, gq3, gk3, b256, b128,
            dup, l)
        og_s, sn = _sample_gla(gla_flat, qs, kg, eg, hs, gout3, eye256, l, 8)
        osw_s, ko, vo = _sample_swa(swa_sinks, swk, swv, qn, kn, kx, vx, hs, dup, l, 8)
        x1, qm = _sample_post1(xs, og_s, org_s, osw_s, wo, gmx3, wq, mgq3, b256, l)
        om = _sample_mem(qm, cmk, cmv, l, 8)
        xs = _sample_ffn(x1, om, wmo, gffn3, w1, w2, l)
        gla_s.append(sn.reshape(ns, GLA_HEADS, GLA_DK, GLA_DV))
        rgh_s.append(hn)
        rgc_s.append(cnew.reshape(ns, CONV_W - 1, RG_WIDTH))
        swk_s.append(ko.reshape(ns, WINDOW, SWA_KV, HEAD_DIM))
        swv_s.append(vo.reshape(ns, WINDOW, SWA_KV, HEAD_DIM))

    return (xp.reshape(nb, t_len, D_MODEL), xs.reshape(ns, 1, D_MODEL),
            jnp.stack(gla_p), jnp.stack(gla_s), jnp.stack(rgh_p), jnp.stack(rgh_s),
            jnp.stack(rgc_p), jnp.stack(rgc_s), jnp.stack(swk_p), jnp.stack(swk_s),
            jnp.stack(swv_p), jnp.stack(swv_s),
            mk_all.reshape(depth, nb, n_mem, MEM_HEADS, MEM_HD), mv_all.reshape(depth, nb, n_mem, MEM_HEADS, MEM_HD))
```

```python
import functools
import math

import jax
import jax.numpy as jnp
from jax import lax
from jax.experimental import pallas as pl
from jax.experimental.pallas import tpu as pltpu

F32 = jnp.float32
BF16 = jnp.bfloat16

D_MODEL = 1024
EPS = 1e-6
GLA_HEADS, GLA_DK, GLA_DV, GLA_RANK, GLA_TAU = 4, 48, 96, 16, 16.0
GLA_QK = GLA_HEADS * GLA_DK
GLA_V = GLA_HEADS * GLA_DV
RG_WIDTH, RG_BLOCKS, RG_C, CONV_W = 384, 4, 8.0, 4
RG_BW = RG_WIDTH // RG_BLOCKS
SWA_HEADS, SWA_KV, HEAD_DIM, WINDOW = 4, 2, 64, 128
SWA_G = SWA_HEADS // SWA_KV
SWA_Q = SWA_HEADS * HEAD_DIM
SWA_KVW = SWA_KV * HEAD_DIM
MEM_HEADS, MEM_HD = 4, 64
MEM_W = MEM_HEADS * MEM_HD
D_FF = 4 * D_MODEL

LANE = 128
SUBLANE = 8
GLA_DVP = LANE
GLA_VP = GLA_HEADS * GLA_DVP
GLA_QKP = 2 * LANE
MIX_W = GLA_VP + RG_WIDTH + SWA_Q

C_RX, C_RY, C_GQ, C_GK, C_SQ, C_GV, C_GOG, C_SK, C_SV = 0, 384, 768, 1024, 1280, 1536, 2048, 2560, 2688
NP = 2816
GLR_LANE = GLA_QK

VMEM_LIMIT = 56 * 1024 * 1024
NEG = -1e30

MIX_T = 256
ROW_T = 512
FF_CHUNK = 1024


def _cparams(*sem):
    return pltpu.CompilerParams(dimension_semantics=sem, vmem_limit_bytes=VMEM_LIMIT)


def _rms(x, g):
    ms = jnp.mean(x * x, axis=-1, keepdims=True)
    return (x * lax.rsqrt(ms + EPS)) * g


def _dot(a, b):
    return jnp.dot(a.astype(BF16), b.astype(BF16), preferred_element_type=F32)


def _dot_nt(a, b):
    return lax.dot_general(a.astype(BF16), b.astype(BF16), (((1,), (1,)), ((), ())), preferred_element_type=F32)


def _split2_dot(x, m):
    hi = x.astype(BF16)
    lo = (x - hi.astype(F32)).astype(BF16)
    return jnp.dot(hi, m, preferred_element_type=F32) + jnp.dot(lo, m, preferred_element_type=F32)


def _split3(x):
    x1 = x.astype(BF16)
    r1 = x - x1.astype(F32)
    x2 = r1.astype(BF16)
    x3 = (r1 - x2.astype(F32)).astype(BF16)
    return x1, x2, x3


def _split3_dot_left(m, x):
    d = functools.partial(jnp.dot, preferred_element_type=F32)
    x1, x2, x3 = _split3(x)
    return d(m, x1) + d(m, x2) + d(m, x3)


def _seg_rms(x, bones, seg, g):
    ms = _split2_dot(x * x, bones) * (1.0 / seg)
    return (x * lax.rsqrt(ms + EPS)) * g


def _sigmoid(x):
    return 1.0 / (1.0 + jnp.exp(-x))


def _softplus(x):
    return jnp.maximum(x, 0.0) + jnp.log1p(jnp.exp(-jnp.abs(x)))


def _log_sigmoid(z):
    return -_softplus(-z)


def _gelu_tanh(x):
    return x * (0.5 * (1.0 + jnp.tanh(math.sqrt(2.0 / math.pi) * (x + 0.044715 * (x * x * x)))))


def _lane_head_mask(width, seg, h, dtype=F32):
    lane = lax.broadcasted_iota(jnp.int32, (1, width), 1)
    return ((lane >= seg * h) & (lane < seg * (h + 1))).astype(dtype)


def _masked_softmax_rows(s, sink):
    m = jnp.maximum(jnp.max(s, axis=-1, keepdims=True), sink)
    e = jnp.exp(s - m)
    den = jnp.sum(e, axis=-1, keepdims=True) + jnp.exp(sink - m)
    return e / den


def _softmax_rows(s):
    m = jnp.max(s, axis=-1, keepdims=True)
    e = jnp.exp(s - m)
    return e / jnp.sum(e, axis=-1, keepdims=True)


def _alibi_slope(h):
    return 2.0 ** (-8.0 * (h + 1) / SWA_HEADS)


def _rows(x, starts, size):
    parts = [x[s:s + size] for s in starts]
    return parts[0] if len(parts) == 1 else jnp.concatenate(parts, axis=0)


def _in_proj_body(x_ref, g_ref, w_ref, o_ref):
    xn = _rms(x_ref[...], g_ref[...])
    o_ref[...] = jnp.dot(xn.astype(BF16), w_ref[...], preferred_element_type=F32)


def _in_proj(x2d, g3, w3, layer, tm):
    m = x2d.shape[0]
    return pl.pallas_call(
        _in_proj_body,
        grid=(m // tm,),
        in_specs=[
            pl.BlockSpec((tm, D_MODEL), lambda i: (i, 0)),
            pl.BlockSpec((None, 1, D_MODEL), lambda i: (layer, 0, 0)),
            pl.BlockSpec((None, D_MODEL, NP), lambda i: (layer, 0, 0)),
        ],
        out_specs=pl.BlockSpec((tm, NP), lambda i: (i, 0)),
        out_shape=jax.ShapeDtypeStruct((m, NP), F32),
        compiler_params=_cparams("parallel"),
        name="in_proj",
    )(x2d, g3, w3)


def _ffn_body(x_ref, g_ref, w1_ref, w2_ref, o_ref):
    x = x_ref[...]
    xn = _rms(x, g_ref[...]).astype(BF16)
    acc = x
    for c in range(D_FF // FF_CHUNK):
        h = jnp.dot(xn, w1_ref[:, c * FF_CHUNK:(c + 1) * FF_CHUNK], preferred_element_type=F32)
        h = jnp.square(jnp.maximum(h, 0.0))
        acc = acc + jnp.dot(h.astype(BF16), w2_ref[c * FF_CHUNK:(c + 1) * FF_CHUNK, :], preferred_element_type=F32)
    o_ref[...] = acc


def _ffn(x2d, g3, w1, w2, layer, tm):
    m = x2d.shape[0]
    const = dict(pipeline_mode=pl.Buffered(1))
    return pl.pallas_call(
        _ffn_body,
        grid=(m // tm,),
        in_specs=[
            pl.BlockSpec((tm, D_MODEL), lambda i: (i, 0)),
            pl.BlockSpec((None, 1, D_MODEL), lambda i: (layer, 0, 0)),
            pl.BlockSpec((None, D_MODEL, D_FF), lambda i: (layer, 0, 0), **const),
            pl.BlockSpec((None, D_FF, D_MODEL), lambda i: (layer, 0, 0), **const),
        ],
        out_specs=pl.BlockSpec((tm, D_MODEL), lambda i: (i, 0)),
        out_shape=jax.ShapeDtypeStruct((m, D_MODEL), F32),
        compiler_params=_cparams("parallel"),
        name="ffn",
    )(x2d, g3, w1, w2)


def _mem_kv_body(mem_ref, g_ref, w_ref, gk_ref, bones_ref, k_ref, v_ref):
    kv = jnp.dot(_rms(mem_ref[...], g_ref[...]).astype(BF16), w_ref[...], preferred_element_type=F32)
    k_ref[...] = _seg_rms(kv[:, :MEM_W], bones_ref[...], MEM_HD, gk_ref[...])
    v_ref[...] = kv[:, MEM_W:]


def _mem_kv(mem2d, g3, wkv, gk3, bones256):
    depth = wkv.shape[0]
    m = mem2d.shape[0]
    return pl.pallas_call(
        _mem_kv_body,
        grid=(depth,),
        in_specs=[
            pl.BlockSpec((m, D_MODEL), lambda l: (0, 0)),
            pl.BlockSpec((None, 1, D_MODEL), lambda l: (l, 0, 0)),
            pl.BlockSpec((None, D_MODEL, 2 * MEM_W), lambda l: (l, 0, 0)),
            pl.BlockSpec((None, 1, MEM_W), lambda l: (l, 0, 0)),
            pl.BlockSpec((MEM_W, MEM_W), lambda l: (0, 0)),
        ],
        out_specs=[pl.BlockSpec((None, m, MEM_W), lambda l: (l, 0, 0))] * 2,
        out_shape=[jax.ShapeDtypeStruct((depth, m, MEM_W), F32)] * 2,
        compiler_params=_cparams("parallel"),
        name="mem_kv",
    )(mem2d, g3, wkv, gk3, bones256)


def _mix_out(x, og, orr, osw, wo_ref):
    y = _dot(og, wo_ref[0:GLA_VP, :])
    y = y + _dot(orr, wo_ref[GLA_VP:GLA_VP + RG_WIDTH, :])
    y = y + _dot(osw, wo_ref[GLA_VP + RG_WIDTH:MIX_W, :])
    return x + y


def _mem_q(x1, gx, wq_ref, gq, bones):
    q = jnp.dot(_rms(x1, gx).astype(BF16), wq_ref[...], preferred_element_type=F32)
    return _seg_rms(q, bones, MEM_HD, gq)


def _gla_tile(h_scr, wg_ref, bg_ref, gout_ref, st_scr, cum_scr, d_scr, oc_scr):
    tt = h_scr.shape[0]
    hf = tt // 2
    qraw = h_scr[:, C_GQ:C_GQ + GLA_QKP]
    k = h_scr[:, C_GK:C_GK + GLA_QKP]
    z = jnp.dot(qraw.astype(BF16), wg_ref[...], preferred_element_type=F32) + bg_ref[...]
    g = _log_sigmoid(z) * (1.0 / GLA_TAU)
    q = qraw * (GLA_DK ** -0.5)

    row = lax.broadcasted_iota(jnp.int32, (tt, tt), 0)
    col = lax.broadcasted_iota(jnp.int32, (tt, tt), 1)
    cum = _split3_dot_left((row >= col).astype(BF16), g)
    cum_scr[...] = cum
    rowi = lax.broadcasted_iota(jnp.int32, (tt, 1), 0)

    def boundary(b, n):
        return jnp.concatenate(
            [jnp.broadcast_to(cum_scr[pl.ds(gi * 2 * b + b - 1, 1), :], (n, GLA_QKP)) for gi in range(tt // (2 * b))],
            axis=0)

    rd1 = pltpu.roll(cum, 1, 0)
    rd2 = pltpu.roll(cum, 2, 0)
    ru1 = pltpu.roll(cum, tt - 1, 0)
    m4 = rowi & 3
    bounds = {
        1: jnp.where((rowi & 1) == 0, cum, rd1),
        2: jnp.where(m4 == 0, ru1, jnp.where(m4 == 1, cum, jnp.where(m4 == 2, rd1, rd2))),
        4: boundary(4, 2 * 4),
    }
    low = [(q.astype(BF16), k.astype(BF16), 0)]
    for b in (1, 2, 4):
        second = (rowi & (2 * b - 1)) >= b
        c = bounds[b]
        qs = q * jnp.exp(jnp.where(second, cum - c, NEG))
        ks = k * jnp.exp(jnp.where(second, NEG, c - cum))
        low.append((qs.astype(BF16), ks.astype(BF16), int(math.log2(2 * b))))

    mid = []
    b = SUBLANE
    while b <= hf:
        ng = tt // (2 * b)
        c = boundary(b, b)
        firsts = [gi * 2 * b for gi in range(ng)]
        seconds = [gi * 2 * b + b for gi in range(ng)]
        qs2 = _rows(q, seconds, b) * jnp.exp(_rows(cum, seconds, b) - c)
        ks1 = _rows(k, firsts, b) * jnp.exp(c - _rows(cum, firsts, b))
        if b < hf:
            zb = jnp.zeros((b, GLA_QKP), F32)
            ks1 = jnp.concatenate([piece for gi in range(ng) for piece in (ks1[gi * b:(gi + 1) * b], zb)], axis=0)
        mid.append((b, qs2.astype(BF16), ks1.astype(BF16)))
        b *= 2

    st = st_scr[...]
    o_inter = _dot_nt(q * jnp.exp(cum), st)
    last = cum_scr[pl.ds(tt - 1, 1), :]
    kd = k * jnp.exp(last - cum)
    v_all = h_scr[:, C_GV:C_GV + GLA_VP]
    upd = jnp.dot(v_all.T.astype(BF16), kd.astype(BF16), preferred_element_type=F32)
    vrow = lax.broadcasted_iota(jnp.int32, (GLA_VP, 1), 0) // GLA_DVP
    lane = lax.broadcasted_iota(jnp.int32, (1, GLA_QKP), 1)
    khead = ((lane >= GLA_DK).astype(jnp.int32) + (lane >= 2 * GLA_DK).astype(jnp.int32)
             + (lane >= 3 * GLA_DK).astype(jnp.int32) + (lane >= 4 * GLA_DK).astype(jnp.int32))
    st_new = jnp.where(vrow == khead, st * jnp.exp(last) + upd, 0.0)
    st_scr[...] = st_new

    nh = GLA_HEADS
    hq = hf // 2
    c128 = lax.broadcasted_iota(jnp.int32, (1, hf), 1)
    r_low = lax.broadcasted_iota(jnp.int32, (nh * hf, 1), 0) & (hf - 1)
    r_mid = lax.broadcasted_iota(jnp.int32, (nh * hq, 1), 0) & (hq - 1)
    low_masks = [(r_low >> s) == (c128 >> s) for (_, _, s) in low]
    mid_masks = {bb: (r_mid >> int(math.log2(bb))) == (c128 >> int(math.log2(2 * bb)))
                 for (bb, _, _) in mid if 2 * bb < hf}
    nt = (((1,), (1,)), ((), ()))
    hms = [_lane_head_mask(GLA_QKP, GLA_DK, h, BF16) for h in range(nh)]
    stack_heads = lambda xb: jnp.concatenate([xb * hm for hm in hms], axis=0)
    for half in range(2):
        rs = slice(half * hf, (half + 1) * hf)
        acc = None
        for (qb, kb, _), mk in zip(low, low_masks):
            p = lax.dot_general(stack_heads(qb[rs]), kb[rs], nt, preferred_element_type=F32)
            p = jnp.where(mk, p, 0.0)
            acc = p if acc is None else acc + p
        for h in range(nh):
            d_scr[h, half] = acc[h * hf:(h + 1) * hf]
        for bb, qb2, kb1 in mid:
            if bb == hf:
                continue
            cs = slice(half * hq, (half + 1) * hq)
            p = lax.dot_general(stack_heads(qb2[cs]), kb1[rs], nt, preferred_element_type=F32)
            if bb in mid_masks:
                p = jnp.where(mid_masks[bb], p, 0.0)
            for h in range(nh):
                for gi in range(hf // (2 * bb)):
                    dst = pl.ds(gi * 2 * bb + bb, bb)
                    src = slice(h * hq + gi * bb, h * hq + (gi + 1) * bb)
                    d_scr[h, half, dst, :] = d_scr[h, half, dst, :] + p[src, :]
    _, qb2, kb1 = mid[-1]
    off_all = lax.dot_general(stack_heads(qb2), kb1, nt, preferred_element_type=F32)
    gout = gout_ref[...]
    for h in range(nh):
        off = off_all[h * hf:(h + 1) * hf]
        sl = slice(h * GLA_DVP, (h + 1) * GLA_DVP)
        v_h = v_all[:, sl].astype(BF16)
        o0 = jnp.dot(d_scr[h, 0].astype(BF16), v_h[0:hf], preferred_element_type=F32)
        a1 = jnp.concatenate([off, d_scr[h, 1]], axis=1).astype(BF16)
        o1 = jnp.dot(a1, v_h, preferred_element_type=F32)
        o = jnp.concatenate([o0, o1], axis=0) + o_inter[:, sl]
        ms = jnp.sum(o * o, axis=-1, keepdims=True) * (1.0 / GLA_DV)
        gate = h_scr[:, C_GOG + h * GLA_DVP:C_GOG + (h + 1) * GLA_DVP]
        oc_scr[:, sl] = ((o * lax.rsqrt(ms + EPS)) * gout) * (gate * _sigmoid(gate))
    return st_new


def _rg_gates(xc, wa_ref, ba, wx_ref, bx, lam):
    r = _sigmoid(jnp.dot(xc.astype(BF16), wa_ref[...], preferred_element_type=F32) + ba)
    i = _sigmoid(jnp.dot(xc.astype(BF16), wx_ref[...], preferred_element_type=F32) + bx)
    log_a = (-RG_C * r) * _softplus(-lam)
    a = jnp.exp(log_a)
    b = jnp.sqrt(1.0 - jnp.exp(2.0 * log_a)) * (i * xc)
    return a, b


def _rg_pitch(tt):
    return tt // SUBLANE + SUBLANE


def _rg_tiles(h_scr, cw_ref, cb_ref, wa_ref, ba_ref, wx_ref, bx_ref, lam_ref, hc_scr, cc_scr, ra_scr, rb_scr, rh_scr,
              rp_scr, oc_scr):
    nb, tt = len(h_scr), h_scr[0].shape[0]
    rowi = lax.broadcasted_iota(jnp.int32, (tt, 1), 0)
    nl = RG_WIDTH // LANE
    seg = tt // SUBLANE
    pitch = _rg_pitch(tt)
    cw = cw_ref[...]
    for bi in range(nb):
        rx = h_scr[bi][:,C_RX:C_RX + RG_WIDTH]
        p0, p1, p2 = cc_scr[bi, 0:1, :], cc_scr[bi, 1:2, :], cc_scr[bi, 2:3, :]
        x1 = jnp.where(rowi == 0, p2, pltpu.roll(rx, 1, 0))
        x2 = jnp.where(rowi == 0, p1, jnp.where(rowi == 1, p2, pltpu.roll(rx, 2, 0)))
        x3 = jnp.where(rowi == 0, p0, jnp.where(rowi == 1, p1, jnp.where(rowi == 2, p2, pltpu.roll(rx, 3, 0))))
        xc = cb_ref[...] + x3 * cw[0:1, :]
        xc = xc + x2 * cw[1:2, :]
        xc = xc + x1 * cw[2:3, :]
        xc = xc + rx * cw[3:4, :]
        a, b = _rg_gates(xc, wa_ref, ba_ref[...], wx_ref, bx_ref[...], lam_ref[...])
        for c in range(nl):
            for j in range(SUBLANE):
                dst = pl.ds(j * pitch, seg)
                ra_scr[bi, c, dst, :] = a[j * seg:(j + 1) * seg, c * LANE:(c + 1) * LANE]
                rb_scr[bi, c, dst, :] = b[j * seg:(j + 1) * seg, c * LANE:(c + 1) * LANE]

    chains = [(bi, c) for bi in range(nb) for c in range(nl)]
    hloc = {ch: jnp.zeros((SUBLANE, LANE), F32) for ch in chains}
    pc = {ch: jnp.ones((SUBLANE, LANE), F32) for ch in chains}
    for i in range(seg):
        idx = pl.ds(i, SUBLANE, stride=pitch)
        for ch in chains:
            bi, c = ch
            ai = ra_scr[bi, c, idx, :]
            hloc[ch] = ai * hloc[ch] + rb_scr[bi, c, idx, :]
            pc[ch] = ai * pc[ch]
            rh_scr[bi, c, idx, :] = hloc[ch]
            rp_scr[bi, c, idx, :] = pc[ch]
    outs = []
    for bi in range(nb):
        hl = jnp.concatenate([hloc[(bi, c)] for c in range(nl)], axis=1)
        pl_ = jnp.concatenate([pc[(bi, c)] for c in range(nl)], axis=1)
        hprev = hc_scr[bi, 0:1, :]
        carries = []
        for j in range(SUBLANE):
            carries.append(jnp.broadcast_to(hprev, (seg, RG_WIDTH)))
            hprev = hl[j:j + 1, :] + pl_[j:j + 1, :] * hprev
        hc_scr[bi, 0:1, :] = hprev
        carry = jnp.concatenate(carries, axis=0)
        unpitch = lambda r, c: jnp.concatenate([r[bi, c, pl.ds(j * pitch, seg), :] for j in range(SUBLANE)], axis=0)
        h = jnp.concatenate(
            [unpitch(rh_scr, c) + unpitch(rp_scr, c) * carry[:, c * LANE:(c + 1) * LANE] for c in range(nl)], axis=1)
        oc_scr[bi][:, GLA_VP:GLA_VP + RG_WIDTH] = h * _gelu_tanh(h_scr[bi][:,C_RY:C_RY + RG_WIDTH])
        tail = h_scr[bi][pl.ds(tt - (CONV_W - 1), CONV_W - 1), C_RX:C_RX + RG_WIDTH]
        cc_scr[bi, 0:CONV_W - 1, :] = tail
        outs.append((hprev, tail))
    return outs


def _stack_heads(x, masks):
    return jnp.concatenate([x * m for m in masks], axis=0)


def _unstack_heads(x4, masks):
    n = x4.shape[0] // len(masks)
    out = masks[0] * x4[0:n]
    for h in range(1, len(masks)):
        out = out + masks[h] * x4[h * n:(h + 1) * n]
    return out


def _swa_tiles(h_scr, sink_ref, layer, first, gq_ref, gk_ref, b256_ref, b128_ref, dup_ref, kp_scr, vp_scr, oc_scr):
    nb, tt = len(h_scr), h_scr[0].shape[0]
    w = WINDOW
    nh = SWA_HEADS
    hmb = [_lane_head_mask(SWA_Q, HEAD_DIM, h, BF16) for h in range(nh)]
    hmf = [_lane_head_mask(SWA_Q, HEAD_DIM, h) for h in range(nh)]
    qi = lax.broadcasted_iota(jnp.int32, (w, 2 * w), 0)
    kj = lax.broadcasted_iota(jnp.int32, (w, 2 * w), 1)
    dist_i = qi + w - kj
    in_window = (dist_i >= 0) & (dist_i <= WINDOW)
    dist = dist_i.astype(F32)
    bias = jnp.concatenate([jnp.where(in_window, -_alibi_slope(h) * dist, NEG) for h in range(nh)], axis=0)
    kj4 = lax.broadcasted_iota(jnp.int32, (nh * w, 2 * w), 1)
    bias_first = jnp.where(jnp.logical_and(first, kj4 < w), NEG, bias)
    sink_col = jnp.concatenate([jnp.full((w, 1), sink_ref[layer, h], F32) for h in range(nh)], axis=0)
    dup = dup_ref[...]
    nt = (((1,), (1,)), ((), ()))
    s_parts, v_parts, wins = [], [], []
    for bi in range(nb):
        qn = _seg_rms(h_scr[bi][:,C_SQ:C_SQ + SWA_Q], b256_ref[...], HEAD_DIM, gq_ref[...])
        qn = (qn * (HEAD_DIM ** -0.5)).astype(BF16)
        kn = _seg_rms(h_scr[bi][:,C_SK:C_SK + SWA_KVW], b128_ref[...], HEAD_DIM, gk_ref[...])
        v = h_scr[bi][:,C_SV:C_SV + SWA_KVW]
        kcat = jnp.concatenate([kp_scr[bi], kn], axis=0).astype(BF16)
        vcat = jnp.concatenate([vp_scr[bi], v], axis=0).astype(BF16)
        kexp = jnp.dot(kcat, dup, preferred_element_type=F32).astype(BF16)
        vexp = jnp.dot(vcat, dup, preferred_element_type=F32).astype(BF16)
        for blk in range(tt // w):
            q4 = _stack_heads(qn[blk * w:(blk + 1) * w], hmb)
            s = lax.dot_general(q4, kexp[blk * w:(blk + 2) * w], nt, preferred_element_type=F32)
            s_parts.append(s + (bias_first if blk == 0 else bias))
            v_parts.append(vexp[blk * w:(blk + 2) * w])
        k_win = kn[tt - w:tt]
        v_win = v[tt - w:tt]
        kp_scr[bi] = k_win
        vp_scr[bi] = v_win
        wins.append((k_win, v_win))
    p_all = _masked_softmax_rows(jnp.concatenate(s_parts, axis=0),
                                 jnp.concatenate([sink_col] * len(s_parts), axis=0)).astype(BF16)
    n4 = nh * w
    for i, vb in enumerate(v_parts):
        bi, blk = divmod(i, tt // w)
        o4 = jnp.dot(p_all[i * n4:(i + 1) * n4], vb, preferred_element_type=F32)
        oc_scr[bi][blk * w:(blk + 1) * w, GLA_VP + RG_WIDTH:MIX_W] = _unstack_heads(o4, hmf)
    return wins


def _mem_attend_tiles(x1, nb, gx, wq_ref, gq, bones, mk_ref, mv_ref, wmo_ref):
    nh = MEM_HEADS
    tt = x1.shape[0] // nb
    hmb = [_lane_head_mask(MEM_W, MEM_HD, h, BF16) for h in range(nh)]
    hmf = [_lane_head_mask(MEM_W, MEM_HD, h) for h in range(nh)]
    nt = (((1,), (1,)), ((), ()))
    qn = (_mem_q(x1, gx, wq_ref, gq, bones) * (MEM_HD ** -0.5)).astype(BF16)
    s_parts = [lax.dot_general(_stack_heads(qn[bi * tt:(bi + 1) * tt], hmb), mk_ref[bi].astype(BF16), nt,
                               preferred_element_type=F32) for bi in range(nb)]
    p_all = _softmax_rows(jnp.concatenate(s_parts, axis=0)).astype(BF16)
    n4 = nh * tt
    o = jnp.concatenate(
        [_unstack_heads(jnp.dot(p_all[bi * n4:(bi + 1) * n4], mv_ref[bi].astype(BF16), preferred_element_type=F32),
                        hmf) for bi in range(nb)], axis=0)
    return x1 + jnp.dot(o.astype(BF16), wmo_ref[...], preferred_element_type=F32)


def _mixer_body(sink_ref, x_ref, gmix_ref, win_ref, wg_ref, bg_ref, gout_ref, cw_ref, cb_ref, wa_ref, ba_ref,
                wx_ref, bx_ref, lam_ref, gq_ref, gk_ref, b256_ref, b128_ref, dup_ref, wo_ref, gmx_ref, wq_ref,
                mgq_ref, mk_ref, mv_ref, wmo_ref,
                o_ref, st_ref, hl_ref, cbuf_ref, ko_ref, vo_ref,
                h_scr, oc_scr, st_scr, cum_scr, d_scr, hc_scr, cc_scr, ra_scr, rb_scr, rh_scr, rp_scr, kp_scr, vp_scr,
                *, layer):
    nb = x_ref.shape[0]
    first = pl.program_id(0) == 0

    @pl.when(first)
    def _():
        for scr in (st_scr, hc_scr, cc_scr, kp_scr, vp_scr):
            scr[...] = jnp.zeros_like(scr)

    tt = x_ref.shape[1]
    xn = jnp.concatenate([_rms(x_ref[bi], gmix_ref[...]).astype(BF16) for bi in range(nb)], axis=0)
    h_scr[...] = jnp.dot(xn, win_ref[...], preferred_element_type=F32)
    hs = [h_scr.at[pl.ds(bi * tt, tt)] for bi in range(nb)]
    ocs = [oc_scr.at[pl.ds(bi * tt, tt)] for bi in range(nb)]
    for bi in range(nb):
        st_ref[bi] = _gla_tile(hs[bi], wg_ref, bg_ref, gout_ref, st_scr.at[bi], cum_scr.at[bi], d_scr.at[bi], ocs[bi])
    rg_out = _rg_tiles(hs, cw_ref, cb_ref, wa_ref, ba_ref, wx_ref, bx_ref, lam_ref, hc_scr, cc_scr, ra_scr, rb_scr,
                       rh_scr, rp_scr, ocs)
    for bi, (h_last, tail) in enumerate(rg_out):
        hl_ref[bi] = h_last
        cbuf_ref[bi] = tail
    wins = _swa_tiles(hs, sink_ref, layer, first, gq_ref, gk_ref, b256_ref, b128_ref, dup_ref, kp_scr, vp_scr, ocs)
    for bi, (k_win, v_win) in enumerate(wins):
        ko_ref[bi] = k_win
        vo_ref[bi] = v_win
    x1 = jnp.concatenate([x_ref[bi] for bi in range(nb)], axis=0)
    x1 = x1 + jnp.dot(oc_scr[...].astype(BF16), wo_ref[...], preferred_element_type=F32)
    x2 = _mem_attend_tiles(x1, nb, gmx_ref[...], wq_ref, mgq_ref[...], b256_ref[...], mk_ref, mv_ref, wmo_ref)
    for bi in range(nb):
        o_ref[bi] = x2[bi * tt:(bi + 1) * tt]


def _mixer_prompt(sinks, x3d, gmix3, w_in_p, wg, bg3, gout3, cw, cb3, wa, ba3, wx, bx3, lam3, gq3, gk3, b256, b128,
                  dup, wo, gmx3, wq, mgq3, mk, mv, wmo, layer):
    nb, t_len, _ = x3d.shape
    tt = MIX_T
    n_mem = mk.shape[2]
    lay3 = lambda t: (layer, 0, 0)
    once = dict(pipeline_mode=pl.Buffered(1))
    vec = lambda w: pl.BlockSpec((None, 1, w), lay3)
    full = lambda a: pl.BlockSpec(a.shape, lambda t: (0,) * a.ndim)
    state = lambda r, w: pl.BlockSpec((nb, r, w), lambda t: (0, 0, 0))
    mem = pl.BlockSpec((None, nb, n_mem, MEM_W), lambda t: (layer, 0, 0, 0))
    return pl.pallas_call(
        functools.partial(_mixer_body, layer=layer),
        grid=(t_len // tt,),
        in_specs=[
            pl.BlockSpec(memory_space=pltpu.SMEM),
            pl.BlockSpec((nb, tt, D_MODEL), lambda t: (0, t, 0)),
            vec(D_MODEL),
            pl.BlockSpec((None, D_MODEL, NP), lay3, **once),
            pl.BlockSpec((None, GLA_QKP, GLA_QKP), lay3), vec(GLA_QKP), vec(GLA_DVP),
            pl.BlockSpec((None, CONV_W, RG_WIDTH), lay3), vec(RG_WIDTH),
            pl.BlockSpec((None, RG_WIDTH, RG_WIDTH), lay3), vec(RG_WIDTH),
            pl.BlockSpec((None, RG_WIDTH, RG_WIDTH), lay3), vec(RG_WIDTH), vec(RG_WIDTH),
            vec(SWA_Q), vec(SWA_KVW), full(b256), full(b128), full(dup),
            pl.BlockSpec((None, MIX_W, D_MODEL), lay3, **once),
            vec(D_MODEL), pl.BlockSpec((None, D_MODEL, MEM_W), lay3), vec(MEM_W),
            mem, mem,
            pl.BlockSpec((None, MEM_W, D_MODEL), lay3),
        ],
        out_specs=[
            pl.BlockSpec((nb, tt, D_MODEL), lambda t: (0, t, 0)),
            state(GLA_VP, GLA_QKP), state(1, RG_WIDTH), state(CONV_W - 1, RG_WIDTH),
            state(WINDOW, SWA_KVW), state(WINDOW, SWA_KVW),
        ],
        out_shape=[
            jax.ShapeDtypeStruct((nb, t_len, D_MODEL), F32),
            jax.ShapeDtypeStruct((nb, GLA_VP, GLA_QKP), F32),
            jax.ShapeDtypeStruct((nb, 1, RG_WIDTH), F32),
            jax.ShapeDtypeStruct((nb, CONV_W - 1, RG_WIDTH), F32),
            jax.ShapeDtypeStruct((nb, WINDOW, SWA_KVW), F32),
            jax.ShapeDtypeStruct((nb, WINDOW, SWA_KVW), F32),
        ],
        scratch_shapes=[
            pltpu.VMEM((nb * tt, NP), F32), pltpu.VMEM((nb * tt, MIX_W), F32),
            pltpu.VMEM((nb, GLA_VP, GLA_QKP), F32), pltpu.VMEM((nb, tt, GLA_QKP), F32),
            pltpu.VMEM((nb, GLA_HEADS, 2, tt // 2, tt // 2), F32),
            pltpu.VMEM((nb, SUBLANE, RG_WIDTH), F32), pltpu.VMEM((nb, SUBLANE, RG_WIDTH), F32),
            *[pltpu.VMEM((nb, RG_WIDTH // LANE, SUBLANE * _rg_pitch(tt), LANE), F32) for _ in range(4)],
            pltpu.VMEM((nb, WINDOW, SWA_KVW), F32), pltpu.VMEM((nb, WINDOW, SWA_KVW), F32),
        ],
        compiler_params=_cparams("arbitrary"),
        name="mixer_prompt",
    )(sinks, x3d, gmix3, w_in_p, wg, bg3, gout3, cw, cb3, wa, ba3, wx, bx3, lam3, gq3, gk3, b256, b128, dup, wo,
      gmx3, wq, mgq3, mk, mv, wmo)


def _sample_rows_body(h_ref, hst_ref, cbuf_ref, cw_ref, cb_ref, wa_ref, ba_ref, wx_ref, bx_ref, lam_ref, gq_ref,
                      gk_ref, b256_ref, b128_ref, dup_ref,
                      org_ref, hn_ref, cnew_ref, qn_ref, kx_ref, vx_ref, knt_ref, vnt_ref):
    rx = h_ref[:, C_RX:C_RX + RG_WIDTH]
    cw = cw_ref[...]
    xc = cb_ref[...] + cbuf_ref[0] * cw[0:1, :]
    xc = xc + cbuf_ref[1] * cw[1:2, :]
    xc = xc + cbuf_ref[2] * cw[2:3, :]
    xc = xc + rx * cw[3:4, :]
    a, b = _rg_gates(xc, wa_ref, ba_ref[...], wx_ref, bx_ref[...], lam_ref[...])
    hn = a * hst_ref[...] + b
    hn_ref[...] = hn
    org_ref[...] = hn * _gelu_tanh(h_ref[:, C_RY:C_RY + RG_WIDTH])
    cnew_ref[0] = cbuf_ref[1]
    cnew_ref[1] = cbuf_ref[2]
    cnew_ref[2] = rx

    qn_ref[...] = _seg_rms(h_ref[:, C_SQ:C_SQ + SWA_Q], b256_ref[...], HEAD_DIM, gq_ref[...])
    kn = _seg_rms(h_ref[:, C_SK:C_SK + SWA_KVW], b128_ref[...], HEAD_DIM, gk_ref[...])
    v = h_ref[:, C_SV:C_SV + SWA_KVW]
    kx_ref[...] = jnp.dot(kn.astype(BF16), dup_ref[...], preferred_element_type=F32)
    vx_ref[...] = jnp.dot(v.astype(BF16), dup_ref[...], preferred_element_type=F32)
    knt = kn.T
    vnt = v.T
    nblk, _, bt = knt_ref.shape
    for i in range(nblk):
        knt_ref[i] = knt[:, i * bt:(i + 1) * bt]
        vnt_ref[i] = vnt[:, i * bt:(i + 1) * bt]


def _sample_rows(hs, hst, cbuf, cw, cb3, wa, ba3, wx, bx3, lam3, gq3, gk3, b256, b128, dup, layer, bt):
    n = hs.shape[0]
    lay3 = lambda i: (layer, 0, 0)
    full = lambda a: pl.BlockSpec(a.shape, lambda i: (0,) * a.ndim)
    vec = lambda w: pl.BlockSpec((None, 1, w), lay3)
    sq = lambda w: pl.BlockSpec((None, w, w), lay3)
    taps = pl.BlockSpec((None, CONV_W - 1, n, RG_WIDTH), lambda i: (layer, 0, 0, 0))
    shapes = [(n, RG_WIDTH), (n, RG_WIDTH), (CONV_W - 1, n, RG_WIDTH), (n, SWA_Q), (n, SWA_Q), (n, SWA_Q),
              (n // bt, SWA_KVW, bt), (n // bt, SWA_KVW, bt)]
    return pl.pallas_call(
        _sample_rows_body,
        grid=(1,),
        in_specs=[
            full(hs), pl.BlockSpec((None, n, RG_WIDTH), lay3), taps,
            pl.BlockSpec((None, CONV_W, RG_WIDTH), lay3), vec(RG_WIDTH),
            sq(RG_WIDTH), vec(RG_WIDTH), sq(RG_WIDTH), vec(RG_WIDTH), vec(RG_WIDTH),
            vec(SWA_Q), vec(SWA_KVW), full(b256), full(b128), full(dup),
        ],
        out_specs=[pl.BlockSpec(s, lambda i, nd=len(s): (0,) * nd) for s in shapes],
        out_shape=[jax.ShapeDtypeStruct(s, F32) for s in shapes],
        compiler_params=_cparams("arbitrary"),
        name="sample_rows",
    )(hs, hst, cbuf, cw, cb3, wa, ba3, wx, bx3, lam3, gq3, gk3, b256, b128, dup)


def _sample_gla_body(s0_ref, h_ref, wg_ref, bg_ref, goutc_ref, o_ref, sn_ref, qt_scr, kt_scr, et_scr, vt_scr, gt_scr):
    h = pl.program_id(0)
    qraw = h_ref[:, C_GQ:C_GQ + GLA_QKP]
    z = jnp.dot(qraw.astype(BF16), wg_ref[...], preferred_element_type=F32) + bg_ref[...]
    et_scr[...] = jnp.exp(_log_sigmoid(z) * (1.0 / GLA_TAU)).T
    qt_scr[...] = (qraw * (GLA_DK ** -0.5)).T
    kt_scr[...] = h_ref[:, C_GK:C_GK + GLA_QKP].T
    vt_scr[...] = h_ref[:, C_GV:C_GV + GLA_VP].T
    gt_scr[...] = h_ref[:, C_GOG:C_GOG + GLA_VP].T
    r0 = pl.multiple_of(h * GLA_DK, SUBLANE)
    v0 = pl.multiple_of(h * GLA_DVP, GLA_DVP)
    vt = vt_scr[pl.ds(v0, GLA_DV), :]
    o = jnp.zeros(vt.shape, F32)
    for k in range(GLA_DK):
        sn = et_scr[pl.ds(r0 + k, 1), :] * s0_ref[k] + kt_scr[pl.ds(r0 + k, 1), :] * vt
        sn_ref[k] = sn
        o = o + qt_scr[pl.ds(r0 + k, 1), :] * sn
    ms = jnp.sum(o * o, axis=0, keepdims=True) * (1.0 / GLA_DV)
    gate = gt_scr[pl.ds(v0, GLA_DV), :]
    y = ((o * lax.rsqrt(ms + EPS)) * goutc_ref[...]) * (gate * _sigmoid(gate))
    y = jnp.concatenate([y, jnp.zeros((GLA_DVP - GLA_DV, y.shape[1]), F32)], axis=0)
    o_ref[...] = y.T


def _sample_gla(s0t, hs, wg, bg3, goutc, layer):
    n = hs.shape[0]
    lay3 = lambda h: (layer, 0, 0)
    return pl.pallas_call(
        _sample_gla_body,
        grid=(GLA_HEADS,),
        in_specs=[
            pl.BlockSpec((None, None, GLA_DK, GLA_DV, n), lambda h: (layer, h, 0, 0, 0)),
            pl.BlockSpec((n, NP), lambda h: (0, 0)),
            pl.BlockSpec((None, GLA_QKP, GLA_QKP), lay3),
            pl.BlockSpec((None, 1, GLA_QKP), lay3),
            pl.BlockSpec((None, GLA_DV, 1), lay3),
        ],
        out_specs=[pl.BlockSpec((n, GLA_DVP), lambda h: (0, h)),
                   pl.BlockSpec((None, GLA_DK, GLA_DV, n), lambda h: (h, 0, 0, 0))],
        out_shape=[jax.ShapeDtypeStruct((n, GLA_VP), F32),
                   jax.ShapeDtypeStruct((GLA_HEADS, GLA_DK, GLA_DV, n), F32)],
        scratch_shapes=[pltpu.VMEM((GLA_QKP, n), F32)] * 3 + [pltpu.VMEM((GLA_VP, n), F32)] * 2,
        compiler_params=_cparams("arbitrary"),
        name="sample_gla",
    )(s0t, hs, wg, bg3, goutc)


def _sample_swa_body(sink_ref, kct_ref, vct_ref, qn_ref, kx_ref, vx_ref, knt_ref, vnt_ref, dupt_ref, o_ref, kot_ref,
                     vot_ref, *, layer):
    bt = kct_ref.shape[0]
    dupt = dupt_ref[...]
    hi = lax.broadcasted_iota(jnp.int32, (SWA_HEADS, SWA_Q), 0)
    li = lax.broadcasted_iota(jnp.int32, (SWA_HEADS, SWA_Q), 1)
    hm4 = ((li >= hi * HEAD_DIM) & (li < (hi + 1) * HEAD_DIM)).astype(F32)
    hrow = lax.broadcasted_iota(jnp.int32, (SWA_HEADS, 1), 0)
    slope = jnp.zeros((SWA_HEADS, 1), F32)
    sink = jnp.zeros((SWA_HEADS, 1), F32)
    for h in range(SWA_HEADS):
        slope = jnp.where(hrow == h, _alibi_slope(h), slope)
        sink = jnp.where(hrow == h, sink_ref[layer, h], sink)
    dist = (WINDOW - lax.broadcasted_iota(jnp.int32, (1, WINDOW), 1)).astype(F32)
    last = lax.broadcasted_iota(jnp.int32, (SWA_KVW, WINDOW), 1) == WINDOW - 1
    nt = (((1,), (1,)), ((), ()))
    for j in range(bt):
        kt = kct_ref[j]
        vt = vct_ref[j]
        kexp = jnp.dot(dupt, kt.astype(BF16), preferred_element_type=F32).astype(BF16)
        vexp = jnp.dot(dupt, vt.astype(BF16), preferred_element_type=F32).astype(BF16)
        q4 = (qn_ref[j:j + 1, :] * hm4).astype(BF16)
        s = jnp.dot(q4, kexp, preferred_element_type=F32)
        s = s * (HEAD_DIM ** -0.5) - slope * dist
        s_new = jnp.sum(q4.astype(F32) * kx_ref[j:j + 1, :], axis=-1, keepdims=True) * (HEAD_DIM ** -0.5)
        m = jnp.maximum(jnp.maximum(jnp.max(s, axis=-1, keepdims=True), s_new), sink)
        e = jnp.exp(s - m)
        e_new = jnp.exp(s_new - m)
        den = jnp.sum(e, axis=-1, keepdims=True) + e_new + jnp.exp(sink - m)
        o4 = lax.dot_general((e / den).astype(BF16), vexp, nt, preferred_element_type=F32)
        o4 = o4 + (e_new / den).astype(BF16).astype(F32) * vx_ref[j:j + 1, :]
        o_ref[j:j + 1, :] = jnp.sum(o4 * hm4, axis=0, keepdims=True)
        kot_ref[j] = jnp.where(last, knt_ref[:, j:j + 1], pltpu.roll(kt, WINDOW - 1, 1))
        vot_ref[j] = jnp.where(last, vnt_ref[:, j:j + 1], pltpu.roll(vt, WINDOW - 1, 1))


def _sample_swa(sinks, kct, vct, qn, kx, vx, knt, vnt, dupt, layer, bt):
    n = qn.shape[0]
    row = lambda w: pl.BlockSpec((bt, w), lambda i: (i, 0))
    cache = pl.BlockSpec((None, bt, SWA_KVW, WINDOW), lambda i: (layer, i, 0, 0))
    cache_o = pl.BlockSpec((bt, SWA_KVW, WINDOW), lambda i: (i, 0, 0))
    col = pl.BlockSpec((None, SWA_KVW, bt), lambda i: (i, 0, 0))
    return pl.pallas_call(
        functools.partial(_sample_swa_body, layer=layer),
        grid=(n // bt,),
        in_specs=[
            pl.BlockSpec(memory_space=pltpu.SMEM), cache, cache,
            row(SWA_Q), row(SWA_Q), row(SWA_Q), col, col,
            pl.BlockSpec(dupt.shape, lambda i: (0, 0)),
        ],
        out_specs=[row(SWA_Q), cache_o, cache_o],
        out_shape=[
            jax.ShapeDtypeStruct((n, SWA_Q), F32),
            jax.ShapeDtypeStruct((n, SWA_KVW, WINDOW), F32),
            jax.ShapeDtypeStruct((n, SWA_KVW, WINDOW), F32),
        ],
        compiler_params=_cparams("parallel"),
        name="sample_swa",
    )(sinks, kct, vct, qn, kx, vx, knt, vnt, dupt)


def _sample_post1_body(x_ref, og_ref, or_ref, os_ref, wo_ref, gx_ref, wq_ref, gq_ref, bones_ref, x1_ref, qn_ref):
    x1 = _mix_out(x_ref[...], og_ref[...], or_ref[...], os_ref[...], wo_ref)
    x1_ref[...] = x1
    qn_ref[...] = _mem_q(x1, gx_ref[...], wq_ref, gq_ref[...], bones_ref[...])


def _sample_post1(x2d, og, orr, osw, wo, gx3, wq, gq3, bones256, layer):
    n = x2d.shape[0]
    lay3 = lambda i: (layer, 0, 0)
    full = lambda a: pl.BlockSpec(a.shape, lambda i: (0,) * a.ndim)
    return pl.pallas_call(
        _sample_post1_body,
        grid=(1,),
        in_specs=[
            full(x2d), full(og), full(orr), full(osw),
            pl.BlockSpec((None, MIX_W, D_MODEL), lay3),
            pl.BlockSpec((None, 1, D_MODEL), lay3),
            pl.BlockSpec((None, D_MODEL, MEM_W), lay3),
            pl.BlockSpec((None, 1, MEM_W), lay3),
            full(bones256),
        ],
        out_specs=[pl.BlockSpec((n, D_MODEL), lambda i: (0, 0)), pl.BlockSpec((n, MEM_W), lambda i: (0, 0))],
        out_shape=[jax.ShapeDtypeStruct((n, D_MODEL), F32), jax.ShapeDtypeStruct((n, MEM_W), F32)],
        compiler_params=_cparams("arbitrary"),
        name="sample_post1",
    )(x2d, og, orr, osw, wo, gx3, wq, gq3, bones256)


def _sample_mem_body(qn_ref, mk_ref, mv_ref, o_ref):
    bt = mk_ref.shape[0]
    hi = lax.broadcasted_iota(jnp.int32, (MEM_HEADS, MEM_W), 0)
    li = lax.broadcasted_iota(jnp.int32, (MEM_HEADS, MEM_W), 1)
    hm4 = ((li >= hi * MEM_HD) & (li < (hi + 1) * MEM_HD)).astype(F32)
    for j in range(bt):
        q4 = (qn_ref[j:j + 1, :] * hm4).astype(BF16)
        s = jnp.dot(q4, mk_ref[j].astype(BF16), preferred_element_type=F32)
        p = _softmax_rows(s * (MEM_HD ** -0.5))
        o4 = lax.dot_general(p.astype(BF16), mv_ref[j].astype(BF16), (((1,), (1,)), ((), ())),
                             preferred_element_type=F32)
        o_ref[j:j + 1, :] = jnp.sum(o4 * hm4, axis=0, keepdims=True)


def _sample_mem(qn, mk, mv, layer, bt):
    n = qn.shape[0]
    n_mem = mk.shape[3]
    cache = pl.BlockSpec((None, bt, MEM_W, n_mem), lambda i: (layer, i, 0, 0))
    return pl.pallas_call(
        _sample_mem_body,
        grid=(n // bt,),
        in_specs=[pl.BlockSpec((bt, MEM_W), lambda i: (i, 0)), cache, cache],
        out_specs=pl.BlockSpec((bt, MEM_W), lambda i: (i, 0)),
        out_shape=jax.ShapeDtypeStruct((n, MEM_W), F32),
        compiler_params=_cparams("parallel"),
        name="sample_mem",
    )(qn, mk, mv)


def _ffn_res_body(x_ref, o_ref_in, wmo_ref, g_ref, w1_ref, w2_ref, out_ref):
    x = x_ref[...] + jnp.dot(o_ref_in[...].astype(BF16), wmo_ref[...], preferred_element_type=F32)
    xn = _rms(x, g_ref[...]).astype(BF16)
    acc = x
    for c in range(D_FF // FF_CHUNK):
        h = jnp.dot(xn, w1_ref[:, c * FF_CHUNK:(c + 1) * FF_CHUNK], preferred_element_type=F32)
        h = jnp.square(jnp.maximum(h, 0.0))
        acc = acc + jnp.dot(h.astype(BF16), w2_ref[c * FF_CHUNK:(c + 1) * FF_CHUNK, :], preferred_element_type=F32)
    out_ref[...] = acc


def _sample_ffn(x1, o, wmo, g3, w1, w2, layer):
    n = x1.shape[0]
    lay3 = lambda i: (layer, 0, 0)
    const = dict(pipeline_mode=pl.Buffered(1))
    return pl.pallas_call(
        _ffn_res_body,
        grid=(1,),
        in_specs=[
            pl.BlockSpec((n, D_MODEL), lambda i: (0, 0)),
            pl.BlockSpec((n, MEM_W), lambda i: (0, 0)),
            pl.BlockSpec((None, MEM_W, D_MODEL), lay3),
            pl.BlockSpec((None, 1, D_MODEL), lay3),
            pl.BlockSpec((None, D_MODEL, D_FF), lay3, **const),
            pl.BlockSpec((None, D_FF, D_MODEL), lay3, **const),
        ],
        out_specs=pl.BlockSpec((n, D_MODEL), lambda i: (0, 0)),
        out_shape=jax.ShapeDtypeStruct((n, D_MODEL), F32),
        compiler_params=_cparams("arbitrary"),
        name="sample_ffn",
    )(x1, o, wmo, g3, w1, w2)


def _pad_heads(a, axis, heads, width, padded):
    shp = a.shape
    a = a.reshape(shp[:axis] + (heads, width) + shp[axis + 1:])
    pad = [(0, 0)] * a.ndim
    pad[axis + 1] = (0, padded - width)
    a = jnp.pad(a, pad)
    return a.reshape(shp[:axis] + (heads * padded,) + shp[axis + 1:])


def _prep_w_in(w_in):
    offs = [0]
    for w in (GLA_QK, GLA_QK, GLA_V, GLA_RANK, GLA_V, RG_WIDTH, RG_WIDTH, SWA_Q, SWA_KVW, SWA_KVW):
        offs.append(offs[-1] + w)
    gq, gk, gv, glr, gog, rx, ry, sq, sk, sv = [w_in[..., offs[i]:offs[i + 1]] for i in range(10)]
    zeros = lambda n: jnp.zeros(w_in.shape[:-1] + (n,), w_in.dtype)
    cols = [rx, ry,
            gq, glr, zeros(GLA_QKP - GLA_QK - GLA_RANK),
            gk, zeros(GLA_QKP - GLA_QK),
            sq,
            _pad_heads(gv, 2, GLA_HEADS, GLA_DV, GLA_DVP),
            _pad_heads(gog, 2, GLA_HEADS, GLA_DV, GLA_DVP),
            sk, sv]
    out = jnp.concatenate(cols, axis=-1)
    assert out.shape[-1] == NP
    return out.astype(BF16)


def _block_diag(w):
    depth, nb, bw, _ = w.shape
    eye = jnp.eye(nb, dtype=w.dtype)
    return jnp.einsum("lncd,nm->lncmd", w, eye).reshape(depth, nb * bw, nb * bw)


def _block_ones(n, seg):
    i = jnp.arange(n) // seg
    return (i[:, None] == i[None, :]).astype(BF16)


def _dup_matrix():
    src = jnp.arange(SWA_Q)
    src = (src // HEAD_DIM // SWA_G) * HEAD_DIM + src % HEAD_DIM
    return (jnp.arange(SWA_KVW)[:, None] == src[None, :]).astype(BF16)


def kernel(x_prompt, x_sample, mem_prompt, state_gla, state_rg_h, state_rg_conv, cache_swa_k, cache_swa_v, cache_mem_k, cache_mem_v, g_mix, w_in, gla_w_gate2, gla_b_gate, gla_g_out, rg_conv_w, rg_conv_b, rg_w_a, rg_b_a, rg_w_x, rg_b_x, rg_lam, swa_g_q, swa_g_k, swa_sinks, w_out, g_mem_x, g_mem_m, mem_w_q, mem_w_kv, mem_g_q, mem_g_k, mem_w_o, g_ffn, ffn_w1, ffn_w2):
    depth = w_in.shape[0]
    nb, t_len, _ = x_prompt.shape
    ns = x_sample.shape[0]
    n_mem = mem_prompt.shape[1]
    assert t_len % ROW_T == 0 and t_len % MIX_T == 0 and x_sample.shape[1] == 1 and ns % 8 == 0
    row3 = lambda a: a.reshape(depth, 1, a.shape[-1])

    w_in_p = _prep_w_in(w_in)
    wg = jnp.zeros((depth, GLA_QKP, GLA_QKP), F32).at[:, GLR_LANE:GLR_LANE + GLA_RANK, :GLA_QK].set(gla_w_gate2)
    wg = wg.astype(BF16)
    bg3 = row3(jnp.pad(gla_b_gate, ((0, 0), (0, GLA_QKP - GLA_QK))))
    gout3 = row3(jnp.pad(gla_g_out, ((0, 0), (0, GLA_DVP - GLA_DV))))
    wa = _block_diag(rg_w_a).astype(BF16)
    wx = _block_diag(rg_w_x).astype(BF16)
    ba3, bx3, lam3, cb3 = row3(rg_b_a), row3(rg_b_x), row3(rg_lam), row3(rg_conv_b)
    gq3 = row3(jnp.tile(swa_g_q, (1, SWA_HEADS)))
    gk3 = row3(jnp.tile(swa_g_k, (1, SWA_KV)))
    wo = jnp.concatenate([_pad_heads(w_out[:, :GLA_V], 1, GLA_HEADS, GLA_DV, GLA_DVP), w_out[:, GLA_V:]], axis=1)
    wo = wo.astype(BF16)
    gmx3, gmm3, gffn3, gmix3 = row3(g_mem_x), row3(g_mem_m), row3(g_ffn), row3(g_mix)
    mgq3 = row3(jnp.tile(mem_g_q, (1, MEM_HEADS)))
    mgk3 = row3(jnp.tile(mem_g_k, (1, MEM_HEADS)))
    wq = mem_w_q.astype(BF16)
    wkv = mem_w_kv.astype(BF16)
    wmo = mem_w_o.astype(BF16)
    w1 = ffn_w1.astype(BF16)
    w2 = ffn_w2.astype(BF16)
    b256 = _block_ones(SWA_Q, HEAD_DIM)
    b128 = _block_ones(SWA_KVW, HEAD_DIM)
    dup = _dup_matrix()

    mk_all, mv_all = _mem_kv(mem_prompt.reshape(nb * n_mem, D_MODEL), gmm3, wkv, mgk3, b256)
    mk_all = mk_all.reshape(depth, nb, n_mem, MEM_W)
    mv_all = mv_all.reshape(depth, nb, n_mem, MEM_W)

    xp = x_prompt.reshape(nb * t_len, D_MODEL)
    xs = x_sample.reshape(ns, D_MODEL)
    rg_conv_t = state_rg_conv.transpose(0, 2, 1, 3)
    gla_t = state_gla.transpose(0, 2, 3, 4, 1)
    swk = cache_swa_k.transpose(0, 1, 3, 4, 2).reshape(depth, ns, SWA_KVW, WINDOW)
    swv = cache_swa_v.transpose(0, 1, 3, 4, 2).reshape(depth, ns, SWA_KVW, WINDOW)
    cmk = cache_mem_k.transpose(0, 1, 3, 4, 2).reshape(depth, ns, MEM_W, n_mem)
    cmv = cache_mem_v.transpose(0, 1, 3, 4, 2).reshape(depth, ns, MEM_W, n_mem)
    goutc = gla_g_out.reshape(depth, GLA_DV, 1)
    dupt = dup.T
    sbt = min(ns, 32)

    gla_p, gla_s, rgh_p, rgh_s, rgc_p, rgc_s = [], [], [], [], [], []
    swk_p, swk_s, swv_p, swv_s = [], [], [], []
    ts = min(ns, 128)
    for l in range(depth):
        xp, st, hl, cbuf, kb, vb = _mixer_prompt(
            swa_sinks, xp.reshape(nb, t_len, D_MODEL), gmix3, w_in_p, wg, bg3, gout3, rg_conv_w, cb3, wa, ba3, wx,
            bx3, lam3, gq3, gk3, b256, b128, dup, wo, gmx3, wq, mgq3, mk_all, mv_all, wmo, l)
        xp = _ffn(xp.reshape(nb * t_len, D_MODEL), gffn3, w1, w2, l, ROW_T)
        st = st.reshape(nb, GLA_HEADS, GLA_DVP, GLA_QKP)
        gla_p.append(jnp.stack(
            [st[:, h, :GLA_DV, h * GLA_DK:(h + 1) * GLA_DK].transpose(0, 2, 1) for h in range(GLA_HEADS)], axis=1))
        rgh_p.append(hl.reshape(nb, RG_WIDTH))
        rgc_p.append(cbuf)
        swk_p.append(kb.reshape(nb, WINDOW, SWA_KV, HEAD_DIM))
        swv_p.append(vb.reshape(nb, WINDOW, SWA_KV, HEAD_DIM))

        hs = _in_proj(xs, gmix3, w_in_p, l, ts)
        org_s, hn, cnew, qn, kx, vx, knt, vnt = _sample_rows(
            hs, state_rg_h, rg_conv_t, rg_conv_w, cb3, wa, ba3, wx, bx3, lam3, gq3, gk3, b256, b128, dup, l, sbt)
        og_s, sn = _sample_gla(gla_t, hs, wg, bg3, goutc, l)
        osw_s, ko, vo = _sample_swa(swa_sinks, swk, swv, qn, kx, vx, knt, vnt, dupt, l, sbt)
        x1, qm = _sample_post1(xs, og_s, org_s, osw_s, wo, gmx3, wq, mgq3, b256, l)
        om = _sample_mem(qm, cmk, cmv, l, 8)
        xs = _sample_ffn(x1, om, wmo, gffn3, w1, w2, l)
        gla_s.append(sn)
        rgh_s.append(hn)
        rgc_s.append(cnew)
        swk_s.append(ko.reshape(ns, SWA_KV, HEAD_DIM, WINDOW))
        swv_s.append(vo.reshape(ns, SWA_KV, HEAD_DIM, WINDOW))

    return (xp.reshape(nb, t_len, D_MODEL), xs.reshape(ns, 1, D_MODEL),
            jnp.stack(gla_p), jnp.stack(gla_s).transpose(0, 4, 1, 2, 3), jnp.stack(rgh_p), jnp.stack(rgh_s),
            jnp.stack(rgc_p), jnp.stack(rgc_s).transpose(0, 2, 1, 3), jnp.stack(swk_p),
            jnp.stack(swk_s).transpose(0, 1, 4, 2, 3),
            jnp.stack(swv_p), jnp.stack(swv_s).transpose(0, 1, 4, 2, 3),
            mk_all.reshape(depth, nb, n_mem, MEM_HEADS, MEM_HD), mv_all.reshape(depth, nb, n_mem, MEM_HEADS, MEM_HD))
```

```python
import functools
import math

import jax
import jax.numpy as jnp
from jax import lax
from jax.experimental import pallas as pl
from jax.experimental.pallas import tpu as pltpu

F32 = jnp.float32
BF16 = jnp.bfloat16

D_MODEL = 1024
EPS = 1e-6
GLA_HEADS, GLA_DK, GLA_DV, GLA_RANK, GLA_TAU = 4, 48, 96, 16, 16.0
GLA_QK = GLA_HEADS * GLA_DK
GLA_V = GLA_HEADS * GLA_DV
RG_WIDTH, RG_BLOCKS, RG_C, CONV_W = 384, 4, 8.0, 4
RG_BW = RG_WIDTH // RG_BLOCKS
SWA_HEADS, SWA_KV, HEAD_DIM, WINDOW = 4, 2, 64, 128
SWA_G = SWA_HEADS // SWA_KV
SWA_Q = SWA_HEADS * HEAD_DIM
SWA_KVW = SWA_KV * HEAD_DIM
MEM_HEADS, MEM_HD = 4, 64
MEM_W = MEM_HEADS * MEM_HD
D_FF = 4 * D_MODEL

LANE = 128
SUBLANE = 8
GLA_DVP = LANE
GLA_VP = GLA_HEADS * GLA_DVP
GLA_QKP = 2 * LANE
MIX_W = GLA_VP + RG_WIDTH + SWA_Q

C_RX, C_RY, C_GQ, C_GK, C_SQ, C_GV, C_GOG, C_SK, C_SV = 0, 384, 768, 1024, 1280, 1536, 2048, 2560, 2688
NP = 2816
GLR_LANE = GLA_QK

VMEM_LIMIT = 56 * 1024 * 1024
NEG = -1e30

MIX_T = 256
ROW_T = 512
FF_CHUNK = 1024


def _cparams(*sem):
    return pltpu.CompilerParams(dimension_semantics=sem, vmem_limit_bytes=VMEM_LIMIT)


def _rms(x, g):
    ms = jnp.mean(x * x, axis=-1, keepdims=True)
    return (x * lax.rsqrt(ms + EPS)) * g


def _dot(a, b):
    return jnp.dot(a.astype(BF16), b.astype(BF16), preferred_element_type=F32)


def _dot_nt(a, b):
    return lax.dot_general(a.astype(BF16), b.astype(BF16), (((1,), (1,)), ((), ())), preferred_element_type=F32)


def _split2_dot(x, m):
    hi = x.astype(BF16)
    lo = (x - hi.astype(F32)).astype(BF16)
    return jnp.dot(hi, m, preferred_element_type=F32) + jnp.dot(lo, m, preferred_element_type=F32)


def _split3(x):
    x1 = x.astype(BF16)
    r1 = x - x1.astype(F32)
    x2 = r1.astype(BF16)
    x3 = (r1 - x2.astype(F32)).astype(BF16)
    return x1, x2, x3


def _split3_dot_left(m, x):
    d = functools.partial(jnp.dot, preferred_element_type=F32)
    x1, x2, x3 = _split3(x)
    return d(m, x1) + d(m, x2) + d(m, x3)


def _seg_rms(x, bones, seg, g):
    ms = _split2_dot(x * x, bones) * (1.0 / seg)
    return (x * lax.rsqrt(ms + EPS)) * g


def _sigmoid(x):
    return 1.0 / (1.0 + jnp.exp(-x))


def _softplus(x):
    return jnp.maximum(x, 0.0) + jnp.log1p(jnp.exp(-jnp.abs(x)))


def _log_sigmoid(z):
    return -_softplus(-z)


def _gelu_tanh(x):
    return x * (0.5 * (1.0 + jnp.tanh(math.sqrt(2.0 / math.pi) * (x + 0.044715 * (x * x * x)))))


def _lane_head_mask(width, seg, h, dtype=F32):
    lane = lax.broadcasted_iota(jnp.int32, (1, width), 1)
    return ((lane >= seg * h) & (lane < seg * (h + 1))).astype(dtype)


def _masked_softmax_rows(s, sink):
    m = jnp.maximum(jnp.max(s, axis=-1, keepdims=True), sink)
    e = jnp.exp(s - m)
    den = jnp.sum(e, axis=-1, keepdims=True) + jnp.exp(sink - m)
    return e / den


def _softmax_rows(s):
    m = jnp.max(s, axis=-1, keepdims=True)
    e = jnp.exp(s - m)
    return e / jnp.sum(e, axis=-1, keepdims=True)


def _alibi_slope(h):
    return 2.0 ** (-8.0 * (h + 1) / SWA_HEADS)


def _rows(x, starts, size):
    parts = [x[s:s + size] for s in starts]
    return parts[0] if len(parts) == 1 else jnp.concatenate(parts, axis=0)


def _in_proj_body(x_ref, g_ref, w_ref, o_ref):
    xn = _rms(x_ref[...], g_ref[...])
    o_ref[...] = jnp.dot(xn.astype(BF16), w_ref[...], preferred_element_type=F32)


def _in_proj(x2d, g3, w3, layer, tm):
    m = x2d.shape[0]
    return pl.pallas_call(
        _in_proj_body,
        grid=(m // tm,),
        in_specs=[
            pl.BlockSpec((tm, D_MODEL), lambda i: (i, 0)),
            pl.BlockSpec((None, 1, D_MODEL), lambda i: (layer, 0, 0)),
            pl.BlockSpec((None, D_MODEL, NP), lambda i: (layer, 0, 0)),
        ],
        out_specs=pl.BlockSpec((tm, NP), lambda i: (i, 0)),
        out_shape=jax.ShapeDtypeStruct((m, NP), F32),
        compiler_params=_cparams("parallel"),
        name="in_proj",
    )(x2d, g3, w3)


def _ffn_body(x_ref, g_ref, w1_ref, w2_ref, o_ref):
    x = x_ref[...]
    xn = _rms(x, g_ref[...]).astype(BF16)
    acc = x
    for c in range(D_FF // FF_CHUNK):
        h = jnp.dot(xn, w1_ref[:, c * FF_CHUNK:(c + 1) * FF_CHUNK], preferred_element_type=F32)
        h = jnp.square(jnp.maximum(h, 0.0))
        acc = acc + jnp.dot(h.astype(BF16), w2_ref[c * FF_CHUNK:(c + 1) * FF_CHUNK, :], preferred_element_type=F32)
    o_ref[...] = acc


def _ffn(x2d, g3, w1, w2, layer, tm):
    m = x2d.shape[0]
    const = dict(pipeline_mode=pl.Buffered(1))
    return pl.pallas_call(
        _ffn_body,
        grid=(m // tm,),
        in_specs=[
            pl.BlockSpec((tm, D_MODEL), lambda i: (i, 0)),
            pl.BlockSpec((None, 1, D_MODEL), lambda i: (layer, 0, 0)),
            pl.BlockSpec((None, D_MODEL, D_FF), lambda i: (layer, 0, 0), **const),
            pl.BlockSpec((None, D_FF, D_MODEL), lambda i: (layer, 0, 0), **const),
        ],
        out_specs=pl.BlockSpec((tm, D_MODEL), lambda i: (i, 0)),
        out_shape=jax.ShapeDtypeStruct((m, D_MODEL), F32),
        compiler_params=_cparams("parallel"),
        name="ffn",
    )(x2d, g3, w1, w2)


def _mem_kv_body(mem_ref, g_ref, w_ref, gk_ref, bones_ref, k_ref, v_ref, vt_ref):
    kv = jnp.dot(_rms(mem_ref[...], g_ref[...]).astype(BF16), w_ref[...], preferred_element_type=F32)
    k_ref[...] = _seg_rms(kv[:, :MEM_W], bones_ref[...], MEM_HD, gk_ref[...])
    v = kv[:, MEM_W:]
    v_ref[...] = v
    nb, _, n_mem = vt_ref.shape
    for b in range(nb):
        vt_ref[b] = v[b * n_mem:(b + 1) * n_mem].T


def _mem_kv(mem2d, g3, wkv, gk3, bones256, nb):
    depth = wkv.shape[0]
    m = mem2d.shape[0]
    return pl.pallas_call(
        _mem_kv_body,
        grid=(depth,),
        in_specs=[
            pl.BlockSpec((m, D_MODEL), lambda l: (0, 0)),
            pl.BlockSpec((None, 1, D_MODEL), lambda l: (l, 0, 0)),
            pl.BlockSpec((None, D_MODEL, 2 * MEM_W), lambda l: (l, 0, 0)),
            pl.BlockSpec((None, 1, MEM_W), lambda l: (l, 0, 0)),
            pl.BlockSpec((MEM_W, MEM_W), lambda l: (0, 0)),
        ],
        out_specs=[pl.BlockSpec((None, m, MEM_W), lambda l: (l, 0, 0))] * 2
        + [pl.BlockSpec((None, nb, MEM_W, m // nb), lambda l: (l, 0, 0, 0))],
        out_shape=[jax.ShapeDtypeStruct((depth, m, MEM_W), F32)] * 2
        + [jax.ShapeDtypeStruct((depth, nb, MEM_W, m // nb), F32)],
        compiler_params=_cparams("parallel"),
        name="mem_kv",
    )(mem2d, g3, wkv, gk3, bones256)


def _mix_out(x, og, orr, osw, wo_ref):
    y = _dot(og, wo_ref[0:GLA_VP, :])
    y = y + _dot(orr, wo_ref[GLA_VP:GLA_VP + RG_WIDTH, :])
    y = y + _dot(osw, wo_ref[GLA_VP + RG_WIDTH:MIX_W, :])
    return x + y


def _mem_q(x1, gx, wq_ref, gq, bones):
    q = jnp.dot(_rms(x1, gx).astype(BF16), wq_ref[...], preferred_element_type=F32)
    return _seg_rms(q, bones, MEM_HD, gq)


def _gla_tile(h_scr, wg_ref, bg_ref, gout_ref, st_scr, cum_scr, d_scr, oc_scr):
    tt = h_scr.shape[0]
    hf = tt // 2
    qraw = h_scr[:, C_GQ:C_GQ + GLA_QKP]
    k = h_scr[:, C_GK:C_GK + GLA_QKP]
    z = jnp.dot(qraw.astype(BF16), wg_ref[...], preferred_element_type=F32) + bg_ref[...]
    g = _log_sigmoid(z) * (1.0 / GLA_TAU)
    q = qraw * (GLA_DK ** -0.5)

    row = lax.broadcasted_iota(jnp.int32, (tt, tt), 0)
    col = lax.broadcasted_iota(jnp.int32, (tt, tt), 1)
    cum = _split3_dot_left((row >= col).astype(BF16), g)
    cum_scr[...] = cum
    rowi = lax.broadcasted_iota(jnp.int32, (tt, 1), 0)

    def boundary(b, n):
        return jnp.concatenate(
            [jnp.broadcast_to(cum_scr[pl.ds(gi * 2 * b + b - 1, 1), :], (n, GLA_QKP)) for gi in range(tt // (2 * b))],
            axis=0)

    rd1 = pltpu.roll(cum, 1, 0)
    rd2 = pltpu.roll(cum, 2, 0)
    ru1 = pltpu.roll(cum, tt - 1, 0)
    m4 = rowi & 3
    bounds = {
        1: jnp.where((rowi & 1) == 0, cum, rd1),
        2: jnp.where(m4 == 0, ru1, jnp.where(m4 == 1, cum, jnp.where(m4 == 2, rd1, rd2))),
        4: boundary(4, 2 * 4),
    }
    low = [(q.astype(BF16), k.astype(BF16), 0)]
    for b in (1, 2, 4):
        second = (rowi & (2 * b - 1)) >= b
        c = bounds[b]
        qs = q * jnp.exp(jnp.where(second, cum - c, NEG))
        ks = k * jnp.exp(jnp.where(second, NEG, c - cum))
        low.append((qs.astype(BF16), ks.astype(BF16), int(math.log2(2 * b))))

    mid = []
    b = SUBLANE
    while b <= hf:
        ng = tt // (2 * b)
        c = boundary(b, b)
        firsts = [gi * 2 * b for gi in range(ng)]
        seconds = [gi * 2 * b + b for gi in range(ng)]
        qs2 = _rows(q, seconds, b) * jnp.exp(_rows(cum, seconds, b) - c)
        ks1 = _rows(k, firsts, b) * jnp.exp(c - _rows(cum, firsts, b))
        if b < hf:
            zb = jnp.zeros((b, GLA_QKP), F32)
            ks1 = jnp.concatenate([piece for gi in range(ng) for piece in (ks1[gi * b:(gi + 1) * b], zb)], axis=0)
        mid.append((b, qs2.astype(BF16), ks1.astype(BF16)))
        b *= 2

    st = st_scr[...]
    o_inter = _dot_nt(q * jnp.exp(cum), st)
    last = cum_scr[pl.ds(tt - 1, 1), :]
    kd = k * jnp.exp(last - cum)
    v_all = h_scr[:, C_GV:C_GV + GLA_VP]
    upd = jnp.dot(v_all.T.astype(BF16), kd.astype(BF16), preferred_element_type=F32)
    vrow = lax.broadcasted_iota(jnp.int32, (GLA_VP, 1), 0) // GLA_DVP
    lane = lax.broadcasted_iota(jnp.int32, (1, GLA_QKP), 1)
    khead = ((lane >= GLA_DK).astype(jnp.int32) + (lane >= 2 * GLA_DK).astype(jnp.int32)
             + (lane >= 3 * GLA_DK).astype(jnp.int32) + (lane >= 4 * GLA_DK).astype(jnp.int32))
    st_new = jnp.where(vrow == khead, st * jnp.exp(last) + upd, 0.0)
    st_scr[...] = st_new

    nh = GLA_HEADS
    hq = hf // 2
    c128 = lax.broadcasted_iota(jnp.int32, (1, hf), 1)
    r_low = lax.broadcasted_iota(jnp.int32, (nh * hf, 1), 0) & (hf - 1)
    r_mid = lax.broadcasted_iota(jnp.int32, (nh * hq, 1), 0) & (hq - 1)
    low_masks = [(r_low >> s) == (c128 >> s) for (_, _, s) in low]
    mid_masks = {bb: (r_mid >> int(math.log2(bb))) == (c128 >> int(math.log2(2 * bb)))
                 for (bb, _, _) in mid if 2 * bb < hf}
    nt = (((1,), (1,)), ((), ()))
    hms = [_lane_head_mask(GLA_QKP, GLA_DK, h, BF16) for h in range(nh)]
    stack_heads = lambda xb: jnp.concatenate([xb * hm for hm in hms], axis=0)
    for half in range(2):
        rs = slice(half * hf, (half + 1) * hf)
        acc = None
        for (qb, kb, _), mk in zip(low, low_masks):
            p = lax.dot_general(stack_heads(qb[rs]), kb[rs], nt, preferred_element_type=F32)
            p = jnp.where(mk, p, 0.0)
            acc = p if acc is None else acc + p
        for h in range(nh):
            d_scr[h, half] = acc[h * hf:(h + 1) * hf]
        for bb, qb2, kb1 in mid:
            if bb == hf:
                continue
            cs = slice(half * hq, (half + 1) * hq)
            p = lax.dot_general(stack_heads(qb2[cs]), kb1[rs], nt, preferred_element_type=F32)
            if bb in mid_masks:
                p = jnp.where(mid_masks[bb], p, 0.0)
            for h in range(nh):
                for gi in range(hf // (2 * bb)):
                    dst = pl.ds(gi * 2 * bb + bb, bb)
                    src = slice(h * hq + gi * bb, h * hq + (gi + 1) * bb)
                    d_scr[h, half, dst, :] = d_scr[h, half, dst, :] + p[src, :]
    _, qb2, kb1 = mid[-1]
    off_all = lax.dot_general(stack_heads(qb2), kb1, nt, preferred_element_type=F32)
    gout = gout_ref[...]
    for h in range(nh):
        off = off_all[h * hf:(h + 1) * hf]
        sl = slice(h * GLA_DVP, (h + 1) * GLA_DVP)
        v_h = v_all[:, sl].astype(BF16)
        o0 = jnp.dot(d_scr[h, 0].astype(BF16), v_h[0:hf], preferred_element_type=F32)
        a1 = jnp.concatenate([off, d_scr[h, 1]], axis=1).astype(BF16)
        o1 = jnp.dot(a1, v_h, preferred_element_type=F32)
        o = jnp.concatenate([o0, o1], axis=0) + o_inter[:, sl]
        ms = jnp.sum(o * o, axis=-1, keepdims=True) * (1.0 / GLA_DV)
        gate = h_scr[:, C_GOG + h * GLA_DVP:C_GOG + (h + 1) * GLA_DVP]
        oc_scr[:, sl] = ((o * lax.rsqrt(ms + EPS)) * gout) * (gate * _sigmoid(gate))
    return st_new


def _rg_gates(xc, wa_ref, ba, wx_ref, bx, lam):
    r = _sigmoid(jnp.dot(xc.astype(BF16), wa_ref[...], preferred_element_type=F32) + ba)
    i = _sigmoid(jnp.dot(xc.astype(BF16), wx_ref[...], preferred_element_type=F32) + bx)
    log_a = (-RG_C * r) * _softplus(-lam)
    a = jnp.exp(log_a)
    b = jnp.sqrt(1.0 - jnp.exp(2.0 * log_a)) * (i * xc)
    return a, b


def _rg_pitch(tt):
    return tt // SUBLANE + SUBLANE


def _rg_tiles(h_scr, cw_ref, cb_ref, wa_ref, ba_ref, wx_ref, bx_ref, lam_ref, hc_scr, cc_scr, ra_scr, rb_scr, rh_scr,
              rp_scr, oc_scr):
    nb, tt = len(h_scr), h_scr[0].shape[0]
    rowi = lax.broadcasted_iota(jnp.int32, (tt, 1), 0)
    nl = RG_WIDTH // LANE
    seg = tt // SUBLANE
    pitch = _rg_pitch(tt)
    cw = cw_ref[...]
    for bi in range(nb):
        rx = h_scr[bi][:,C_RX:C_RX + RG_WIDTH]
        p0, p1, p2 = cc_scr[bi, 0:1, :], cc_scr[bi, 1:2, :], cc_scr[bi, 2:3, :]
        x1 = jnp.where(rowi == 0, p2, pltpu.roll(rx, 1, 0))
        x2 = jnp.where(rowi == 0, p1, jnp.where(rowi == 1, p2, pltpu.roll(rx, 2, 0)))
        x3 = jnp.where(rowi == 0, p0, jnp.where(rowi == 1, p1, jnp.where(rowi == 2, p2, pltpu.roll(rx, 3, 0))))
        xc = cb_ref[...] + x3 * cw[0:1, :]
        xc = xc + x2 * cw[1:2, :]
        xc = xc + x1 * cw[2:3, :]
        xc = xc + rx * cw[3:4, :]
        a, b = _rg_gates(xc, wa_ref, ba_ref[...], wx_ref, bx_ref[...], lam_ref[...])
        for c in range(nl):
            for j in range(SUBLANE):
                dst = pl.ds(j * pitch, seg)
                ra_scr[bi, c, dst, :] = a[j * seg:(j + 1) * seg, c * LANE:(c + 1) * LANE]
                rb_scr[bi, c, dst, :] = b[j * seg:(j + 1) * seg, c * LANE:(c + 1) * LANE]

    chains = [(bi, c) for bi in range(nb) for c in range(nl)]
    hloc = {ch: jnp.zeros((SUBLANE, LANE), F32) for ch in chains}
    pc = {ch: jnp.ones((SUBLANE, LANE), F32) for ch in chains}
    for i in range(seg):
        idx = pl.ds(i, SUBLANE, stride=pitch)
        for ch in chains:
            bi, c = ch
            ai = ra_scr[bi, c, idx, :]
            hloc[ch] = ai * hloc[ch] + rb_scr[bi, c, idx, :]
            pc[ch] = ai * pc[ch]
            rh_scr[bi, c, idx, :] = hloc[ch]
            rp_scr[bi, c, idx, :] = pc[ch]
    outs = []
    for bi in range(nb):
        hl = jnp.concatenate([hloc[(bi, c)] for c in range(nl)], axis=1)
        pl_ = jnp.concatenate([pc[(bi, c)] for c in range(nl)], axis=1)
        hprev = hc_scr[bi, 0:1, :]
        carries = []
        for j in range(SUBLANE):
            carries.append(jnp.broadcast_to(hprev, (seg, RG_WIDTH)))
            hprev = hl[j:j + 1, :] + pl_[j:j + 1, :] * hprev
        hc_scr[bi, 0:1, :] = hprev
        carry = jnp.concatenate(carries, axis=0)
        unpitch = lambda r, c: jnp.concatenate([r[bi, c, pl.ds(j * pitch, seg), :] for j in range(SUBLANE)], axis=0)
        h = jnp.concatenate(
            [unpitch(rh_scr, c) + unpitch(rp_scr, c) * carry[:, c * LANE:(c + 1) * LANE] for c in range(nl)], axis=1)
        oc_scr[bi][:, GLA_VP:GLA_VP + RG_WIDTH] = h * _gelu_tanh(h_scr[bi][:,C_RY:C_RY + RG_WIDTH])
        tail = h_scr[bi][pl.ds(tt - (CONV_W - 1), CONV_W - 1), C_RX:C_RX + RG_WIDTH]
        cc_scr[bi, 0:CONV_W - 1, :] = tail
        outs.append((hprev, tail))
    return outs


def _stack_heads(x, masks):
    return jnp.concatenate([x * m for m in masks], axis=0)


def _unstack_heads_t(ot, seg, n):
    heads = ot.shape[0] // seg
    blocks = [ot[h * seg:(h + 1) * seg, h * n:(h + 1) * n] for h in range(heads)]
    return jnp.concatenate(blocks, axis=0).T


def _softmax_cols(s, sink=None):
    m = jnp.max(s, axis=0, keepdims=True)
    if sink is not None:
        m = jnp.maximum(m, sink)
    e = jnp.exp(s - m)
    den = jnp.sum(e, axis=0, keepdims=True)
    if sink is not None:
        den = den + jnp.exp(sink - m)
    return e * (1.0 / den)


def _swa_tiles(h_scr, sink_ref, layer, first, gq_ref, gk_ref, b256_ref, b128_ref, dup_ref, dupt_ref, kp_scr, vpt_scr,
               oc_scr):
    nb, tt = len(h_scr), h_scr[0].shape[0]
    w = WINDOW
    nh = SWA_HEADS
    hmb = [_lane_head_mask(SWA_Q, HEAD_DIM, h, BF16) for h in range(nh)]
    kj = lax.broadcasted_iota(jnp.int32, (2 * w, w), 0)
    qi = lax.broadcasted_iota(jnp.int32, (2 * w, w), 1)
    dist_i = qi + w - kj
    in_window = (dist_i >= 0) & (dist_i <= WINDOW)
    dist = dist_i.astype(F32)
    bias = jnp.concatenate([jnp.where(in_window, -_alibi_slope(h) * dist, NEG) for h in range(nh)], axis=1)
    kj4 = lax.broadcasted_iota(jnp.int32, (2 * w, nh * w), 0)
    bias_first = jnp.where(jnp.logical_and(first, kj4 < w), NEG, bias)
    sink_row = jnp.concatenate([jnp.full((1, w), sink_ref[layer, h], F32) for h in range(nh)], axis=1)
    dup = dup_ref[...]
    dupt = dupt_ref[...]
    nt = (((1,), (1,)), ((), ()))
    s_parts, vt_parts, wins = [], [], []
    for bi in range(nb):
        qn = _seg_rms(h_scr[bi][:, C_SQ:C_SQ + SWA_Q], b256_ref[...], HEAD_DIM, gq_ref[...])
        qn = (qn * (HEAD_DIM ** -0.5)).astype(BF16)
        kn = _seg_rms(h_scr[bi][:, C_SK:C_SK + SWA_KVW], b128_ref[...], HEAD_DIM, gk_ref[...])
        v = h_scr[bi][:, C_SV:C_SV + SWA_KVW]
        vt = v.T
        kcat = jnp.concatenate([kp_scr[bi], kn], axis=0).astype(BF16)
        vtcat = jnp.concatenate([vpt_scr[bi], vt], axis=1).astype(BF16)
        kexp = jnp.dot(kcat, dup, preferred_element_type=F32).astype(BF16)
        vexpt = jnp.dot(dupt, vtcat, preferred_element_type=F32).astype(BF16)
        for blk in range(tt // w):
            q4 = _stack_heads(qn[blk * w:(blk + 1) * w], hmb)
            s = lax.dot_general(kexp[blk * w:(blk + 2) * w], q4, nt, preferred_element_type=F32)
            s_parts.append(s + (bias_first if blk == 0 else bias))
            vt_parts.append(vexpt[:, blk * w:(blk + 2) * w])
        k_win = kn[tt - w:tt]
        kp_scr[bi] = k_win
        vpt_scr[bi] = vt[:, tt - w:tt]
        wins.append((k_win, v[tt - w:tt]))
    p_all = _softmax_cols(jnp.concatenate(s_parts, axis=1),
                          jnp.concatenate([sink_row] * len(s_parts), axis=1)).astype(BF16)
    n4 = nh * w
    for i, vtb in enumerate(vt_parts):
        bi, blk = divmod(i, tt // w)
        ot = jnp.dot(vtb, p_all[:, i * n4:(i + 1) * n4], preferred_element_type=F32)
        oc_scr[bi][blk * w:(blk + 1) * w, GLA_VP + RG_WIDTH:MIX_W] = _unstack_heads_t(ot, HEAD_DIM, w)
    return wins


def _mem_attend_tiles(x1, nb, gx, wq_ref, gq, bones, mk_ref, mvt_ref, wmo_ref):
    nh = MEM_HEADS
    tt = x1.shape[0] // nb
    hmb = [_lane_head_mask(MEM_W, MEM_HD, h, BF16) for h in range(nh)]
    nt = (((1,), (1,)), ((), ()))
    qn = (_mem_q(x1, gx, wq_ref, gq, bones) * (MEM_HD ** -0.5)).astype(BF16)
    s_parts = [lax.dot_general(mk_ref[bi].astype(BF16), _stack_heads(qn[bi * tt:(bi + 1) * tt], hmb), nt,
                               preferred_element_type=F32) for bi in range(nb)]
    p_all = _softmax_cols(jnp.concatenate(s_parts, axis=1)).astype(BF16)
    n4 = nh * tt
    o = jnp.concatenate(
        [_unstack_heads_t(jnp.dot(mvt_ref[bi].astype(BF16), p_all[:, bi * n4:(bi + 1) * n4],
                                  preferred_element_type=F32), MEM_HD, tt) for bi in range(nb)], axis=0)
    return x1 + jnp.dot(o.astype(BF16), wmo_ref[...], preferred_element_type=F32)


def _mixer_body(sink_ref, x_ref, gmix_ref, win_ref, wg_ref, bg_ref, gout_ref, cw_ref, cb_ref, wa_ref, ba_ref,
                wx_ref, bx_ref, lam_ref, gq_ref, gk_ref, b256_ref, b128_ref, dup_ref, dupt_ref, wo_ref, gmx_ref, wq_ref,
                mgq_ref, mk_ref, mv_ref, wmo_ref,
                o_ref, st_ref, hl_ref, cbuf_ref, ko_ref, vo_ref,
                h_scr, oc_scr, st_scr, cum_scr, d_scr, hc_scr, cc_scr, ra_scr, rb_scr, rh_scr, rp_scr, kp_scr, vp_scr,
                *, layer):
    nb = x_ref.shape[0]
    first = pl.program_id(0) == 0

    @pl.when(first)
    def _():
        for scr in (st_scr, hc_scr, cc_scr, kp_scr, vp_scr):
            scr[...] = jnp.zeros_like(scr)

    tt = x_ref.shape[1]
    xn = jnp.concatenate([_rms(x_ref[bi], gmix_ref[...]).astype(BF16) for bi in range(nb)], axis=0)
    h_scr[...] = jnp.dot(xn, win_ref[...], preferred_element_type=F32)
    hs = [h_scr.at[pl.ds(bi * tt, tt)] for bi in range(nb)]
    ocs = [oc_scr.at[pl.ds(bi * tt, tt)] for bi in range(nb)]
    for bi in range(nb):
        st_ref[bi] = _gla_tile(hs[bi], wg_ref, bg_ref, gout_ref, st_scr.at[bi], cum_scr.at[bi], d_scr.at[bi], ocs[bi])
    rg_out = _rg_tiles(hs, cw_ref, cb_ref, wa_ref, ba_ref, wx_ref, bx_ref, lam_ref, hc_scr, cc_scr, ra_scr, rb_scr,
                       rh_scr, rp_scr, ocs)
    for bi, (h_last, tail) in enumerate(rg_out):
        hl_ref[bi] = h_last
        cbuf_ref[bi] = tail
    wins = _swa_tiles(hs, sink_ref, layer, first, gq_ref, gk_ref, b256_ref, b128_ref, dup_ref, dupt_ref, kp_scr,
                      vp_scr, ocs)
    for bi, (k_win, v_win) in enumerate(wins):
        ko_ref[bi] = k_win
        vo_ref[bi] = v_win
    x1 = jnp.concatenate([x_ref[bi] for bi in range(nb)], axis=0)
    x1 = x1 + jnp.dot(oc_scr[...].astype(BF16), wo_ref[...], preferred_element_type=F32)
    x2 = _mem_attend_tiles(x1, nb, gmx_ref[...], wq_ref, mgq_ref[...], b256_ref[...], mk_ref, mv_ref, wmo_ref)
    for bi in range(nb):
        o_ref[bi] = x2[bi * tt:(bi + 1) * tt]


def _mixer_prompt(sinks, x3d, gmix3, w_in_p, wg, bg3, gout3, cw, cb3, wa, ba3, wx, bx3, lam3, gq3, gk3, b256, b128,
                  dup, dupt, wo, gmx3, wq, mgq3, mk, mvt, wmo, layer):
    nb, t_len, _ = x3d.shape
    tt = MIX_T
    n_mem = mk.shape[2]
    lay3 = lambda t: (layer, 0, 0)
    once = dict(pipeline_mode=pl.Buffered(1))
    vec = lambda w: pl.BlockSpec((None, 1, w), lay3)
    full = lambda a: pl.BlockSpec(a.shape, lambda t: (0,) * a.ndim)
    state = lambda r, w: pl.BlockSpec((nb, r, w), lambda t: (0, 0, 0))
    mem = pl.BlockSpec((None, nb, n_mem, MEM_W), lambda t: (layer, 0, 0, 0))
    memt = pl.BlockSpec((None, nb, MEM_W, n_mem), lambda t: (layer, 0, 0, 0))
    return pl.pallas_call(
        functools.partial(_mixer_body, layer=layer),
        grid=(t_len // tt,),
        in_specs=[
            pl.BlockSpec(memory_space=pltpu.SMEM),
            pl.BlockSpec((nb, tt, D_MODEL), lambda t: (0, t, 0)),
            vec(D_MODEL),
            pl.BlockSpec((None, D_MODEL, NP), lay3, **once),
            pl.BlockSpec((None, GLA_QKP, GLA_QKP), lay3), vec(GLA_QKP), vec(GLA_DVP),
            pl.BlockSpec((None, CONV_W, RG_WIDTH), lay3), vec(RG_WIDTH),
            pl.BlockSpec((None, RG_WIDTH, RG_WIDTH), lay3), vec(RG_WIDTH),
            pl.BlockSpec((None, RG_WIDTH, RG_WIDTH), lay3), vec(RG_WIDTH), vec(RG_WIDTH),
            vec(SWA_Q), vec(SWA_KVW), full(b256), full(b128), full(dup), full(dupt),
            pl.BlockSpec((None, MIX_W, D_MODEL), lay3, **once),
            vec(D_MODEL), pl.BlockSpec((None, D_MODEL, MEM_W), lay3), vec(MEM_W),
            mem, memt,
            pl.BlockSpec((None, MEM_W, D_MODEL), lay3),
        ],
        out_specs=[
            pl.BlockSpec((nb, tt, D_MODEL), lambda t: (0, t, 0)),
            state(GLA_VP, GLA_QKP), state(1, RG_WIDTH), state(CONV_W - 1, RG_WIDTH),
            state(WINDOW, SWA_KVW), state(WINDOW, SWA_KVW),
        ],
        out_shape=[
            jax.ShapeDtypeStruct((nb, t_len, D_MODEL), F32),
            jax.ShapeDtypeStruct((nb, GLA_VP, GLA_QKP), F32),
            jax.ShapeDtypeStruct((nb, 1, RG_WIDTH), F32),
            jax.ShapeDtypeStruct((nb, CONV_W - 1, RG_WIDTH), F32),
            jax.ShapeDtypeStruct((nb, WINDOW, SWA_KVW), F32),
            jax.ShapeDtypeStruct((nb, WINDOW, SWA_KVW), F32),
        ],
        scratch_shapes=[
            pltpu.VMEM((nb * tt, NP), F32), pltpu.VMEM((nb * tt, MIX_W), F32),
            pltpu.VMEM((nb, GLA_VP, GLA_QKP), F32), pltpu.VMEM((nb, tt, GLA_QKP), F32),
            pltpu.VMEM((nb, GLA_HEADS, 2, tt // 2, tt // 2), F32),
            pltpu.VMEM((nb, SUBLANE, RG_WIDTH), F32), pltpu.VMEM((nb, SUBLANE, RG_WIDTH), F32),
            *[pltpu.VMEM((nb, RG_WIDTH // LANE, SUBLANE * _rg_pitch(tt), LANE), F32) for _ in range(4)],
            pltpu.VMEM((nb, WINDOW, SWA_KVW), F32), pltpu.VMEM((nb, WINDOW, SWA_KVW), F32),
        ],
        compiler_params=_cparams("arbitrary"),
        name="mixer_prompt",
    )(sinks, x3d, gmix3, w_in_p, wg, bg3, gout3, cw, cb3, wa, ba3, wx, bx3, lam3, gq3, gk3, b256, b128, dup, dupt,
      wo, gmx3, wq, mgq3, mk, mvt, wmo)


def _sample_rows_body(h_ref, hst_ref, cbuf_ref, cw_ref, cb_ref, wa_ref, ba_ref, wx_ref, bx_ref, lam_ref, gq_ref,
                      gk_ref, b256_ref, b128_ref, dup_ref,
                      org_ref, hn_ref, cnew_ref, qn_ref, kx_ref, vx_ref, knt_ref, vnt_ref):
    rx = h_ref[:, C_RX:C_RX + RG_WIDTH]
    cw = cw_ref[...]
    xc = cb_ref[...] + cbuf_ref[0] * cw[0:1, :]
    xc = xc + cbuf_ref[1] * cw[1:2, :]
    xc = xc + cbuf_ref[2] * cw[2:3, :]
    xc = xc + rx * cw[3:4, :]
    a, b = _rg_gates(xc, wa_ref, ba_ref[...], wx_ref, bx_ref[...], lam_ref[...])
    hn = a * hst_ref[...] + b
    hn_ref[...] = hn
    org_ref[...] = hn * _gelu_tanh(h_ref[:, C_RY:C_RY + RG_WIDTH])
    cnew_ref[0] = cbuf_ref[1]
    cnew_ref[1] = cbuf_ref[2]
    cnew_ref[2] = rx

    qn_ref[...] = _seg_rms(h_ref[:, C_SQ:C_SQ + SWA_Q], b256_ref[...], HEAD_DIM, gq_ref[...])
    kn = _seg_rms(h_ref[:, C_SK:C_SK + SWA_KVW], b128_ref[...], HEAD_DIM, gk_ref[...])
    v = h_ref[:, C_SV:C_SV + SWA_KVW]
    kx_ref[...] = jnp.dot(kn.astype(BF16), dup_ref[...], preferred_element_type=F32)
    vx_ref[...] = jnp.dot(v.astype(BF16), dup_ref[...], preferred_element_type=F32)
    knt = kn.T
    vnt = v.T
    nblk, _, bt = knt_ref.shape
    for i in range(nblk):
        knt_ref[i] = knt[:, i * bt:(i + 1) * bt]
        vnt_ref[i] = vnt[:, i * bt:(i + 1) * bt]


def _sample_rows(hs, hst, cbuf, cw, cb3, wa, ba3, wx, bx3, lam3, gq3, gk3, b256, b128, dup, layer, bt):
    n = hs.shape[0]
    lay3 = lambda i: (layer, 0, 0)
    full = lambda a: pl.BlockSpec(a.shape, lambda i: (0,) * a.ndim)
    vec = lambda w: pl.BlockSpec((None, 1, w), lay3)
    sq = lambda w: pl.BlockSpec((None, w, w), lay3)
    taps = pl.BlockSpec((None, CONV_W - 1, n, RG_WIDTH), lambda i: (layer, 0, 0, 0))
    shapes = [(n, RG_WIDTH), (n, RG_WIDTH), (CONV_W - 1, n, RG_WIDTH), (n, SWA_Q), (n, SWA_Q), (n, SWA_Q),
              (n // bt, SWA_KVW, bt), (n // bt, SWA_KVW, bt)]
    return pl.pallas_call(
        _sample_rows_body,
        grid=(1,),
        in_specs=[
            full(hs), pl.BlockSpec((None, n, RG_WIDTH), lay3), taps,
            pl.BlockSpec((None, CONV_W, RG_WIDTH), lay3), vec(RG_WIDTH),
            sq(RG_WIDTH), vec(RG_WIDTH), sq(RG_WIDTH), vec(RG_WIDTH), vec(RG_WIDTH),
            vec(SWA_Q), vec(SWA_KVW), full(b256), full(b128), full(dup),
        ],
        out_specs=[pl.BlockSpec(s, lambda i, nd=len(s): (0,) * nd) for s in shapes],
        out_shape=[jax.ShapeDtypeStruct(s, F32) for s in shapes],
        compiler_params=_cparams("arbitrary"),
        name="sample_rows",
    )(hs, hst, cbuf, cw, cb3, wa, ba3, wx, bx3, lam3, gq3, gk3, b256, b128, dup)


def _sample_gla_body(s0_ref, h_ref, wg_ref, bg_ref, goutc_ref, o_ref, sn_ref, qt_scr, kt_scr, et_scr, vt_scr, gt_scr):
    h = pl.program_id(0)
    qraw = h_ref[:, C_GQ:C_GQ + GLA_QKP]
    z = jnp.dot(qraw.astype(BF16), wg_ref[...], preferred_element_type=F32) + bg_ref[...]
    et_scr[...] = jnp.exp(_log_sigmoid(z) * (1.0 / GLA_TAU)).T
    qt_scr[...] = (qraw * (GLA_DK ** -0.5)).T
    kt_scr[...] = h_ref[:, C_GK:C_GK + GLA_QKP].T
    vt_scr[...] = h_ref[:, C_GV:C_GV + GLA_VP].T
    gt_scr[...] = h_ref[:, C_GOG:C_GOG + GLA_VP].T
    r0 = pl.multiple_of(h * GLA_DK, SUBLANE)
    v0 = pl.multiple_of(h * GLA_DVP, GLA_DVP)
    vt = vt_scr[pl.ds(v0, GLA_DV), :]
    o = jnp.zeros(vt.shape, F32)
    for k in range(GLA_DK):
        sn = et_scr[pl.ds(r0 + k, 1), :] * s0_ref[k] + kt_scr[pl.ds(r0 + k, 1), :] * vt
        sn_ref[k] = sn
        o = o + qt_scr[pl.ds(r0 + k, 1), :] * sn
    ms = jnp.sum(o * o, axis=0, keepdims=True) * (1.0 / GLA_DV)
    gate = gt_scr[pl.ds(v0, GLA_DV), :]
    y = ((o * lax.rsqrt(ms + EPS)) * goutc_ref[...]) * (gate * _sigmoid(gate))
    y = jnp.concatenate([y, jnp.zeros((GLA_DVP - GLA_DV, y.shape[1]), F32)], axis=0)
    o_ref[...] = y.T


def _sample_gla(s0t, hs, wg, bg3, goutc, layer):
    n = hs.shape[0]
    lay3 = lambda h: (layer, 0, 0)
    return pl.pallas_call(
        _sample_gla_body,
        grid=(GLA_HEADS,),
        in_specs=[
            pl.BlockSpec((None, None, GLA_DK, GLA_DV, n), lambda h: (layer, h, 0, 0, 0)),
            pl.BlockSpec((n, NP), lambda h: (0, 0)),
            pl.BlockSpec((None, GLA_QKP, GLA_QKP), lay3),
            pl.BlockSpec((None, 1, GLA_QKP), lay3),
            pl.BlockSpec((None, GLA_DV, 1), lay3),
        ],
        out_specs=[pl.BlockSpec((n, GLA_DVP), lambda h: (0, h)),
                   pl.BlockSpec((None, GLA_DK, GLA_DV, n), lambda h: (h, 0, 0, 0))],
        out_shape=[jax.ShapeDtypeStruct((n, GLA_VP), F32),
                   jax.ShapeDtypeStruct((GLA_HEADS, GLA_DK, GLA_DV, n), F32)],
        scratch_shapes=[pltpu.VMEM((GLA_QKP, n), F32)] * 3 + [pltpu.VMEM((GLA_VP, n), F32)] * 2,
        compiler_params=_cparams("arbitrary"),
        name="sample_gla",
    )(s0t, hs, wg, bg3, goutc)


def _sample_swa_body(sink_ref, kct_ref, vct_ref, qn_ref, kx_ref, vx_ref, knt_ref, vnt_ref, dupt_ref, o_ref, kot_ref,
                     vot_ref, *, layer):
    bt = kct_ref.shape[0]
    dupt = dupt_ref[...]
    hi = lax.broadcasted_iota(jnp.int32, (SWA_HEADS, SWA_Q), 0)
    li = lax.broadcasted_iota(jnp.int32, (SWA_HEADS, SWA_Q), 1)
    hm4 = ((li >= hi * HEAD_DIM) & (li < (hi + 1) * HEAD_DIM)).astype(F32)
    hrow = lax.broadcasted_iota(jnp.int32, (SWA_HEADS, 1), 0)
    slope = jnp.zeros((SWA_HEADS, 1), F32)
    sink = jnp.zeros((SWA_HEADS, 1), F32)
    for h in range(SWA_HEADS):
        slope = jnp.where(hrow == h, _alibi_slope(h), slope)
        sink = jnp.where(hrow == h, sink_ref[layer, h], sink)
    dist = (WINDOW - lax.broadcasted_iota(jnp.int32, (1, WINDOW), 1)).astype(F32)
    last = lax.broadcasted_iota(jnp.int32, (SWA_KVW, WINDOW), 1) == WINDOW - 1
    nt = (((1,), (1,)), ((), ()))
    for j in range(bt):
        kt = kct_ref[j]
        vt = vct_ref[j]
        kexp = jnp.dot(dupt, kt.astype(BF16), preferred_element_type=F32).astype(BF16)
        vexp = jnp.dot(dupt, vt.astype(BF16), preferred_element_type=F32).astype(BF16)
        q4 = (qn_ref[j:j + 1, :] * hm4).astype(BF16)
        s = jnp.dot(q4, kexp, preferred_element_type=F32)
        s = s * (HEAD_DIM ** -0.5) - slope * dist
        s_new = jnp.sum(q4.astype(F32) * kx_ref[j:j + 1, :], axis=-1, keepdims=True) * (HEAD_DIM ** -0.5)
        m = jnp.maximum(jnp.maximum(jnp.max(s, axis=-1, keepdims=True), s_new), sink)
        e = jnp.exp(s - m)
        e_new = jnp.exp(s_new - m)
        den = jnp.sum(e, axis=-1, keepdims=True) + e_new + jnp.exp(sink - m)
        o4 = lax.dot_general((e / den).astype(BF16), vexp, nt, preferred_element_type=F32)
        o4 = o4 + (e_new / den).astype(BF16).astype(F32) * vx_ref[j:j + 1, :]
        o_ref[j:j + 1, :] = jnp.sum(o4 * hm4, axis=0, keepdims=True)
        kot_ref[j] = jnp.where(last, knt_ref[:, j:j + 1], pltpu.roll(kt, WINDOW - 1, 1))
        vot_ref[j] = jnp.where(last, vnt_ref[:, j:j + 1], pltpu.roll(vt, WINDOW - 1, 1))


def _sample_swa(sinks, kct, vct, qn, kx, vx, knt, vnt, dupt, layer, bt):
    n = qn.shape[0]
    row = lambda w: pl.BlockSpec((bt, w), lambda i: (i, 0))
    cache = pl.BlockSpec((None, bt, SWA_KVW, WINDOW), lambda i: (layer, i, 0, 0))
    cache_o = pl.BlockSpec((bt, SWA_KVW, WINDOW), lambda i: (i, 0, 0))
    col = pl.BlockSpec((None, SWA_KVW, bt), lambda i: (i, 0, 0))
    return pl.pallas_call(
        functools.partial(_sample_swa_body, layer=layer),
        grid=(n // bt,),
        in_specs=[
            pl.BlockSpec(memory_space=pltpu.SMEM), cache, cache,
            row(SWA_Q), row(SWA_Q), row(SWA_Q), col, col,
            pl.BlockSpec(dupt.shape, lambda i: (0, 0)),
        ],
        out_specs=[row(SWA_Q), cache_o, cache_o],
        out_shape=[
            jax.ShapeDtypeStruct((n, SWA_Q), F32),
            jax.ShapeDtypeStruct((n, SWA_KVW, WINDOW), F32),
            jax.ShapeDtypeStruct((n, SWA_KVW, WINDOW), F32),
        ],
        compiler_params=_cparams("parallel"),
        name="sample_swa",
    )(sinks, kct, vct, qn, kx, vx, knt, vnt, dupt)


def _sample_post1_body(x_ref, og_ref, or_ref, os_ref, wo_ref, gx_ref, wq_ref, gq_ref, bones_ref, x1_ref, qn_ref):
    x1 = _mix_out(x_ref[...], og_ref[...], or_ref[...], os_ref[...], wo_ref)
    x1_ref[...] = x1
    qn_ref[...] = _mem_q(x1, gx_ref[...], wq_ref, gq_ref[...], bones_ref[...])


def _sample_post1(x2d, og, orr, osw, wo, gx3, wq, gq3, bones256, layer):
    n = x2d.shape[0]
    lay3 = lambda i: (layer, 0, 0)
    full = lambda a: pl.BlockSpec(a.shape, lambda i: (0,) * a.ndim)
    return pl.pallas_call(
        _sample_post1_body,
        grid=(1,),
        in_specs=[
            full(x2d), full(og), full(orr), full(osw),
            pl.BlockSpec((None, MIX_W, D_MODEL), lay3),
            pl.BlockSpec((None, 1, D_MODEL), lay3),
            pl.BlockSpec((None, D_MODEL, MEM_W), lay3),
            pl.BlockSpec((None, 1, MEM_W), lay3),
            full(bones256),
        ],
        out_specs=[pl.BlockSpec((n, D_MODEL), lambda i: (0, 0)), pl.BlockSpec((n, MEM_W), lambda i: (0, 0))],
        out_shape=[jax.ShapeDtypeStruct((n, D_MODEL), F32), jax.ShapeDtypeStruct((n, MEM_W), F32)],
        compiler_params=_cparams("arbitrary"),
        name="sample_post1",
    )(x2d, og, orr, osw, wo, gx3, wq, gq3, bones256)


def _sample_mem_body(qn_ref, mk_ref, mv_ref, o_ref):
    bt = mk_ref.shape[0]
    hi = lax.broadcasted_iota(jnp.int32, (MEM_HEADS, MEM_W), 0)
    li = lax.broadcasted_iota(jnp.int32, (MEM_HEADS, MEM_W), 1)
    hm4 = ((li >= hi * MEM_HD) & (li < (hi + 1) * MEM_HD)).astype(F32)
    for j in range(bt):
        q4 = (qn_ref[j:j + 1, :] * hm4).astype(BF16)
        s = jnp.dot(q4, mk_ref[j].astype(BF16), preferred_element_type=F32)
        p = _softmax_rows(s * (MEM_HD ** -0.5))
        o4 = lax.dot_general(p.astype(BF16), mv_ref[j].astype(BF16), (((1,), (1,)), ((), ())),
                             preferred_element_type=F32)
        o_ref[j:j + 1, :] = jnp.sum(o4 * hm4, axis=0, keepdims=True)


def _sample_mem(qn, mk, mv, layer, bt):
    n = qn.shape[0]
    n_mem = mk.shape[3]
    cache = pl.BlockSpec((None, bt, MEM_W, n_mem), lambda i: (layer, i, 0, 0))
    return pl.pallas_call(
        _sample_mem_body,
        grid=(n // bt,),
        in_specs=[pl.BlockSpec((bt, MEM_W), lambda i: (i, 0)), cache, cache],
        out_specs=pl.BlockSpec((bt, MEM_W), lambda i: (i, 0)),
        out_shape=jax.ShapeDtypeStruct((n, MEM_W), F32),
        compiler_params=_cparams("parallel"),
        name="sample_mem",
    )(qn, mk, mv)


def _ffn_res_body(x_ref, o_ref_in, wmo_ref, g_ref, w1_ref, w2_ref, out_ref):
    x = x_ref[...] + jnp.dot(o_ref_in[...].astype(BF16), wmo_ref[...], preferred_element_type=F32)
    xn = _rms(x, g_ref[...]).astype(BF16)
    acc = x
    for c in range(D_FF // FF_CHUNK):
        h = jnp.dot(xn, w1_ref[:, c * FF_CHUNK:(c + 1) * FF_CHUNK], preferred_element_type=F32)
        h = jnp.square(jnp.maximum(h, 0.0))
        acc = acc + jnp.dot(h.astype(BF16), w2_ref[c * FF_CHUNK:(c + 1) * FF_CHUNK, :], preferred_element_type=F32)
    out_ref[...] = acc


def _sample_ffn(x1, o, wmo, g3, w1, w2, layer):
    n = x1.shape[0]
    lay3 = lambda i: (layer, 0, 0)
    const = dict(pipeline_mode=pl.Buffered(1))
    return pl.pallas_call(
        _ffn_res_body,
        grid=(1,),
        in_specs=[
            pl.BlockSpec((n, D_MODEL), lambda i: (0, 0)),
            pl.BlockSpec((n, MEM_W), lambda i: (0, 0)),
            pl.BlockSpec((None, MEM_W, D_MODEL), lay3),
            pl.BlockSpec((None, 1, D_MODEL), lay3),
            pl.BlockSpec((None, D_MODEL, D_FF), lay3, **const),
            pl.BlockSpec((None, D_FF, D_MODEL), lay3, **const),
        ],
        out_specs=pl.BlockSpec((n, D_MODEL), lambda i: (0, 0)),
        out_shape=jax.ShapeDtypeStruct((n, D_MODEL), F32),
        compiler_params=_cparams("arbitrary"),
        name="sample_ffn",
    )(x1, o, wmo, g3, w1, w2)


def _pad_heads(a, axis, heads, width, padded):
    shp = a.shape
    a = a.reshape(shp[:axis] + (heads, width) + shp[axis + 1:])
    pad = [(0, 0)] * a.ndim
    pad[axis + 1] = (0, padded - width)
    a = jnp.pad(a, pad)
    return a.reshape(shp[:axis] + (heads * padded,) + shp[axis + 1:])


def _prep_w_in(w_in):
    offs = [0]
    for w in (GLA_QK, GLA_QK, GLA_V, GLA_RANK, GLA_V, RG_WIDTH, RG_WIDTH, SWA_Q, SWA_KVW, SWA_KVW):
        offs.append(offs[-1] + w)
    gq, gk, gv, glr, gog, rx, ry, sq, sk, sv = [w_in[..., offs[i]:offs[i + 1]] for i in range(10)]
    zeros = lambda n: jnp.zeros(w_in.shape[:-1] + (n,), w_in.dtype)
    cols = [rx, ry,
            gq, glr, zeros(GLA_QKP - GLA_QK - GLA_RANK),
            gk, zeros(GLA_QKP - GLA_QK),
            sq,
            _pad_heads(gv, 2, GLA_HEADS, GLA_DV, GLA_DVP),
            _pad_heads(gog, 2, GLA_HEADS, GLA_DV, GLA_DVP),
            sk, sv]
    out = jnp.concatenate(cols, axis=-1)
    assert out.shape[-1] == NP
    return out.astype(BF16)


def _block_diag(w):
    depth, nb, bw, _ = w.shape
    eye = jnp.eye(nb, dtype=w.dtype)
    return jnp.einsum("lncd,nm->lncmd", w, eye).reshape(depth, nb * bw, nb * bw)


def _block_ones(n, seg):
    i = jnp.arange(n) // seg
    return (i[:, None] == i[None, :]).astype(BF16)


def _dup_matrix():
    src = jnp.arange(SWA_Q)
    src = (src // HEAD_DIM // SWA_G) * HEAD_DIM + src % HEAD_DIM
    return (jnp.arange(SWA_KVW)[:, None] == src[None, :]).astype(BF16)


def kernel(x_prompt, x_sample, mem_prompt, state_gla, state_rg_h, state_rg_conv, cache_swa_k, cache_swa_v, cache_mem_k, cache_mem_v, g_mix, w_in, gla_w_gate2, gla_b_gate, gla_g_out, rg_conv_w, rg_conv_b, rg_w_a, rg_b_a, rg_w_x, rg_b_x, rg_lam, swa_g_q, swa_g_k, swa_sinks, w_out, g_mem_x, g_mem_m, mem_w_q, mem_w_kv, mem_g_q, mem_g_k, mem_w_o, g_ffn, ffn_w1, ffn_w2):
    depth = w_in.shape[0]
    nb, t_len, _ = x_prompt.shape
    ns = x_sample.shape[0]
    n_mem = mem_prompt.shape[1]
    assert t_len % ROW_T == 0 and t_len % MIX_T == 0 and x_sample.shape[1] == 1 and ns % 8 == 0
    row3 = lambda a: a.reshape(depth, 1, a.shape[-1])

    w_in_p = _prep_w_in(w_in)
    wg = jnp.zeros((depth, GLA_QKP, GLA_QKP), F32).at[:, GLR_LANE:GLR_LANE + GLA_RANK, :GLA_QK].set(gla_w_gate2)
    wg = wg.astype(BF16)
    bg3 = row3(jnp.pad(gla_b_gate, ((0, 0), (0, GLA_QKP - GLA_QK))))
    gout3 = row3(jnp.pad(gla_g_out, ((0, 0), (0, GLA_DVP - GLA_DV))))
    wa = _block_diag(rg_w_a).astype(BF16)
    wx = _block_diag(rg_w_x).astype(BF16)
    ba3, bx3, lam3, cb3 = row3(rg_b_a), row3(rg_b_x), row3(rg_lam), row3(rg_conv_b)
    gq3 = row3(jnp.tile(swa_g_q, (1, SWA_HEADS)))
    gk3 = row3(jnp.tile(swa_g_k, (1, SWA_KV)))
    wo = jnp.concatenate([_pad_heads(w_out[:, :GLA_V], 1, GLA_HEADS, GLA_DV, GLA_DVP), w_out[:, GLA_V:]], axis=1)
    wo = wo.astype(BF16)
    gmx3, gmm3, gffn3, gmix3 = row3(g_mem_x), row3(g_mem_m), row3(g_ffn), row3(g_mix)
    mgq3 = row3(jnp.tile(mem_g_q, (1, MEM_HEADS)))
    mgk3 = row3(jnp.tile(mem_g_k, (1, MEM_HEADS)))
    wq = mem_w_q.astype(BF16)
    wkv = mem_w_kv.astype(BF16)
    wmo = mem_w_o.astype(BF16)
    w1 = ffn_w1.astype(BF16)
    w2 = ffn_w2.astype(BF16)
    b256 = _block_ones(SWA_Q, HEAD_DIM)
    b128 = _block_ones(SWA_KVW, HEAD_DIM)
    dup = _dup_matrix()

    dupt = dup.T
    mk_all, mv_all, mvt_all = _mem_kv(mem_prompt.reshape(nb * n_mem, D_MODEL), gmm3, wkv, mgk3, b256, nb)
    mk_all = mk_all.reshape(depth, nb, n_mem, MEM_W)
    mv_all = mv_all.reshape(depth, nb, n_mem, MEM_W)

    xp = x_prompt.reshape(nb * t_len, D_MODEL)
    xs = x_sample.reshape(ns, D_MODEL)
    rg_conv_t = state_rg_conv.transpose(0, 2, 1, 3)
    gla_t = state_gla.transpose(0, 2, 3, 4, 1)
    swk = cache_swa_k.transpose(0, 1, 3, 4, 2).reshape(depth, ns, SWA_KVW, WINDOW)
    swv = cache_swa_v.transpose(0, 1, 3, 4, 2).reshape(depth, ns, SWA_KVW, WINDOW)
    cmk = cache_mem_k.transpose(0, 1, 3, 4, 2).reshape(depth, ns, MEM_W, n_mem)
    cmv = cache_mem_v.transpose(0, 1, 3, 4, 2).reshape(depth, ns, MEM_W, n_mem)
    goutc = gla_g_out.reshape(depth, GLA_DV, 1)
    sbt = min(ns, 32)

    gla_p, gla_s, rgh_p, rgh_s, rgc_p, rgc_s = [], [], [], [], [], []
    swk_p, swk_s, swv_p, swv_s = [], [], [], []
    ts = min(ns, 128)
    for l in range(depth):
        xp, st, hl, cbuf, kb, vb = _mixer_prompt(
            swa_sinks, xp.reshape(nb, t_len, D_MODEL), gmix3, w_in_p, wg, bg3, gout3, rg_conv_w, cb3, wa, ba3, wx,
            bx3, lam3, gq3, gk3, b256, b128, dup, dupt, wo, gmx3, wq, mgq3, mk_all, mvt_all, wmo, l)
        xp = _ffn(xp.reshape(nb * t_len, D_MODEL), gffn3, w1, w2, l, ROW_T)
        st = st.reshape(nb, GLA_HEADS, GLA_DVP, GLA_QKP)
        gla_p.append(jnp.stack(
            [st[:, h, :GLA_DV, h * GLA_DK:(h + 1) * GLA_DK].transpose(0, 2, 1) for h in range(GLA_HEADS)], axis=1))
        rgh_p.append(hl.reshape(nb, RG_WIDTH))
        rgc_p.append(cbuf)
        swk_p.append(kb.reshape(nb, WINDOW, SWA_KV, HEAD_DIM))
        swv_p.append(vb.reshape(nb, WINDOW, SWA_KV, HEAD_DIM))

        hs = _in_proj(xs, gmix3, w_in_p, l, ts)
        org_s, hn, cnew, qn, kx, vx, knt, vnt = _sample_rows(
            hs, state_rg_h, rg_conv_t, rg_conv_w, cb3, wa, ba3, wx, bx3, lam3, gq3, gk3, b256, b128, dup, l, sbt)
        og_s, sn = _sample_gla(gla_t, hs, wg, bg3, goutc, l)
        osw_s, ko, vo = _sample_swa(swa_sinks, swk, swv, qn, kx, vx, knt, vnt, dupt, l, sbt)
        x1, qm = _sample_post1(xs, og_s, org_s, osw_s, wo, gmx3, wq, mgq3, b256, l)
        om = _sample_mem(qm, cmk, cmv, l, 8)
        xs = _sample_ffn(x1, om, wmo, gffn3, w1, w2, l)
        gla_s.append(sn)
        rgh_s.append(hn)
        rgc_s.append(cnew)
        swk_s.append(ko.reshape(ns, SWA_KV, HEAD_DIM, WINDOW))
        swv_s.append(vo.reshape(ns, SWA_KV, HEAD_DIM, WINDOW))

    return (xp.reshape(nb, t_len, D_MODEL), xs.reshape(ns, 1, D_MODEL),
            jnp.stack(gla_p), jnp.stack(gla_s).transpose(0, 4, 1, 2, 3), jnp.stack(rgh_p), jnp.stack(rgh_s),
            jnp.stack(rgc_p), jnp.stack(rgc_s).transpose(0, 2, 1, 3), jnp.stack(swk_p),
            jnp.stack(swk_s).transpose(0, 1, 4, 2, 3),
            jnp.stack(swv_p), jnp.stack(swv_s).transpose(0, 1, 4, 2, 3),
            mk_all.reshape(depth, nb, n_mem, MEM_HEADS, MEM_HD), mv_all.reshape(depth, nb, n_mem, MEM_HEADS, MEM_HD))
```

```python
import functools
import math

import jax
import jax.numpy as jnp
from jax import lax
from jax.experimental import pallas as pl
from jax.experimental.pallas import tpu as pltpu

F32 = jnp.float32
BF16 = jnp.bfloat16

D_MODEL = 1024
EPS = 1e-6
GLA_HEADS, GLA_DK, GLA_DV, GLA_RANK, GLA_TAU = 4, 48, 96, 16, 16.0
GLA_QK = GLA_HEADS * GLA_DK
GLA_V = GLA_HEADS * GLA_DV
RG_WIDTH, RG_BLOCKS, RG_C, CONV_W = 384, 4, 8.0, 4
RG_BW = RG_WIDTH // RG_BLOCKS
SWA_HEADS, SWA_KV, HEAD_DIM, WINDOW = 4, 2, 64, 128
SWA_G = SWA_HEADS // SWA_KV
SWA_Q = SWA_HEADS * HEAD_DIM
SWA_KVW = SWA_KV * HEAD_DIM
MEM_HEADS, MEM_HD = 4, 64
MEM_W = MEM_HEADS * MEM_HD
D_FF = 4 * D_MODEL

LANE = 128
SUBLANE = 8
GLA_DVP = LANE
GLA_VP = GLA_HEADS * GLA_DVP
GLA_QKP = 2 * LANE
MIX_W = GLA_VP + RG_WIDTH + SWA_Q

C_RX, C_RY, C_GQ, C_GK, C_SQ, C_GV, C_GOG, C_SK, C_SV = 0, 384, 768, 1024, 1280, 1536, 2048, 2560, 2688
NP = 2816
GLR_LANE = GLA_QK

VMEM_LIMIT = 56 * 1024 * 1024
NEG = -1e30

MIX_T = 256
ROW_T = 512
FF_CHUNK = 1024


def _cparams(*sem):
    return pltpu.CompilerParams(dimension_semantics=sem, vmem_limit_bytes=VMEM_LIMIT)


def _rms(x, g):
    ms = jnp.mean(x * x, axis=-1, keepdims=True)
    return (x * lax.rsqrt(ms + EPS)) * g


def _dot(a, b):
    return jnp.dot(a.astype(BF16), b.astype(BF16), preferred_element_type=F32)


def _dot_nt(a, b):
    return lax.dot_general(a.astype(BF16), b.astype(BF16), (((1,), (1,)), ((), ())), preferred_element_type=F32)


def _split2_dot(x, m):
    hi = x.astype(BF16)
    lo = (x - hi.astype(F32)).astype(BF16)
    return jnp.dot(hi, m, preferred_element_type=F32) + jnp.dot(lo, m, preferred_element_type=F32)


def _split3(x):
    x1 = x.astype(BF16)
    r1 = x - x1.astype(F32)
    x2 = r1.astype(BF16)
    x3 = (r1 - x2.astype(F32)).astype(BF16)
    return x1, x2, x3


def _split3_dot_left(m, x):
    d = functools.partial(jnp.dot, preferred_element_type=F32)
    x1, x2, x3 = _split3(x)
    return d(m, x1) + d(m, x2) + d(m, x3)


def _seg_rms(x, bones, seg, g):
    ms = _split2_dot(x * x, bones) * (1.0 / seg)
    return (x * lax.rsqrt(ms + EPS)) * g


def _sigmoid(x):
    return 1.0 / (1.0 + jnp.exp(-x))


def _softplus(x):
    return jnp.maximum(x, 0.0) + jnp.log1p(jnp.exp(-jnp.abs(x)))


def _log_sigmoid(z):
    return -(jnp.maximum(-z, 0.0) + jnp.log(1.0 + jnp.exp(-jnp.abs(z))))


def _gelu_tanh(x):
    return x * (0.5 * (1.0 + jnp.tanh(math.sqrt(2.0 / math.pi) * (x + 0.044715 * (x * x * x)))))


def _lane_head_mask(width, seg, h, dtype=F32):
    lane = lax.broadcasted_iota(jnp.int32, (1, width), 1)
    return ((lane >= seg * h) & (lane < seg * (h + 1))).astype(dtype)


def _masked_softmax_rows(s, sink):
    m = jnp.maximum(jnp.max(s, axis=-1, keepdims=True), sink)
    e = jnp.exp(s - m)
    den = jnp.sum(e, axis=-1, keepdims=True) + jnp.exp(sink - m)
    return e / den


def _softmax_rows(s):
    m = jnp.max(s, axis=-1, keepdims=True)
    e = jnp.exp(s - m)
    return e / jnp.sum(e, axis=-1, keepdims=True)


def _alibi_slope(h):
    return 2.0 ** (-8.0 * (h + 1) / SWA_HEADS)


def _rows(x, starts, size):
    parts = [x[s:s + size] for s in starts]
    return parts[0] if len(parts) == 1 else jnp.concatenate(parts, axis=0)


def _in_proj_body(x_ref, g_ref, w_ref, o_ref):
    xn = _rms(x_ref[...], g_ref[...])
    o_ref[...] = jnp.dot(xn.astype(BF16), w_ref[...], preferred_element_type=F32)


def _in_proj(x2d, g3, w3, layer, tm):
    m = x2d.shape[0]
    return pl.pallas_call(
        _in_proj_body,
        grid=(m // tm,),
        in_specs=[
            pl.BlockSpec((tm, D_MODEL), lambda i: (i, 0)),
            pl.BlockSpec((None, 1, D_MODEL), lambda i: (layer, 0, 0)),
            pl.BlockSpec((None, D_MODEL, NP), lambda i: (layer, 0, 0)),
        ],
        out_specs=pl.BlockSpec((tm, NP), lambda i: (i, 0)),
        out_shape=jax.ShapeDtypeStruct((m, NP), F32),
        compiler_params=_cparams("parallel"),
        name="in_proj",
    )(x2d, g3, w3)


def _ffn_body(x_ref, g_ref, w1_ref, w2_ref, o_ref):
    x = x_ref[...]
    xn = _rms(x, g_ref[...]).astype(BF16)
    acc = x
    for c in range(D_FF // FF_CHUNK):
        h = jnp.dot(xn, w1_ref[:, c * FF_CHUNK:(c + 1) * FF_CHUNK], preferred_element_type=F32)
        h = jnp.square(jnp.maximum(h, 0.0))
        acc = acc + jnp.dot(h.astype(BF16), w2_ref[c * FF_CHUNK:(c + 1) * FF_CHUNK, :], preferred_element_type=F32)
    o_ref[...] = acc


def _ffn(x2d, g3, w1, w2, layer, tm):
    m = x2d.shape[0]
    const = dict(pipeline_mode=pl.Buffered(1))
    return pl.pallas_call(
        _ffn_body,
        grid=(m // tm,),
        in_specs=[
            pl.BlockSpec((tm, D_MODEL), lambda i: (i, 0)),
            pl.BlockSpec((None, 1, D_MODEL), lambda i: (layer, 0, 0)),
            pl.BlockSpec((None, D_MODEL, D_FF), lambda i: (layer, 0, 0), **const),
            pl.BlockSpec((None, D_FF, D_MODEL), lambda i: (layer, 0, 0), **const),
        ],
        out_specs=pl.BlockSpec((tm, D_MODEL), lambda i: (i, 0)),
        out_shape=jax.ShapeDtypeStruct((m, D_MODEL), F32),
        compiler_params=_cparams("parallel"),
        name="ffn",
    )(x2d, g3, w1, w2)


def _mem_kv_body(mem_ref, g_ref, w_ref, gk_ref, bones_ref, k_ref, v_ref, vt_ref):
    kv = jnp.dot(_rms(mem_ref[...], g_ref[...]).astype(BF16), w_ref[...], preferred_element_type=F32)
    k_ref[...] = _seg_rms(kv[:, :MEM_W], bones_ref[...], MEM_HD, gk_ref[...])
    v = kv[:, MEM_W:]
    v_ref[...] = v
    nb, _, n_mem = vt_ref.shape
    for b in range(nb):
        vt_ref[b] = v[b * n_mem:(b + 1) * n_mem].T


def _mem_kv(mem2d, g3, wkv, gk3, bones256, nb):
    depth = wkv.shape[0]
    m = mem2d.shape[0]
    return pl.pallas_call(
        _mem_kv_body,
        grid=(depth,),
        in_specs=[
            pl.BlockSpec((m, D_MODEL), lambda l: (0, 0)),
            pl.BlockSpec((None, 1, D_MODEL), lambda l: (l, 0, 0)),
            pl.BlockSpec((None, D_MODEL, 2 * MEM_W), lambda l: (l, 0, 0)),
            pl.BlockSpec((None, 1, MEM_W), lambda l: (l, 0, 0)),
            pl.BlockSpec((MEM_W, MEM_W), lambda l: (0, 0)),
        ],
        out_specs=[pl.BlockSpec((None, m, MEM_W), lambda l: (l, 0, 0))] * 2
        + [pl.BlockSpec((None, nb, MEM_W, m // nb), lambda l: (l, 0, 0, 0))],
        out_shape=[jax.ShapeDtypeStruct((depth, m, MEM_W), F32)] * 2
        + [jax.ShapeDtypeStruct((depth, nb, MEM_W, m // nb), F32)],
        compiler_params=_cparams("parallel"),
        name="mem_kv",
    )(mem2d, g3, wkv, gk3, bones256)


def _mix_out(x, og, orr, osw, wo_ref):
    y = _dot(og, wo_ref[0:GLA_VP, :])
    y = y + _dot(orr, wo_ref[GLA_VP:GLA_VP + RG_WIDTH, :])
    y = y + _dot(osw, wo_ref[GLA_VP + RG_WIDTH:MIX_W, :])
    return x + y


def _mem_q(x1, gx, wq_ref, gq, bones):
    q = jnp.dot(_rms(x1, gx).astype(BF16), wq_ref[...], preferred_element_type=F32)
    return _seg_rms(q, bones, MEM_HD, gq)


def _gla_tile(h_scr, wg_ref, bg_ref, gout_ref, st_scr, cum_scr, d_scr, oc_scr, st_out):
    tt = h_scr.shape[0]
    hf = tt // 2
    qraw = h_scr[:, C_GQ:C_GQ + GLA_QKP]
    k = h_scr[:, C_GK:C_GK + GLA_QKP]
    z = jnp.dot(qraw.astype(BF16), wg_ref[...], preferred_element_type=F32) + bg_ref[...]
    g = _log_sigmoid(z) * (1.0 / GLA_TAU)
    q = qraw * (GLA_DK ** -0.5)

    row = lax.broadcasted_iota(jnp.int32, (tt, tt), 0)
    col = lax.broadcasted_iota(jnp.int32, (tt, tt), 1)
    cum = _split3_dot_left((row >= col).astype(BF16), g)
    cum_scr[...] = cum
    rowi = lax.broadcasted_iota(jnp.int32, (tt, 1), 0)

    def boundary(b, n):
        return jnp.concatenate(
            [jnp.broadcast_to(cum_scr[pl.ds(gi * 2 * b + b - 1, 1), :], (n, GLA_QKP)) for gi in range(tt // (2 * b))],
            axis=0)

    rd1 = pltpu.roll(cum, 1, 0)
    rd2 = pltpu.roll(cum, 2, 0)
    ru1 = pltpu.roll(cum, tt - 1, 0)
    m4 = rowi & 3
    bounds = {
        1: jnp.where((rowi & 1) == 0, cum, rd1),
        2: jnp.where(m4 == 0, ru1, jnp.where(m4 == 1, cum, jnp.where(m4 == 2, rd1, rd2))),
        4: boundary(4, 2 * 4),
    }
    low = [(q.astype(BF16), k.astype(BF16), 0)]
    for b in (1, 2, 4):
        second = (rowi & (2 * b - 1)) >= b
        c = bounds[b]
        qs = q * jnp.exp(jnp.where(second, cum - c, NEG))
        ks = k * jnp.exp(jnp.where(second, NEG, c - cum))
        low.append((qs.astype(BF16), ks.astype(BF16), int(math.log2(2 * b))))

    mid = []
    b = SUBLANE
    while b <= hf:
        ng = tt // (2 * b)
        c = boundary(b, b)
        firsts = [gi * 2 * b for gi in range(ng)]
        seconds = [gi * 2 * b + b for gi in range(ng)]
        qs2 = _rows(q, seconds, b) * jnp.exp(_rows(cum, seconds, b) - c)
        ks1 = _rows(k, firsts, b) * jnp.exp(c - _rows(cum, firsts, b))
        if b < hf:
            zb = jnp.zeros((b, GLA_QKP), F32)
            ks1 = jnp.concatenate([piece for gi in range(ng) for piece in (ks1[gi * b:(gi + 1) * b], zb)], axis=0)
        mid.append((b, qs2.astype(BF16), ks1.astype(BF16)))
        b *= 2

    st = st_scr[...]
    o_inter = _dot_nt(q * jnp.exp(cum), st)
    last = cum_scr[pl.ds(tt - 1, 1), :]
    kd = k * jnp.exp(last - cum)
    v_all = h_scr[:, C_GV:C_GV + GLA_VP]
    upd = jnp.dot(v_all.T.astype(BF16), kd.astype(BF16), preferred_element_type=F32)
    vrow = lax.broadcasted_iota(jnp.int32, (GLA_VP, 1), 0) // GLA_DVP
    lane = lax.broadcasted_iota(jnp.int32, (1, GLA_QKP), 1)
    khead = ((lane >= GLA_DK).astype(jnp.int32) + (lane >= 2 * GLA_DK).astype(jnp.int32)
             + (lane >= 3 * GLA_DK).astype(jnp.int32) + (lane >= 4 * GLA_DK).astype(jnp.int32))
    st_new = jnp.where(vrow == khead, st * jnp.exp(last) + upd, 0.0)
    st_scr[...] = st_new
    st_out[...] = st_new

    nh = GLA_HEADS
    hq = hf // 2
    c128 = lax.broadcasted_iota(jnp.int32, (1, hf), 1)
    r_low = lax.broadcasted_iota(jnp.int32, (nh * hf, 1), 0) & (hf - 1)
    r_mid = lax.broadcasted_iota(jnp.int32, (nh * hq, 1), 0) & (hq - 1)
    low_masks = [(r_low >> s) == (c128 >> s) for (_, _, s) in low]
    mid_masks = {bb: (r_mid >> int(math.log2(bb))) == (c128 >> int(math.log2(2 * bb)))
                 for (bb, _, _) in mid if 2 * bb < hf}
    nt = (((1,), (1,)), ((), ()))
    hms = [_lane_head_mask(GLA_QKP, GLA_DK, h, BF16) for h in range(nh)]
    stack_heads = lambda xb: jnp.concatenate([xb * hm for hm in hms], axis=0)
    for half in range(2):
        rs = slice(half * hf, (half + 1) * hf)
        acc = None
        for (qb, kb, _), mk in zip(low, low_masks):
            p = lax.dot_general(stack_heads(qb[rs]), kb[rs], nt, preferred_element_type=F32)
            p = jnp.where(mk, p, 0.0)
            acc = p if acc is None else acc + p
        for h in range(nh):
            d_scr[h, half] = acc[h * hf:(h + 1) * hf]
        for bb, qb2, kb1 in mid:
            if bb == hf:
                continue
            cs = slice(half * hq, (half + 1) * hq)
            p = lax.dot_general(stack_heads(qb2[cs]), kb1[rs], nt, preferred_element_type=F32)
            if bb in mid_masks:
                p = jnp.where(mid_masks[bb], p, 0.0)
            for h in range(nh):
                for gi in range(hf // (2 * bb)):
                    dst = pl.ds(gi * 2 * bb + bb, bb)
                    src = slice(h * hq + gi * bb, h * hq + (gi + 1) * bb)
                    d_scr[h, half, dst, :] = d_scr[h, half, dst, :] + p[src, :]
    _, qb2, kb1 = mid[-1]
    off_all = lax.dot_general(stack_heads(qb2), kb1, nt, preferred_element_type=F32)
    gout = gout_ref[...]
    for h in range(nh):
        off = off_all[h * hf:(h + 1) * hf]
        sl = slice(h * GLA_DVP, (h + 1) * GLA_DVP)
        v_h = v_all[:, sl].astype(BF16)
        o0 = jnp.dot(d_scr[h, 0].astype(BF16), v_h[0:hf], preferred_element_type=F32)
        a1 = jnp.concatenate([off, d_scr[h, 1]], axis=1).astype(BF16)
        o1 = jnp.dot(a1, v_h, preferred_element_type=F32)
        o = jnp.concatenate([o0, o1], axis=0) + o_inter[:, sl]
        ms = jnp.sum(o * o, axis=-1, keepdims=True) * (1.0 / GLA_DV)
        gate = h_scr[:, C_GOG + h * GLA_DVP:C_GOG + (h + 1) * GLA_DVP]
        oc_scr[:, sl] = ((o * lax.rsqrt(ms + EPS)) * gout) * (gate * _sigmoid(gate))


def _rg_gates(xc, wa_ref, ba, wx_ref, bx, lam):
    r = _sigmoid(jnp.dot(xc.astype(BF16), wa_ref[...], preferred_element_type=F32) + ba)
    i = _sigmoid(jnp.dot(xc.astype(BF16), wx_ref[...], preferred_element_type=F32) + bx)
    log_a = (-RG_C * r) * _softplus(-lam)
    a = jnp.exp(log_a)
    b = jnp.sqrt(1.0 - jnp.exp(2.0 * log_a)) * (i * xc)
    return a, b


def _rg_pitch(tt):
    return tt // SUBLANE + SUBLANE


def _rg_tiles(h_scr, cw_ref, cb_ref, wa_ref, ba_ref, wx_ref, bx_ref, lam_ref, hc_scr, cc_scr, ra_scr, rb_scr, rh_scr,
              rp_scr, oc_scr):
    nb, tt = len(h_scr), h_scr[0].shape[0]
    rowi = lax.broadcasted_iota(jnp.int32, (tt, 1), 0)
    nl = RG_WIDTH // LANE
    seg = tt // SUBLANE
    pitch = _rg_pitch(tt)
    cw = cw_ref[...]
    for bi in range(nb):
        rx = h_scr[bi][:,C_RX:C_RX + RG_WIDTH]
        p0, p1, p2 = cc_scr[bi, 0:1, :], cc_scr[bi, 1:2, :], cc_scr[bi, 2:3, :]
        x1 = jnp.where(rowi == 0, p2, pltpu.roll(rx, 1, 0))
        x2 = jnp.where(rowi == 0, p1, jnp.where(rowi == 1, p2, pltpu.roll(rx, 2, 0)))
        x3 = jnp.where(rowi == 0, p0, jnp.where(rowi == 1, p1, jnp.where(rowi == 2, p2, pltpu.roll(rx, 3, 0))))
        xc = cb_ref[...] + x3 * cw[0:1, :]
        xc = xc + x2 * cw[1:2, :]
        xc = xc + x1 * cw[2:3, :]
        xc = xc + rx * cw[3:4, :]
        a, b = _rg_gates(xc, wa_ref, ba_ref[...], wx_ref, bx_ref[...], lam_ref[...])
        for c in range(nl):
            for j in range(SUBLANE):
                dst = pl.ds(j * pitch, seg)
                ra_scr[bi, c, dst, :] = a[j * seg:(j + 1) * seg, c * LANE:(c + 1) * LANE]
                rb_scr[bi, c, dst, :] = b[j * seg:(j + 1) * seg, c * LANE:(c + 1) * LANE]

    chains = [(bi, c) for bi in range(nb) for c in range(nl)]
    hloc = {ch: jnp.zeros((SUBLANE, LANE), F32) for ch in chains}
    pc = {ch: jnp.ones((SUBLANE, LANE), F32) for ch in chains}
    for i in range(seg):
        idx = pl.ds(i, SUBLANE, stride=pitch)
        for ch in chains:
            bi, c = ch
            ai = ra_scr[bi, c, idx, :]
            hloc[ch] = ai * hloc[ch] + rb_scr[bi, c, idx, :]
            pc[ch] = ai * pc[ch]
            rh_scr[bi, c, idx, :] = hloc[ch]
            rp_scr[bi, c, idx, :] = pc[ch]
    outs = []
    for bi in range(nb):
        hl = jnp.concatenate([hloc[(bi, c)] for c in range(nl)], axis=1)
        pl_ = jnp.concatenate([pc[(bi, c)] for c in range(nl)], axis=1)
        hprev = hc_scr[bi, 0:1, :]
        carries = []
        for j in range(SUBLANE):
            carries.append(jnp.broadcast_to(hprev, (seg, RG_WIDTH)))
            hprev = hl[j:j + 1, :] + pl_[j:j + 1, :] * hprev
        hc_scr[bi, 0:1, :] = hprev
        carry = jnp.concatenate(carries, axis=0)
        unpitch = lambda r, c: jnp.concatenate([r[bi, c, pl.ds(j * pitch, seg), :] for j in range(SUBLANE)], axis=0)
        h = jnp.concatenate(
            [unpitch(rh_scr, c) + unpitch(rp_scr, c) * carry[:, c * LANE:(c + 1) * LANE] for c in range(nl)], axis=1)
        oc_scr[bi][:, GLA_VP:GLA_VP + RG_WIDTH] = h * _gelu_tanh(h_scr[bi][:,C_RY:C_RY + RG_WIDTH])
        tail = h_scr[bi][pl.ds(tt - (CONV_W - 1), CONV_W - 1), C_RX:C_RX + RG_WIDTH]
        cc_scr[bi, 0:CONV_W - 1, :] = tail
        outs.append((hprev, tail))
    return outs


def _stack_heads(x, masks):
    return jnp.concatenate([x * m for m in masks], axis=0)


def _unstack_heads_t(ot, seg, n):
    heads = ot.shape[0] // seg
    blocks = [ot[h * seg:(h + 1) * seg, h * n:(h + 1) * n] for h in range(heads)]
    return jnp.concatenate(blocks, axis=0).T


def _softmax_cols(s, sink=None):
    m = jnp.max(s, axis=0, keepdims=True)
    if sink is not None:
        m = jnp.maximum(m, sink)
    e = jnp.exp(s - m)
    den = jnp.sum(e, axis=0, keepdims=True)
    if sink is not None:
        den = den + jnp.exp(sink - m)
    return e * (1.0 / den)


def _swa_tiles(h_scr, sink_ref, layer, first, gq_ref, gk_ref, b256_ref, b128_ref, dup_ref, dupt_ref, kp_scr, vpt_scr,
               oc_scr):
    nb, tt = len(h_scr), h_scr[0].shape[0]
    w = WINDOW
    nh = SWA_HEADS
    hmb = [_lane_head_mask(SWA_Q, HEAD_DIM, h, BF16) for h in range(nh)]
    kj = lax.broadcasted_iota(jnp.int32, (2 * w, w), 0)
    qi = lax.broadcasted_iota(jnp.int32, (2 * w, w), 1)
    dist_i = qi + w - kj
    in_window = (dist_i >= 0) & (dist_i <= WINDOW)
    dist = dist_i.astype(F32)
    bias = jnp.concatenate([jnp.where(in_window, -_alibi_slope(h) * dist, NEG) for h in range(nh)], axis=1)
    kj4 = lax.broadcasted_iota(jnp.int32, (2 * w, nh * w), 0)
    bias_first = jnp.where(jnp.logical_and(first, kj4 < w), NEG, bias)
    sink_row = jnp.concatenate([jnp.full((1, w), sink_ref[layer, h], F32) for h in range(nh)], axis=1)
    dup = dup_ref[...]
    dupt = dupt_ref[...]
    nt = (((1,), (1,)), ((), ()))
    s_parts, vt_parts, wins = [], [], []
    for bi in range(nb):
        qn = _seg_rms(h_scr[bi][:, C_SQ:C_SQ + SWA_Q], b256_ref[...], HEAD_DIM, gq_ref[...])
        qn = (qn * (HEAD_DIM ** -0.5)).astype(BF16)
        kn = _seg_rms(h_scr[bi][:, C_SK:C_SK + SWA_KVW], b128_ref[...], HEAD_DIM, gk_ref[...])
        v = h_scr[bi][:, C_SV:C_SV + SWA_KVW]
        vt = v.T
        kcat = jnp.concatenate([kp_scr[bi], kn], axis=0).astype(BF16)
        vtcat = jnp.concatenate([vpt_scr[bi], vt], axis=1).astype(BF16)
        kexp = jnp.dot(kcat, dup, preferred_element_type=F32).astype(BF16)
        vexpt = jnp.dot(dupt, vtcat, preferred_element_type=F32).astype(BF16)
        for blk in range(tt // w):
            q4 = _stack_heads(qn[blk * w:(blk + 1) * w], hmb)
            s = lax.dot_general(kexp[blk * w:(blk + 2) * w], q4, nt, preferred_element_type=F32)
            s_parts.append(s + (bias_first if blk == 0 else bias))
            vt_parts.append(vexpt[:, blk * w:(blk + 2) * w])
        k_win = kn[tt - w:tt]
        kp_scr[bi] = k_win
        vpt_scr[bi] = vt[:, tt - w:tt]
        wins.append((k_win, v[tt - w:tt]))
    p_all = _softmax_cols(jnp.concatenate(s_parts, axis=1),
                          jnp.concatenate([sink_row] * len(s_parts), axis=1)).astype(BF16)
    n4 = nh * w
    for i, vtb in enumerate(vt_parts):
        bi, blk = divmod(i, tt // w)
        ot = jnp.dot(vtb, p_all[:, i * n4:(i + 1) * n4], preferred_element_type=F32)
        oc_scr[bi][blk * w:(blk + 1) * w, GLA_VP + RG_WIDTH:MIX_W] = _unstack_heads_t(ot, HEAD_DIM, w)
    return wins


def _mem_attend_tiles(x1s, gx, wq_ref, gq, bones, mk_ref, mvt_ref, wmo_ref):
    nh = MEM_HEADS
    nb, tt = len(x1s), x1s[0].shape[0]
    hmb = [_lane_head_mask(MEM_W, MEM_HD, h, BF16) for h in range(nh)]
    nt = (((1,), (1,)), ((), ()))
    xn = jnp.concatenate([_rms(x1, gx).astype(BF16) for x1 in x1s], axis=0)
    q = jnp.dot(xn, wq_ref[...], preferred_element_type=F32)
    qn = (_seg_rms(q, bones, MEM_HD, gq) * (MEM_HD ** -0.5)).astype(BF16)
    s_parts = [lax.dot_general(mk_ref[bi].astype(BF16), _stack_heads(qn[bi * tt:(bi + 1) * tt], hmb), nt,
                               preferred_element_type=F32) for bi in range(nb)]
    p_all = _softmax_cols(jnp.concatenate(s_parts, axis=1)).astype(BF16)
    n4 = nh * tt
    o = jnp.concatenate(
        [_unstack_heads_t(jnp.dot(mvt_ref[bi].astype(BF16), p_all[:, bi * n4:(bi + 1) * n4],
                                  preferred_element_type=F32), MEM_HD, tt) for bi in range(nb)], axis=0)
    y = jnp.dot(o.astype(BF16), wmo_ref[...], preferred_element_type=F32)
    return [x1s[bi] + y[bi * tt:(bi + 1) * tt] for bi in range(nb)]


def _mixer_body(sink_ref, x_ref, gmix_ref, win_ref, wg_ref, bg_ref, gout_ref, cw_ref, cb_ref, wa_ref,
                ba_ref, wx_ref, bx_ref, lam_ref, gq_ref, gk_ref, b256_ref, b128_ref, dup_ref, dupt_ref, wo_ref,
                gmx_ref, wq_ref, mgq_ref, mk_ref, mv_ref, wmo_ref,
                o_ref, st_ref, hl_ref, cbuf_ref, ko_ref, vo_ref,
                h_scr, oc_scr, st_scr, cum_scr, d_scr, hc_scr, cc_scr, ra_scr, rb_scr, rh_scr, rp_scr, kp_scr, vp_scr,
                *, layer):
    nb, tt = x_ref.shape[0], x_ref.shape[1]
    first = pl.program_id(0) == 0

    @pl.when(first)
    def _():
        for scr in (st_scr, hc_scr, cc_scr, kp_scr, vp_scr):
            scr[...] = jnp.zeros_like(scr)

    xn = jnp.concatenate([_rms(x_ref[bi], gmix_ref[...]).astype(BF16) for bi in range(nb)], axis=0)
    h_scr[...] = jnp.dot(xn, win_ref[...], preferred_element_type=F32)
    hs = [h_scr.at[pl.ds(bi * tt, tt)] for bi in range(nb)]
    ocs = [oc_scr.at[pl.ds(bi * tt, tt)] for bi in range(nb)]
    for bi in range(nb):
        _gla_tile(hs[bi], wg_ref, bg_ref, gout_ref, st_scr.at[bi], cum_scr.at[bi], d_scr.at[bi], ocs[bi],
                  st_ref.at[bi])
    rg_out = _rg_tiles(hs, cw_ref, cb_ref, wa_ref, ba_ref, wx_ref, bx_ref, lam_ref, hc_scr, cc_scr, ra_scr, rb_scr,
                       rh_scr, rp_scr, ocs)
    for bi, (h_last, tail) in enumerate(rg_out):
        hl_ref[bi] = h_last
        cbuf_ref[bi] = tail
    wins = _swa_tiles(hs, sink_ref, layer, first, gq_ref, gk_ref, b256_ref, b128_ref, dup_ref, dupt_ref, kp_scr,
                      vp_scr, ocs)
    for bi, (k_win, v_win) in enumerate(wins):
        ko_ref[bi] = k_win
        vo_ref[bi] = v_win
    y = jnp.dot(oc_scr[...].astype(BF16), wo_ref[...], preferred_element_type=F32)
    x1s = [x_ref[bi] + y[bi * tt:(bi + 1) * tt] for bi in range(nb)]
    x2s = _mem_attend_tiles(x1s, gmx_ref[...], wq_ref, mgq_ref[...], b256_ref[...], mk_ref, mv_ref, wmo_ref)
    for bi in range(nb):
        o_ref[bi] = x2s[bi]


def _mixer_prompt(sinks, x3d, gmix3, w_in_p, wg, bg3, gout3, cw, cb3, wa, ba3, wx, bx3, lam3, gq3, gk3, b256, b128,
                  dup, dupt, wo, gmx3, wq, mgq3, mk, mvt, wmo, layer):
    nb, t_len, _ = x3d.shape
    tt = MIX_T
    nt = t_len // tt
    n_mem = mk.shape[2]
    lay3 = lambda t: (layer, 0, 0)
    once = dict(pipeline_mode=pl.Buffered(1))
    vec = lambda w: pl.BlockSpec((None, 1, w), lay3)
    full = lambda a: pl.BlockSpec(a.shape, lambda t: (0,) * a.ndim)
    state = lambda r, w: pl.BlockSpec((nb, r, w), lambda t: (0, 0, 0))
    mem = pl.BlockSpec((None, nb, n_mem, MEM_W), lambda t: (layer, 0, 0, 0))
    memt = pl.BlockSpec((None, nb, MEM_W, n_mem), lambda t: (layer, 0, 0, 0))
    return pl.pallas_call(
        functools.partial(_mixer_body, layer=layer),
        grid=(nt,),
        in_specs=[
            pl.BlockSpec(memory_space=pltpu.SMEM),
            pl.BlockSpec((nb, tt, D_MODEL), lambda t: (0, t, 0)),
            vec(D_MODEL),
            pl.BlockSpec((None, D_MODEL, NP), lay3, **once),
            pl.BlockSpec((None, GLA_QKP, GLA_QKP), lay3), vec(GLA_QKP), vec(GLA_DVP),
            pl.BlockSpec((None, CONV_W, RG_WIDTH), lay3), vec(RG_WIDTH),
            pl.BlockSpec((None, RG_WIDTH, RG_WIDTH), lay3), vec(RG_WIDTH),
            pl.BlockSpec((None, RG_WIDTH, RG_WIDTH), lay3), vec(RG_WIDTH), vec(RG_WIDTH),
            vec(SWA_Q), vec(SWA_KVW), full(b256), full(b128), full(dup), full(dupt),
            pl.BlockSpec((None, MIX_W, D_MODEL), lay3, **once),
            vec(D_MODEL), pl.BlockSpec((None, D_MODEL, MEM_W), lay3), vec(MEM_W),
            mem, memt,
            pl.BlockSpec((None, MEM_W, D_MODEL), lay3),
        ],
        out_specs=[
            pl.BlockSpec((nb, tt, D_MODEL), lambda t: (0, t, 0)),
            state(GLA_VP, GLA_QKP), state(1, RG_WIDTH), state(CONV_W - 1, RG_WIDTH),
            state(WINDOW, SWA_KVW), state(WINDOW, SWA_KVW),
        ],
        out_shape=[
            jax.ShapeDtypeStruct((nb, t_len, D_MODEL), F32),
            jax.ShapeDtypeStruct((nb, GLA_VP, GLA_QKP), F32),
            jax.ShapeDtypeStruct((nb, 1, RG_WIDTH), F32),
            jax.ShapeDtypeStruct((nb, CONV_W - 1, RG_WIDTH), F32),
            jax.ShapeDtypeStruct((nb, WINDOW, SWA_KVW), F32),
            jax.ShapeDtypeStruct((nb, WINDOW, SWA_KVW), F32),
        ],
        scratch_shapes=[
            pltpu.VMEM((nb * tt, NP), F32), pltpu.VMEM((nb * tt, MIX_W), F32),
            pltpu.VMEM((nb, GLA_VP, GLA_QKP), F32), pltpu.VMEM((nb, tt, GLA_QKP), F32),
            pltpu.VMEM((nb, GLA_HEADS, 2, tt // 2, tt // 2), F32),
            pltpu.VMEM((nb, SUBLANE, RG_WIDTH), F32), pltpu.VMEM((nb, SUBLANE, RG_WIDTH), F32),
            *[pltpu.VMEM((nb, RG_WIDTH // LANE, SUBLANE * _rg_pitch(tt), LANE), F32) for _ in range(4)],
            pltpu.VMEM((nb, WINDOW, SWA_KVW), F32), pltpu.VMEM((nb, WINDOW, SWA_KVW), F32),
        ],
        compiler_params=_cparams("arbitrary"),
        name="mixer_prompt",
    )(sinks, x3d, gmix3, w_in_p, wg, bg3, gout3, cw, cb3, wa, ba3, wx, bx3, lam3, gq3, gk3, b256, b128, dup, dupt,
      wo, gmx3, wq, mgq3, mk, mvt, wmo)


def _sample_rows_body(h_ref, hst_ref, cbuf_ref, cw_ref, cb_ref, wa_ref, ba_ref, wx_ref, bx_ref, lam_ref, gq_ref,
                      gk_ref, b256_ref, b128_ref, dup_ref,
                      org_ref, hn_ref, cnew_ref, qn_ref, kx_ref, vx_ref, knt_ref, vnt_ref):
    rx = h_ref[:, C_RX:C_RX + RG_WIDTH]
    cw = cw_ref[...]
    xc = cb_ref[...] + cbuf_ref[0] * cw[0:1, :]
    xc = xc + cbuf_ref[1] * cw[1:2, :]
    xc = xc + cbuf_ref[2] * cw[2:3, :]
    xc = xc + rx * cw[3:4, :]
    a, b = _rg_gates(xc, wa_ref, ba_ref[...], wx_ref, bx_ref[...], lam_ref[...])
    hn = a * hst_ref[...] + b
    hn_ref[...] = hn
    org_ref[...] = hn * _gelu_tanh(h_ref[:, C_RY:C_RY + RG_WIDTH])
    cnew_ref[0] = cbuf_ref[1]
    cnew_ref[1] = cbuf_ref[2]
    cnew_ref[2] = rx

    qn_ref[...] = _seg_rms(h_ref[:, C_SQ:C_SQ + SWA_Q], b256_ref[...], HEAD_DIM, gq_ref[...])
    kn = _seg_rms(h_ref[:, C_SK:C_SK + SWA_KVW], b128_ref[...], HEAD_DIM, gk_ref[...])
    v = h_ref[:, C_SV:C_SV + SWA_KVW]
    kx_ref[...] = jnp.dot(kn.astype(BF16), dup_ref[...], preferred_element_type=F32)
    vx_ref[...] = jnp.dot(v.astype(BF16), dup_ref[...], preferred_element_type=F32)
    knt = kn.T
    vnt = v.T
    nblk, _, bt = knt_ref.shape
    for i in range(nblk):
        knt_ref[i] = knt[:, i * bt:(i + 1) * bt]
        vnt_ref[i] = vnt[:, i * bt:(i + 1) * bt]


def _sample_rows(hs, hst, cbuf, cw, cb3, wa, ba3, wx, bx3, lam3, gq3, gk3, b256, b128, dup, layer, bt):
    n = hs.shape[0]
    lay3 = lambda i: (layer, 0, 0)
    full = lambda a: pl.BlockSpec(a.shape, lambda i: (0,) * a.ndim)
    vec = lambda w: pl.BlockSpec((None, 1, w), lay3)
    sq = lambda w: pl.BlockSpec((None, w, w), lay3)
    taps = pl.BlockSpec((None, CONV_W - 1, n, RG_WIDTH), lambda i: (layer, 0, 0, 0))
    shapes = [(n, RG_WIDTH), (n, RG_WIDTH), (CONV_W - 1, n, RG_WIDTH), (n, SWA_Q), (n, SWA_Q), (n, SWA_Q),
              (n // bt, SWA_KVW, bt), (n // bt, SWA_KVW, bt)]
    return pl.pallas_call(
        _sample_rows_body,
        grid=(1,),
        in_specs=[
            full(hs), pl.BlockSpec((None, n, RG_WIDTH), lay3), taps,
            pl.BlockSpec((None, CONV_W, RG_WIDTH), lay3), vec(RG_WIDTH),
            sq(RG_WIDTH), vec(RG_WIDTH), sq(RG_WIDTH), vec(RG_WIDTH), vec(RG_WIDTH),
            vec(SWA_Q), vec(SWA_KVW), full(b256), full(b128), full(dup),
        ],
        out_specs=[pl.BlockSpec(s, lambda i, nd=len(s): (0,) * nd) for s in shapes],
        out_shape=[jax.ShapeDtypeStruct(s, F32) for s in shapes],
        compiler_params=_cparams("arbitrary"),
        name="sample_rows",
    )(hs, hst, cbuf, cw, cb3, wa, ba3, wx, bx3, lam3, gq3, gk3, b256, b128, dup)


def _sample_gla_body(s0_ref, h_ref, wg_ref, bg_ref, goutc_ref, o_ref, sn_ref, qt_scr, kt_scr, et_scr, vt_scr, gt_scr):
    h = pl.program_id(0)
    qraw = h_ref[:, C_GQ:C_GQ + GLA_QKP]
    z = jnp.dot(qraw.astype(BF16), wg_ref[...], preferred_element_type=F32) + bg_ref[...]
    et_scr[...] = jnp.exp(_log_sigmoid(z) * (1.0 / GLA_TAU)).T
    qt_scr[...] = (qraw * (GLA_DK ** -0.5)).T
    kt_scr[...] = h_ref[:, C_GK:C_GK + GLA_QKP].T
    vt_scr[...] = h_ref[:, C_GV:C_GV + GLA_VP].T
    gt_scr[...] = h_ref[:, C_GOG:C_GOG + GLA_VP].T
    r0 = pl.multiple_of(h * GLA_DK, SUBLANE)
    v0 = pl.multiple_of(h * GLA_DVP, GLA_DVP)
    vt = vt_scr[pl.ds(v0, GLA_DV), :]
    o = jnp.zeros(vt.shape, F32)
    for k in range(GLA_DK):
        sn = et_scr[pl.ds(r0 + k, 1), :] * s0_ref[k] + kt_scr[pl.ds(r0 + k, 1), :] * vt
        sn_ref[k] = sn
        o = o + qt_scr[pl.ds(r0 + k, 1), :] * sn
    ms = jnp.sum(o * o, axis=0, keepdims=True) * (1.0 / GLA_DV)
    gate = gt_scr[pl.ds(v0, GLA_DV), :]
    y = ((o * lax.rsqrt(ms + EPS)) * goutc_ref[...]) * (gate * _sigmoid(gate))
    y = jnp.concatenate([y, jnp.zeros((GLA_DVP - GLA_DV, y.shape[1]), F32)], axis=0)
    o_ref[...] = y.T


def _sample_gla(s0t, hs, wg, bg3, goutc, layer):
    n = hs.shape[0]
    lay3 = lambda h: (layer, 0, 0)
    return pl.pallas_call(
        _sample_gla_body,
        grid=(GLA_HEADS,),
        in_specs=[
            pl.BlockSpec((None, None, GLA_DK, GLA_DV, n), lambda h: (layer, h, 0, 0, 0)),
            pl.BlockSpec((n, NP), lambda h: (0, 0)),
            pl.BlockSpec((None, GLA_QKP, GLA_QKP), lay3),
            pl.BlockSpec((None, 1, GLA_QKP), lay3),
            pl.BlockSpec((None, GLA_DV, 1), lay3),
        ],
        out_specs=[pl.BlockSpec((n, GLA_DVP), lambda h: (0, h)),
                   pl.BlockSpec((None, None, GLA_DK, GLA_DV, n), lambda h: (layer, h, 0, 0, 0))],
        out_shape=[jax.ShapeDtypeStruct((n, GLA_VP), F32), jax.ShapeDtypeStruct(s0t.shape, F32)],
        scratch_shapes=[pltpu.VMEM((GLA_QKP, n), F32)] * 3 + [pltpu.VMEM((GLA_VP, n), F32)] * 2,
        input_output_aliases={0: 1},
        compiler_params=_cparams("arbitrary"),
        name="sample_gla",
    )(s0t, hs, wg, bg3, goutc)


def _sample_swa_body(sink_ref, kct_ref, vct_ref, qn_ref, kx_ref, vx_ref, knt_ref, vnt_ref, dupt_ref, o_ref, kot_ref,
                     vot_ref, *, layer):
    bt = kct_ref.shape[0]
    dupt = dupt_ref[...]
    hi = lax.broadcasted_iota(jnp.int32, (SWA_HEADS, SWA_Q), 0)
    li = lax.broadcasted_iota(jnp.int32, (SWA_HEADS, SWA_Q), 1)
    hm4 = ((li >= hi * HEAD_DIM) & (li < (hi + 1) * HEAD_DIM)).astype(F32)
    hrow = lax.broadcasted_iota(jnp.int32, (SWA_HEADS, 1), 0)
    slope = jnp.zeros((SWA_HEADS, 1), F32)
    sink = jnp.zeros((SWA_HEADS, 1), F32)
    for h in range(SWA_HEADS):
        slope = jnp.where(hrow == h, _alibi_slope(h), slope)
        sink = jnp.where(hrow == h, sink_ref[layer, h], sink)
    dist = (WINDOW - lax.broadcasted_iota(jnp.int32, (1, WINDOW), 1)).astype(F32)
    last = lax.broadcasted_iota(jnp.int32, (SWA_KVW, WINDOW), 1) == WINDOW - 1
    nt = (((1,), (1,)), ((), ()))
    for j in range(bt):
        kt = kct_ref[j]
        vt = vct_ref[j]
        kexp = jnp.dot(dupt, kt.astype(BF16), preferred_element_type=F32).astype(BF16)
        vexp = jnp.dot(dupt, vt.astype(BF16), preferred_element_type=F32).astype(BF16)
        q4 = (qn_ref[j:j + 1, :] * hm4).astype(BF16)
        s = jnp.dot(q4, kexp, preferred_element_type=F32)
        s = s * (HEAD_DIM ** -0.5) - slope * dist
        s_new = jnp.sum(q4.astype(F32) * kx_ref[j:j + 1, :], axis=-1, keepdims=True) * (HEAD_DIM ** -0.5)
        m = jnp.maximum(jnp.maximum(jnp.max(s, axis=-1, keepdims=True), s_new), sink)
        e = jnp.exp(s - m)
        e_new = jnp.exp(s_new - m)
        den = jnp.sum(e, axis=-1, keepdims=True) + e_new + jnp.exp(sink - m)
        o4 = lax.dot_general((e / den).astype(BF16), vexp, nt, preferred_element_type=F32)
        o4 = o4 + (e_new / den).astype(BF16).astype(F32) * vx_ref[j:j + 1, :]
        o_ref[j:j + 1, :] = jnp.sum(o4 * hm4, axis=0, keepdims=True)
        kot_ref[j] = jnp.where(last, knt_ref[:, j:j + 1], pltpu.roll(kt, WINDOW - 1, 1))
        vot_ref[j] = jnp.where(last, vnt_ref[:, j:j + 1], pltpu.roll(vt, WINDOW - 1, 1))


def _sample_swa(sinks, kct, vct, qn, kx, vx, knt, vnt, dupt, layer, bt):
    n = qn.shape[0]
    row = lambda w: pl.BlockSpec((bt, w), lambda i: (i, 0))
    cache = pl.BlockSpec((None, bt, SWA_KVW, WINDOW), lambda i: (layer, i, 0, 0))
    col = pl.BlockSpec((None, SWA_KVW, bt), lambda i: (i, 0, 0))
    return pl.pallas_call(
        functools.partial(_sample_swa_body, layer=layer),
        grid=(n // bt,),
        in_specs=[
            pl.BlockSpec(memory_space=pltpu.SMEM), cache, cache,
            row(SWA_Q), row(SWA_Q), row(SWA_Q), col, col,
            pl.BlockSpec(dupt.shape, lambda i: (0, 0)),
        ],
        out_specs=[row(SWA_Q), cache, cache],
        out_shape=[jax.ShapeDtypeStruct((n, SWA_Q), F32), jax.ShapeDtypeStruct(kct.shape, F32),
                   jax.ShapeDtypeStruct(vct.shape, F32)],
        input_output_aliases={1: 1, 2: 2},
        compiler_params=_cparams("parallel"),
        name="sample_swa",
    )(sinks, kct, vct, qn, kx, vx, knt, vnt, dupt)


def _sample_post1_body(x_ref, og_ref, or_ref, os_ref, wo_ref, gx_ref, wq_ref, gq_ref, bones_ref, x1_ref, qn_ref):
    x1 = _mix_out(x_ref[...], og_ref[...], or_ref[...], os_ref[...], wo_ref)
    x1_ref[...] = x1
    qn_ref[...] = _mem_q(x1, gx_ref[...], wq_ref, gq_ref[...], bones_ref[...])


def _sample_post1(x2d, og, orr, osw, wo, gx3, wq, gq3, bones256, layer):
    n = x2d.shape[0]
    lay3 = lambda i: (layer, 0, 0)
    full = lambda a: pl.BlockSpec(a.shape, lambda i: (0,) * a.ndim)
    return pl.pallas_call(
        _sample_post1_body,
        grid=(1,),
        in_specs=[
            full(x2d), full(og), full(orr), full(osw),
            pl.BlockSpec((None, MIX_W, D_MODEL), lay3),
            pl.BlockSpec((None, 1, D_MODEL), lay3),
            pl.BlockSpec((None, D_MODEL, MEM_W), lay3),
            pl.BlockSpec((None, 1, MEM_W), lay3),
            full(bones256),
        ],
        out_specs=[pl.BlockSpec((n, D_MODEL), lambda i: (0, 0)), pl.BlockSpec((n, MEM_W), lambda i: (0, 0))],
        out_shape=[jax.ShapeDtypeStruct((n, D_MODEL), F32), jax.ShapeDtypeStruct((n, MEM_W), F32)],
        compiler_params=_cparams("arbitrary"),
        name="sample_post1",
    )(x2d, og, orr, osw, wo, gx3, wq, gq3, bones256)


def _sample_mem_body(qn_ref, mk_ref, mv_ref, o_ref):
    bt = mk_ref.shape[0]
    hi = lax.broadcasted_iota(jnp.int32, (MEM_HEADS, MEM_W), 0)
    li = lax.broadcasted_iota(jnp.int32, (MEM_HEADS, MEM_W), 1)
    hm4 = ((li >= hi * MEM_HD) & (li < (hi + 1) * MEM_HD)).astype(F32)
    for j in range(bt):
        q4 = (qn_ref[j:j + 1, :] * hm4).astype(BF16)
        s = jnp.dot(q4, mk_ref[j].astype(BF16), preferred_element_type=F32)
        p = _softmax_rows(s * (MEM_HD ** -0.5))
        o4 = lax.dot_general(p.astype(BF16), mv_ref[j].astype(BF16), (((1,), (1,)), ((), ())),
                             preferred_element_type=F32)
        o_ref[j:j + 1, :] = jnp.sum(o4 * hm4, axis=0, keepdims=True)


def _sample_mem(qn, mk, mv, layer, bt):
    n = qn.shape[0]
    n_mem = mk.shape[3]
    cache = pl.BlockSpec((None, bt, MEM_W, n_mem), lambda i: (layer, i, 0, 0))
    return pl.pallas_call(
        _sample_mem_body,
        grid=(n // bt,),
        in_specs=[pl.BlockSpec((bt, MEM_W), lambda i: (i, 0)), cache, cache],
        out_specs=pl.BlockSpec((bt, MEM_W), lambda i: (i, 0)),
        out_shape=jax.ShapeDtypeStruct((n, MEM_W), F32),
        compiler_params=_cparams("parallel"),
        name="sample_mem",
    )(qn, mk, mv)


def _ffn_res_body(x_ref, o_ref_in, wmo_ref, g_ref, w1_ref, w2_ref, out_ref):
    x = x_ref[...] + jnp.dot(o_ref_in[...].astype(BF16), wmo_ref[...], preferred_element_type=F32)
    xn = _rms(x, g_ref[...]).astype(BF16)
    acc = x
    for c in range(D_FF // FF_CHUNK):
        h = jnp.dot(xn, w1_ref[:, c * FF_CHUNK:(c + 1) * FF_CHUNK], preferred_element_type=F32)
        h = jnp.square(jnp.maximum(h, 0.0))
        acc = acc + jnp.dot(h.astype(BF16), w2_ref[c * FF_CHUNK:(c + 1) * FF_CHUNK, :], preferred_element_type=F32)
    out_ref[...] = acc


def _sample_ffn(x1, o, wmo, g3, w1, w2, layer):
    n = x1.shape[0]
    lay3 = lambda i: (layer, 0, 0)
    const = dict(pipeline_mode=pl.Buffered(1))
    return pl.pallas_call(
        _ffn_res_body,
        grid=(1,),
        in_specs=[
            pl.BlockSpec((n, D_MODEL), lambda i: (0, 0)),
            pl.BlockSpec((n, MEM_W), lambda i: (0, 0)),
            pl.BlockSpec((None, MEM_W, D_MODEL), lay3),
            pl.BlockSpec((None, 1, D_MODEL), lay3),
            pl.BlockSpec((None, D_MODEL, D_FF), lay3, **const),
            pl.BlockSpec((None, D_FF, D_MODEL), lay3, **const),
        ],
        out_specs=pl.BlockSpec((n, D_MODEL), lambda i: (0, 0)),
        out_shape=jax.ShapeDtypeStruct((n, D_MODEL), F32),
        compiler_params=_cparams("arbitrary"),
        name="sample_ffn",
    )(x1, o, wmo, g3, w1, w2)


def _pad_heads(a, axis, heads, width, padded):
    shp = a.shape
    a = a.reshape(shp[:axis] + (heads, width) + shp[axis + 1:])
    pad = [(0, 0)] * a.ndim
    pad[axis + 1] = (0, padded - width)
    a = jnp.pad(a, pad)
    return a.reshape(shp[:axis] + (heads * padded,) + shp[axis + 1:])


def _w_in_pieces(w):
    offs = [0]
    for n in (GLA_QK, GLA_QK, GLA_V, GLA_RANK, GLA_V, RG_WIDTH, RG_WIDTH, SWA_Q, SWA_KVW, SWA_KVW):
        offs.append(offs[-1] + n)
    gq, gk, gv, glr, gog, rx, ry, sq, sk, sv = [w[..., offs[i]:offs[i + 1]] for i in range(10)]
    zeros = lambda n: jnp.zeros(w.shape[:-1] + (n,), w.dtype)

    def heads(a):
        out = []
        for h in range(GLA_HEADS):
            out += [a[..., h * GLA_DV:(h + 1) * GLA_DV], zeros(GLA_DVP - GLA_DV)]
        return out

    return ([rx, ry, gq, glr, zeros(GLA_QKP - GLA_QK - GLA_RANK), gk, zeros(GLA_QKP - GLA_QK), sq]
            + heads(gv) + heads(gog) + [sk, sv])


def _prep_w_in_body(w_ref, o_ref):
    o_ref[...] = jnp.concatenate(_w_in_pieces(w_ref[...]), axis=-1).astype(BF16)


def _prep_w_in(w_in):
    depth, d, width = w_in.shape
    tr = 256
    return pl.pallas_call(
        _prep_w_in_body,
        grid=(depth, d // tr),
        in_specs=[pl.BlockSpec((None, tr, width), lambda l, r: (l, r, 0))],
        out_specs=pl.BlockSpec((None, tr, NP), lambda l, r: (l, r, 0)),
        out_shape=jax.ShapeDtypeStruct((depth, d, NP), BF16),
        compiler_params=_cparams("parallel", "parallel"),
        name="prep_w_in",
    )(w_in)


def _block_diag(w):
    depth, nb, bw, _ = w.shape
    eye = jnp.eye(nb, dtype=w.dtype)
    return jnp.einsum("lncd,nm->lncmd", w, eye).reshape(depth, nb * bw, nb * bw)


def _block_ones(n, seg):
    i = jnp.arange(n) // seg
    return (i[:, None] == i[None, :]).astype(BF16)


def _dup_matrix():
    src = jnp.arange(SWA_Q)
    src = (src // HEAD_DIM // SWA_G) * HEAD_DIM + src % HEAD_DIM
    return (jnp.arange(SWA_KVW)[:, None] == src[None, :]).astype(BF16)


def kernel(x_prompt, x_sample, mem_prompt, state_gla, state_rg_h, state_rg_conv, cache_swa_k, cache_swa_v, cache_mem_k, cache_mem_v, g_mix, w_in, gla_w_gate2, gla_b_gate, gla_g_out, rg_conv_w, rg_conv_b, rg_w_a, rg_b_a, rg_w_x, rg_b_x, rg_lam, swa_g_q, swa_g_k, swa_sinks, w_out, g_mem_x, g_mem_m, mem_w_q, mem_w_kv, mem_g_q, mem_g_k, mem_w_o, g_ffn, ffn_w1, ffn_w2):
    depth = w_in.shape[0]
    nb, t_len, _ = x_prompt.shape
    ns = x_sample.shape[0]
    n_mem = mem_prompt.shape[1]
    assert t_len % ROW_T == 0 and t_len % MIX_T == 0 and x_sample.shape[1] == 1 and ns % 8 == 0
    row3 = lambda a: a.reshape(depth, 1, a.shape[-1])

    w_in_p = _prep_w_in(w_in)
    wg = jnp.zeros((depth, GLA_QKP, GLA_QKP), F32).at[:, GLR_LANE:GLR_LANE + GLA_RANK, :GLA_QK].set(gla_w_gate2)
    wg = wg.astype(BF16)
    bg3 = row3(jnp.pad(gla_b_gate, ((0, 0), (0, GLA_QKP - GLA_QK))))
    gout3 = row3(jnp.pad(gla_g_out, ((0, 0), (0, GLA_DVP - GLA_DV))))
    wa = _block_diag(rg_w_a).astype(BF16)
    wx = _block_diag(rg_w_x).astype(BF16)
    ba3, bx3, lam3, cb3 = row3(rg_b_a), row3(rg_b_x), row3(rg_lam), row3(rg_conv_b)
    gq3 = row3(jnp.tile(swa_g_q, (1, SWA_HEADS)))
    gk3 = row3(jnp.tile(swa_g_k, (1, SWA_KV)))
    wo = jnp.concatenate([_pad_heads(w_out[:, :GLA_V], 1, GLA_HEADS, GLA_DV, GLA_DVP), w_out[:, GLA_V:]], axis=1)
    wo = wo.astype(BF16)
    gmx3, gmm3, gffn3, gmix3 = row3(g_mem_x), row3(g_mem_m), row3(g_ffn), row3(g_mix)
    mgq3 = row3(jnp.tile(mem_g_q, (1, MEM_HEADS)))
    mgk3 = row3(jnp.tile(mem_g_k, (1, MEM_HEADS)))
    wq = mem_w_q.astype(BF16)
    wkv = mem_w_kv.astype(BF16)
    wmo = mem_w_o.astype(BF16)
    w1 = ffn_w1.astype(BF16)
    w2 = ffn_w2.astype(BF16)
    b256 = _block_ones(SWA_Q, HEAD_DIM)
    b128 = _block_ones(SWA_KVW, HEAD_DIM)
    dup = _dup_matrix()

    dupt = dup.T
    mk_all, mv_all, mvt_all = _mem_kv(mem_prompt.reshape(nb * n_mem, D_MODEL), gmm3, wkv, mgk3, b256, nb)
    mk_all = mk_all.reshape(depth, nb, n_mem, MEM_W)
    mv_all = mv_all.reshape(depth, nb, n_mem, MEM_W)

    xp = x_prompt.reshape(nb * t_len, D_MODEL)
    xs = x_sample.reshape(ns, D_MODEL)
    rg_conv_t = state_rg_conv.transpose(0, 2, 1, 3)
    gla_t = state_gla.transpose(0, 2, 3, 4, 1)
    swk = cache_swa_k.transpose(0, 1, 3, 4, 2).reshape(depth, ns, SWA_KVW, WINDOW)
    swv = cache_swa_v.transpose(0, 1, 3, 4, 2).reshape(depth, ns, SWA_KVW, WINDOW)
    cmk = cache_mem_k.transpose(0, 1, 3, 4, 2).reshape(depth, ns, MEM_W, n_mem)
    cmv = cache_mem_v.transpose(0, 1, 3, 4, 2).reshape(depth, ns, MEM_W, n_mem)
    goutc = gla_g_out.reshape(depth, GLA_DV, 1)
    sbt = min(ns, 32)

    gla_p, rgh_p, rgh_s, rgc_p, rgc_s, swk_p, swv_p = [], [], [], [], [], [], []
    ts = min(ns, 128)
    for l in range(depth):
        xp, st, hl, cbuf, kb, vb = _mixer_prompt(
            swa_sinks, xp.reshape(nb, t_len, D_MODEL), gmix3, w_in_p, wg, bg3, gout3, rg_conv_w, cb3, wa, ba3, wx,
            bx3, lam3, gq3, gk3, b256, b128, dup, dupt, wo, gmx3, wq, mgq3, mk_all, mvt_all, wmo, l)
        xp = _ffn(xp.reshape(nb * t_len, D_MODEL), gffn3, w1, w2, l, ROW_T)
        st = st.reshape(nb, GLA_HEADS, GLA_DVP, GLA_QKP)
        gla_p.append(jnp.stack(
            [st[:, h, :GLA_DV, h * GLA_DK:(h + 1) * GLA_DK].transpose(0, 2, 1) for h in range(GLA_HEADS)], axis=1))
        rgh_p.append(hl.reshape(nb, RG_WIDTH))
        rgc_p.append(cbuf)
        swk_p.append(kb.reshape(nb, WINDOW, SWA_KV, HEAD_DIM))
        swv_p.append(vb.reshape(nb, WINDOW, SWA_KV, HEAD_DIM))

        hs = _in_proj(xs, gmix3, w_in_p, l, ts)
        org_s, hn, cnew, qn, kx, vx, knt, vnt = _sample_rows(
            hs, state_rg_h, rg_conv_t, rg_conv_w, cb3, wa, ba3, wx, bx3, lam3, gq3, gk3, b256, b128, dup, l, sbt)
        og_s, gla_t = _sample_gla(gla_t, hs, wg, bg3, goutc, l)
        osw_s, swk, swv = _sample_swa(swa_sinks, swk, swv, qn, kx, vx, knt, vnt, dupt, l, sbt)
        x1, qm = _sample_post1(xs, og_s, org_s, osw_s, wo, gmx3, wq, mgq3, b256, l)
        om = _sample_mem(qm, cmk, cmv, l, 8)
        xs = _sample_ffn(x1, om, wmo, gffn3, w1, w2, l)
        rgh_s.append(hn)
        rgc_s.append(cnew)

    to_cache = lambda a: a.reshape(depth, ns, SWA_KV, HEAD_DIM, WINDOW).transpose(0, 1, 4, 2, 3)
    return (xp.reshape(nb, t_len, D_MODEL), xs.reshape(ns, 1, D_MODEL),
            jnp.stack(gla_p), gla_t.transpose(0, 4, 1, 2, 3), jnp.stack(rgh_p), jnp.stack(rgh_s),
            jnp.stack(rgc_p), jnp.stack(rgc_s).transpose(0, 2, 1, 3), jnp.stack(swk_p), to_cache(swk),
            jnp.stack(swv_p), to_cache(swv),
            mk_all.reshape(depth, nb, n_mem, MEM_HEADS, MEM_HD), mv_all.reshape(depth, nb, n_mem, MEM_HEADS, MEM_HD))
```

```python
import functools
import math

import jax
import jax.numpy as jnp
from jax import lax
from jax.experimental import pallas as pl
from jax.experimental.pallas import tpu as pltpu

F32 = jnp.float32
BF16 = jnp.bfloat16

D_MODEL = 1024
EPS = 1e-6
GLA_HEADS, GLA_DK, GLA_DV, GLA_RANK, GLA_TAU = 4, 48, 96, 16, 16.0
GLA_QK = GLA_HEADS * GLA_DK
GLA_V = GLA_HEADS * GLA_DV
RG_WIDTH, RG_BLOCKS, RG_C, CONV_W = 384, 4, 8.0, 4
RG_BW = RG_WIDTH // RG_BLOCKS
SWA_HEADS, SWA_KV, HEAD_DIM, WINDOW = 4, 2, 64, 128
SWA_G = SWA_HEADS // SWA_KV
SWA_Q = SWA_HEADS * HEAD_DIM
SWA_KVW = SWA_KV * HEAD_DIM
MEM_HEADS, MEM_HD = 4, 64
MEM_W = MEM_HEADS * MEM_HD
D_FF = 4 * D_MODEL

LANE = 128
SUBLANE = 8
GLA_DVP = LANE
GLA_VP = GLA_HEADS * GLA_DVP
GLA_QKP = 2 * LANE
MIX_W = GLA_VP + RG_WIDTH + SWA_Q

C_RX, C_RY, C_GQ, C_GK, C_SQ, C_GV, C_GOG, C_SK, C_SV = 0, 384, 768, 1024, 1280, 1536, 2048, 2560, 2688
NP = 2816
GLR_LANE = GLA_QK

VMEM_LIMIT = 56 * 1024 * 1024
NEG = -1e30

MIX_T = 256
ROW_T = 512
FF_CHUNK = 1024


def _cparams(*sem):
    return pltpu.CompilerParams(dimension_semantics=sem, vmem_limit_bytes=VMEM_LIMIT)


def _rms(x, g):
    ms = jnp.mean(x * x, axis=-1, keepdims=True)
    return (x * lax.rsqrt(ms + EPS)) * g


def _dot(a, b):
    return jnp.dot(a.astype(BF16), b.astype(BF16), preferred_element_type=F32)


def _dot_nt(a, b):
    return lax.dot_general(a.astype(BF16), b.astype(BF16), (((1,), (1,)), ((), ())), preferred_element_type=F32)


def _split2_dot(x, m):
    hi = x.astype(BF16)
    lo = (x - hi.astype(F32)).astype(BF16)
    return jnp.dot(hi, m, preferred_element_type=F32) + jnp.dot(lo, m, preferred_element_type=F32)


def _split3(x):
    x1 = x.astype(BF16)
    r1 = x - x1.astype(F32)
    x2 = r1.astype(BF16)
    x3 = (r1 - x2.astype(F32)).astype(BF16)
    return x1, x2, x3


def _split3_dot_left(m, x):
    d = functools.partial(jnp.dot, preferred_element_type=F32)
    x1, x2, x3 = _split3(x)
    return d(m, x1) + d(m, x2) + d(m, x3)


def _seg_rms(x, bones, seg, g):
    ms = _split2_dot(x * x, bones) * (1.0 / seg)
    return (x * lax.rsqrt(ms + EPS)) * g


def _sigmoid(x):
    return 1.0 / (1.0 + jnp.exp(-x))


def _softplus(x):
    return jnp.maximum(x, 0.0) + jnp.log1p(jnp.exp(-jnp.abs(x)))


def _log_sigmoid(z):
    return -(jnp.maximum(-z, 0.0) + jnp.log(1.0 + jnp.exp(-jnp.abs(z))))


def _gelu_tanh(x):
    return x * (0.5 * (1.0 + jnp.tanh(math.sqrt(2.0 / math.pi) * (x + 0.044715 * (x * x * x)))))


def _lane_head_mask(width, seg, h, dtype=F32):
    lane = lax.broadcasted_iota(jnp.int32, (1, width), 1)
    return ((lane >= seg * h) & (lane < seg * (h + 1))).astype(dtype)


def _masked_softmax_rows(s, sink):
    m = jnp.maximum(jnp.max(s, axis=-1, keepdims=True), sink)
    e = jnp.exp(s - m)
    den = jnp.sum(e, axis=-1, keepdims=True) + jnp.exp(sink - m)
    return e / den


def _softmax_rows(s):
    m = jnp.max(s, axis=-1, keepdims=True)
    e = jnp.exp(s - m)
    return e / jnp.sum(e, axis=-1, keepdims=True)


def _alibi_slope(h):
    return 2.0 ** (-8.0 * (h + 1) / SWA_HEADS)


def _rows(x, starts, size):
    parts = [x[s:s + size] for s in starts]
    return parts[0] if len(parts) == 1 else jnp.concatenate(parts, axis=0)


def _in_proj_body(x_ref, g_ref, w_ref, o_ref):
    xn = _rms(x_ref[...], g_ref[...])
    o_ref[...] = jnp.dot(xn.astype(BF16), w_ref[...], preferred_element_type=F32)


def _in_proj(x2d, g3, w3, layer, tm):
    m = x2d.shape[0]
    return pl.pallas_call(
        _in_proj_body,
        grid=(m // tm,),
        in_specs=[
            pl.BlockSpec((tm, D_MODEL), lambda i: (i, 0)),
            pl.BlockSpec((None, 1, D_MODEL), lambda i: (layer, 0, 0)),
            pl.BlockSpec((None, D_MODEL, NP), lambda i: (layer, 0, 0)),
        ],
        out_specs=pl.BlockSpec((tm, NP), lambda i: (i, 0)),
        out_shape=jax.ShapeDtypeStruct((m, NP), F32),
        compiler_params=_cparams("parallel"),
        name="in_proj",
    )(x2d, g3, w3)


def _ffn_body(x_ref, g_ref, w1_ref, w2_ref, o_ref):
    x = x_ref[...]
    xn = _rms(x, g_ref[...]).astype(BF16)
    acc = x
    for c in range(D_FF // FF_CHUNK):
        h = jnp.dot(xn, w1_ref[:, c * FF_CHUNK:(c + 1) * FF_CHUNK], preferred_element_type=F32)
        h = jnp.square(jnp.maximum(h, 0.0))
        acc = acc + jnp.dot(h.astype(BF16), w2_ref[c * FF_CHUNK:(c + 1) * FF_CHUNK, :], preferred_element_type=F32)
    o_ref[...] = acc


def _ffn(x2d, g3, w1, w2, layer, tm):
    m = x2d.shape[0]
    const = dict(pipeline_mode=pl.Buffered(1))
    return pl.pallas_call(
        _ffn_body,
        grid=(m // tm,),
        in_specs=[
            pl.BlockSpec((tm, D_MODEL), lambda i: (i, 0)),
            pl.BlockSpec((None, 1, D_MODEL), lambda i: (layer, 0, 0)),
            pl.BlockSpec((None, D_MODEL, D_FF), lambda i: (layer, 0, 0), **const),
            pl.BlockSpec((None, D_FF, D_MODEL), lambda i: (layer, 0, 0), **const),
        ],
        out_specs=pl.BlockSpec((tm, D_MODEL), lambda i: (i, 0)),
        out_shape=jax.ShapeDtypeStruct((m, D_MODEL), F32),
        compiler_params=_cparams("parallel"),
        name="ffn",
    )(x2d, g3, w1, w2)


def _mem_kv_body(mem_ref, g_ref, w_ref, gk_ref, bones_ref, k_ref, v_ref, vt_ref):
    kv = jnp.dot(_rms(mem_ref[...], g_ref[...]).astype(BF16), w_ref[...], preferred_element_type=F32)
    k_ref[...] = _seg_rms(kv[:, :MEM_W], bones_ref[...], MEM_HD, gk_ref[...])
    v = kv[:, MEM_W:]
    v_ref[...] = v
    nb, _, n_mem = vt_ref.shape
    for b in range(nb):
        vt_ref[b] = v[b * n_mem:(b + 1) * n_mem].T


def _mem_kv(mem2d, g3, wkv, gk3, bones256, nb):
    depth = wkv.shape[0]
    m = mem2d.shape[0]
    return pl.pallas_call(
        _mem_kv_body,
        grid=(depth,),
        in_specs=[
            pl.BlockSpec((m, D_MODEL), lambda l: (0, 0)),
            pl.BlockSpec((None, 1, D_MODEL), lambda l: (l, 0, 0)),
            pl.BlockSpec((None, D_MODEL, 2 * MEM_W), lambda l: (l, 0, 0)),
            pl.BlockSpec((None, 1, MEM_W), lambda l: (l, 0, 0)),
            pl.BlockSpec((MEM_W, MEM_W), lambda l: (0, 0)),
        ],
        out_specs=[pl.BlockSpec((None, m, MEM_W), lambda l: (l, 0, 0))] * 2
        + [pl.BlockSpec((None, nb, MEM_W, m // nb), lambda l: (l, 0, 0, 0))],
        out_shape=[jax.ShapeDtypeStruct((depth, m, MEM_W), F32)] * 2
        + [jax.ShapeDtypeStruct((depth, nb, MEM_W, m // nb), F32)],
        compiler_params=_cparams("parallel"),
        name="mem_kv",
    )(mem2d, g3, wkv, gk3, bones256)


def _mix_out(x, og, orr, osw, wo_ref):
    y = _dot(og, wo_ref[0:GLA_VP, :])
    y = y + _dot(orr, wo_ref[GLA_VP:GLA_VP + RG_WIDTH, :])
    y = y + _dot(osw, wo_ref[GLA_VP + RG_WIDTH:MIX_W, :])
    return x + y


def _mem_q(x1, gx, wq_ref, gq, bones):
    q = jnp.dot(_rms(x1, gx).astype(BF16), wq_ref[...], preferred_element_type=F32)
    return _seg_rms(q, bones, MEM_HD, gq)


def _gla_tile(h_scr, wg_ref, bg_ref, gout_ref, st_scr, cum_scr, d_scr, oc_scr, st_out):
    tt = h_scr.shape[0]
    hf = tt // 2
    qraw = h_scr[:, C_GQ:C_GQ + GLA_QKP]
    k = h_scr[:, C_GK:C_GK + GLA_QKP]
    z = jnp.dot(qraw.astype(BF16), wg_ref[...], preferred_element_type=F32) + bg_ref[...]
    g = _log_sigmoid(z) * (1.0 / GLA_TAU)
    q = qraw * (GLA_DK ** -0.5)

    row = lax.broadcasted_iota(jnp.int32, (tt, tt), 0)
    col = lax.broadcasted_iota(jnp.int32, (tt, tt), 1)
    cum = _split3_dot_left((row >= col).astype(BF16), g)
    cum_scr[...] = cum
    rowi = lax.broadcasted_iota(jnp.int32, (tt, 1), 0)

    def boundary(b, n):
        return jnp.concatenate(
            [jnp.broadcast_to(cum_scr[pl.ds(gi * 2 * b + b - 1, 1), :], (n, GLA_QKP)) for gi in range(tt // (2 * b))],
            axis=0)

    rd1 = pltpu.roll(cum, 1, 0)
    rd2 = pltpu.roll(cum, 2, 0)
    ru1 = pltpu.roll(cum, tt - 1, 0)
    m4 = rowi & 3
    bounds = {
        1: jnp.where((rowi & 1) == 0, cum, rd1),
        2: jnp.where(m4 == 0, ru1, jnp.where(m4 == 1, cum, jnp.where(m4 == 2, rd1, rd2))),
        4: boundary(4, 2 * 4),
    }
    low = [(q.astype(BF16), k.astype(BF16), 0)]
    for b in (1, 2, 4):
        second = (rowi & (2 * b - 1)) >= b
        c = bounds[b]
        qs = q * jnp.exp(jnp.where(second, cum - c, NEG))
        ks = k * jnp.exp(jnp.where(second, NEG, c - cum))
        low.append((qs.astype(BF16), ks.astype(BF16), int(math.log2(2 * b))))

    mid = []
    b = SUBLANE
    while b <= hf:
        ng = tt // (2 * b)
        c = boundary(b, b)
        firsts = [gi * 2 * b for gi in range(ng)]
        seconds = [gi * 2 * b + b for gi in range(ng)]
        qs2 = _rows(q, seconds, b) * jnp.exp(_rows(cum, seconds, b) - c)
        ks1 = _rows(k, firsts, b) * jnp.exp(c - _rows(cum, firsts, b))
        if b < hf:
            zb = jnp.zeros((b, GLA_QKP), F32)
            ks1 = jnp.concatenate([piece for gi in range(ng) for piece in (ks1[gi * b:(gi + 1) * b], zb)], axis=0)
        mid.append((b, qs2.astype(BF16), ks1.astype(BF16)))
        b *= 2

    st = st_scr[...]
    o_inter = _dot_nt(q * jnp.exp(cum), st)
    last = cum_scr[pl.ds(tt - 1, 1), :]
    kd = k * jnp.exp(last - cum)
    v_all = h_scr[:, C_GV:C_GV + GLA_VP]
    upd = jnp.dot(v_all.T.astype(BF16), kd.astype(BF16), preferred_element_type=F32)
    vrow = lax.broadcasted_iota(jnp.int32, (GLA_VP, 1), 0) // GLA_DVP
    lane = lax.broadcasted_iota(jnp.int32, (1, GLA_QKP), 1)
    khead = ((lane >= GLA_DK).astype(jnp.int32) + (lane >= 2 * GLA_DK).astype(jnp.int32)
             + (lane >= 3 * GLA_DK).astype(jnp.int32) + (lane >= 4 * GLA_DK).astype(jnp.int32))
    st_new = jnp.where(vrow == khead, st * jnp.exp(last) + upd, 0.0)
    st_scr[...] = st_new
    st_out[...] = st_new

    nh = GLA_HEADS
    hq = hf // 2
    c128 = lax.broadcasted_iota(jnp.int32, (1, hf), 1)
    r_low = lax.broadcasted_iota(jnp.int32, (nh * hf, 1), 0) & (hf - 1)
    r_mid = lax.broadcasted_iota(jnp.int32, (nh * hq, 1), 0) & (hq - 1)
    low_masks = [(r_low >> s) == (c128 >> s) for (_, _, s) in low]
    mid_masks = {bb: (r_mid >> int(math.log2(bb))) == (c128 >> int(math.log2(2 * bb)))
                 for (bb, _, _) in mid if 2 * bb < hf}
    nt = (((1,), (1,)), ((), ()))
    hms = [_lane_head_mask(GLA_QKP, GLA_DK, h, BF16) for h in range(nh)]
    stack_heads = lambda xb: jnp.concatenate([xb * hm for hm in hms], axis=0)
    for half in range(2):
        rs = slice(half * hf, (half + 1) * hf)
        acc = None
        for (qb, kb, _), mk in zip(low, low_masks):
            p = lax.dot_general(stack_heads(qb[rs]), kb[rs], nt, preferred_element_type=F32)
            p = jnp.where(mk, p, 0.0)
            acc = p if acc is None else acc + p
        for h in range(nh):
            d_scr[h, half] = acc[h * hf:(h + 1) * hf]
        for bb, qb2, kb1 in mid:
            if bb == hf:
                continue
            cs = slice(half * hq, (half + 1) * hq)
            p = lax.dot_general(stack_heads(qb2[cs]), kb1[rs], nt, preferred_element_type=F32)
            if bb in mid_masks:
                p = jnp.where(mid_masks[bb], p, 0.0)
            for h in range(nh):
                for gi in range(hf // (2 * bb)):
                    dst = pl.ds(gi * 2 * bb + bb, bb)
                    src = slice(h * hq + gi * bb, h * hq + (gi + 1) * bb)
                    d_scr[h, half, dst, :] = d_scr[h, half, dst, :] + p[src, :]
    _, qb2, kb1 = mid[-1]
    off_all = lax.dot_general(stack_heads(qb2), kb1, nt, preferred_element_type=F32)
    gout = gout_ref[...]
    for h in range(nh):
        off = off_all[h * hf:(h + 1) * hf]
        sl = slice(h * GLA_DVP, (h + 1) * GLA_DVP)
        v_h = v_all[:, sl].astype(BF16)
        o0 = jnp.dot(d_scr[h, 0].astype(BF16), v_h[0:hf], preferred_element_type=F32)
        a1 = jnp.concatenate([off, d_scr[h, 1]], axis=1).astype(BF16)
        o1 = jnp.dot(a1, v_h, preferred_element_type=F32)
        o = jnp.concatenate([o0, o1], axis=0) + o_inter[:, sl]
        ms = jnp.sum(o * o, axis=-1, keepdims=True) * (1.0 / GLA_DV)
        gate = h_scr[:, C_GOG + h * GLA_DVP:C_GOG + (h + 1) * GLA_DVP]
        oc_scr[:, sl] = ((o * lax.rsqrt(ms + EPS)) * gout) * (gate * _sigmoid(gate))


def _rg_gates(xc, wa_ref, ba, wx_ref, bx, lam):
    r = _sigmoid(jnp.dot(xc.astype(BF16), wa_ref[...], preferred_element_type=F32) + ba)
    i = _sigmoid(jnp.dot(xc.astype(BF16), wx_ref[...], preferred_element_type=F32) + bx)
    log_a = (-RG_C * r) * _softplus(-lam)
    a = jnp.exp(log_a)
    b = jnp.sqrt((1.0 + a * a) * jnp.tanh(-log_a)) * (i * xc)
    return a, b


def _rg_pitch(tt):
    return tt // SUBLANE + SUBLANE


def _rg_tiles(h_scr, cw_ref, cb_ref, wa_ref, ba_ref, wx_ref, bx_ref, lam_ref, hc_scr, cc_scr, ra_scr, rb_scr, rh_scr,
              rp_scr, oc_scr):
    nb, tt = len(h_scr), h_scr[0].shape[0]
    rowi = lax.broadcasted_iota(jnp.int32, (tt, 1), 0)
    nl = RG_WIDTH // LANE
    seg = tt // SUBLANE
    pitch = _rg_pitch(tt)
    cw = cw_ref[...]
    for bi in range(nb):
        rx = h_scr[bi][:,C_RX:C_RX + RG_WIDTH]
        p0, p1, p2 = cc_scr[bi, 0:1, :], cc_scr[bi, 1:2, :], cc_scr[bi, 2:3, :]
        x1 = jnp.where(rowi == 0, p2, pltpu.roll(rx, 1, 0))
        x2 = jnp.where(rowi == 0, p1, jnp.where(rowi == 1, p2, pltpu.roll(rx, 2, 0)))
        x3 = jnp.where(rowi == 0, p0, jnp.where(rowi == 1, p1, jnp.where(rowi == 2, p2, pltpu.roll(rx, 3, 0))))
        xc = cb_ref[...] + x3 * cw[0:1, :]
        xc = xc + x2 * cw[1:2, :]
        xc = xc + x1 * cw[2:3, :]
        xc = xc + rx * cw[3:4, :]
        a, b = _rg_gates(xc, wa_ref, ba_ref[...], wx_ref, bx_ref[...], lam_ref[...])
        for c in range(nl):
            for j in range(SUBLANE):
                dst = pl.ds(j * pitch, seg)
                ra_scr[bi, c, dst, :] = a[j * seg:(j + 1) * seg, c * LANE:(c + 1) * LANE]
                rb_scr[bi, c, dst, :] = b[j * seg:(j + 1) * seg, c * LANE:(c + 1) * LANE]

    chains = [(bi, c) for bi in range(nb) for c in range(nl)]
    hloc = {ch: jnp.zeros((SUBLANE, LANE), F32) for ch in chains}
    pc = {ch: jnp.ones((SUBLANE, LANE), F32) for ch in chains}
    for i in range(seg):
        idx = pl.ds(i, SUBLANE, stride=pitch)
        for ch in chains:
            bi, c = ch
            ai = ra_scr[bi, c, idx, :]
            hloc[ch] = ai * hloc[ch] + rb_scr[bi, c, idx, :]
            pc[ch] = ai * pc[ch]
            rh_scr[bi, c, idx, :] = hloc[ch]
            rp_scr[bi, c, idx, :] = pc[ch]
    outs = []
    for bi in range(nb):
        hl = jnp.concatenate([hloc[(bi, c)] for c in range(nl)], axis=1)
        pl_ = jnp.concatenate([pc[(bi, c)] for c in range(nl)], axis=1)
        hprev = hc_scr[bi, 0:1, :]
        carries = []
        for j in range(SUBLANE):
            carries.append(jnp.broadcast_to(hprev, (seg, RG_WIDTH)))
            hprev = hl[j:j + 1, :] + pl_[j:j + 1, :] * hprev
        hc_scr[bi, 0:1, :] = hprev
        carry = jnp.concatenate(carries, axis=0)
        unpitch = lambda r, c: jnp.concatenate([r[bi, c, pl.ds(j * pitch, seg), :] for j in range(SUBLANE)], axis=0)
        h = jnp.concatenate(
            [unpitch(rh_scr, c) + unpitch(rp_scr, c) * carry[:, c * LANE:(c + 1) * LANE] for c in range(nl)], axis=1)
        oc_scr[bi][:, GLA_VP:GLA_VP + RG_WIDTH] = h * _gelu_tanh(h_scr[bi][:,C_RY:C_RY + RG_WIDTH])
        tail = h_scr[bi][pl.ds(tt - (CONV_W - 1), CONV_W - 1), C_RX:C_RX + RG_WIDTH]
        cc_scr[bi, 0:CONV_W - 1, :] = tail
        outs.append((hprev, tail))
    return outs


def _stack_heads(x, masks):
    return jnp.concatenate([x * m for m in masks], axis=0)


def _unstack_heads_t(ot, seg, n):
    heads = ot.shape[0] // seg
    blocks = [ot[h * seg:(h + 1) * seg, h * n:(h + 1) * n] for h in range(heads)]
    return jnp.concatenate(blocks, axis=0).T


def _softmax_cols(s, sink=None):
    m = jnp.max(s, axis=0, keepdims=True)
    if sink is not None:
        m = jnp.maximum(m, sink)
    e = jnp.exp(s - m)
    den = jnp.sum(e, axis=0, keepdims=True)
    if sink is not None:
        den = den + jnp.exp(sink - m)
    return e * (1.0 / den)


def _swa_tiles(h_scr, sink_ref, layer, first, gq_ref, gk_ref, b256_ref, b128_ref, dup_ref, dupt_ref, kp_scr, vpt_scr,
               oc_scr):
    nb, tt = len(h_scr), h_scr[0].shape[0]
    w = WINDOW
    nh = SWA_HEADS
    hmb = [_lane_head_mask(SWA_Q, HEAD_DIM, h, BF16) for h in range(nh)]
    kj = lax.broadcasted_iota(jnp.int32, (2 * w, w), 0)
    qi = lax.broadcasted_iota(jnp.int32, (2 * w, w), 1)
    dist_i = qi + w - kj
    in_window = (dist_i >= 0) & (dist_i <= WINDOW)
    dist = dist_i.astype(F32)
    bias = jnp.concatenate([jnp.where(in_window, -_alibi_slope(h) * dist, NEG) for h in range(nh)], axis=1)
    kj4 = lax.broadcasted_iota(jnp.int32, (2 * w, nh * w), 0)
    bias_first = jnp.where(jnp.logical_and(first, kj4 < w), NEG, bias)
    sink_row = jnp.concatenate([jnp.full((1, w), sink_ref[layer, h], F32) for h in range(nh)], axis=1)
    dup = dup_ref[...]
    dupt = dupt_ref[...]
    nt = (((1,), (1,)), ((), ()))
    s_parts, vt_parts, wins = [], [], []
    for bi in range(nb):
        qn = _seg_rms(h_scr[bi][:, C_SQ:C_SQ + SWA_Q], b256_ref[...], HEAD_DIM, gq_ref[...])
        qn = (qn * (HEAD_DIM ** -0.5)).astype(BF16)
        kn = _seg_rms(h_scr[bi][:, C_SK:C_SK + SWA_KVW], b128_ref[...], HEAD_DIM, gk_ref[...])
        v = h_scr[bi][:, C_SV:C_SV + SWA_KVW]
        vt = v.T
        kcat = jnp.concatenate([kp_scr[bi], kn], axis=0).astype(BF16)
        vtcat = jnp.concatenate([vpt_scr[bi], vt], axis=1).astype(BF16)
        kexp = jnp.dot(kcat, dup, preferred_element_type=F32).astype(BF16)
        vexpt = jnp.dot(dupt, vtcat, preferred_element_type=F32).astype(BF16)
        for blk in range(tt // w):
            q4 = _stack_heads(qn[blk * w:(blk + 1) * w], hmb)
            s = lax.dot_general(kexp[blk * w:(blk + 2) * w], q4, nt, preferred_element_type=F32)
            s_parts.append(s + (bias_first if blk == 0 else bias))
            vt_parts.append(vexpt[:, blk * w:(blk + 2) * w])
        k_win = kn[tt - w:tt]
        kp_scr[bi] = k_win
        vpt_scr[bi] = vt[:, tt - w:tt]
        wins.append((k_win, v[tt - w:tt]))
    p_all = _softmax_cols(jnp.concatenate(s_parts, axis=1),
                          jnp.concatenate([sink_row] * len(s_parts), axis=1)).astype(BF16)
    n4 = nh * w
    for i, vtb in enumerate(vt_parts):
        bi, blk = divmod(i, tt // w)
        ot = jnp.dot(vtb, p_all[:, i * n4:(i + 1) * n4], preferred_element_type=F32)
        oc_scr[bi][blk * w:(blk + 1) * w, GLA_VP + RG_WIDTH:MIX_W] = _unstack_heads_t(ot, HEAD_DIM, w)
    return wins


def _mem_attend_tiles(x1s, gx, wq_ref, gq, bones, mk_ref, mvt_ref, wmo_ref):
    nh = MEM_HEADS
    nb, tt = len(x1s), x1s[0].shape[0]
    hmb = [_lane_head_mask(MEM_W, MEM_HD, h, BF16) for h in range(nh)]
    nt = (((1,), (1,)), ((), ()))
    xn = jnp.concatenate([_rms(x1, gx).astype(BF16) for x1 in x1s], axis=0)
    q = jnp.dot(xn, wq_ref[...], preferred_element_type=F32)
    qn = (_seg_rms(q, bones, MEM_HD, gq) * (MEM_HD ** -0.5)).astype(BF16)
    s_parts = [lax.dot_general(mk_ref[bi].astype(BF16), _stack_heads(qn[bi * tt:(bi + 1) * tt], hmb), nt,
                               preferred_element_type=F32) for bi in range(nb)]
    p_all = _softmax_cols(jnp.concatenate(s_parts, axis=1)).astype(BF16)
    n4 = nh * tt
    o = jnp.concatenate(
        [_unstack_heads_t(jnp.dot(mvt_ref[bi].astype(BF16), p_all[:, bi * n4:(bi + 1) * n4],
                                  preferred_element_type=F32), MEM_HD, tt) for bi in range(nb)], axis=0)
    y = jnp.dot(o.astype(BF16), wmo_ref[...], preferred_element_type=F32)
    return [x1s[bi] + y[bi * tt:(bi + 1) * tt] for bi in range(nb)]


def _mixer_body(sink_ref, x_ref, gmix_ref, win_ref, wg_ref, bg_ref, gout_ref, cw_ref, cb_ref, wa_ref,
                ba_ref, wx_ref, bx_ref, lam_ref, gq_ref, gk_ref, b256_ref, b128_ref, dup_ref, dupt_ref, wo_ref,
                gmx_ref, wq_ref, mgq_ref, mk_ref, mv_ref, wmo_ref,
                o_ref, st_ref, hl_ref, cbuf_ref, ko_ref, vo_ref,
                h_scr, oc_scr, st_scr, cum_scr, d_scr, hc_scr, cc_scr, ra_scr, rb_scr, rh_scr, rp_scr, kp_scr, vp_scr,
                *, layer):
    nb, tt = x_ref.shape[0], x_ref.shape[1]
    first = pl.program_id(0) == 0

    @pl.when(first)
    def _():
        for scr in (st_scr, hc_scr, cc_scr, kp_scr, vp_scr):
            scr[...] = jnp.zeros_like(scr)

    xn = jnp.concatenate([_rms(x_ref[bi], gmix_ref[...]).astype(BF16) for bi in range(nb)], axis=0)
    h_scr[...] = jnp.dot(xn, win_ref[...], preferred_element_type=F32)
    hs = [h_scr.at[pl.ds(bi * tt, tt)] for bi in range(nb)]
    ocs = [oc_scr.at[pl.ds(bi * tt, tt)] for bi in range(nb)]
    for bi in range(nb):
        _gla_tile(hs[bi], wg_ref, bg_ref, gout_ref, st_scr.at[bi], cum_scr.at[bi], d_scr.at[bi], ocs[bi],
                  st_ref.at[bi])
    rg_out = _rg_tiles(hs, cw_ref, cb_ref, wa_ref, ba_ref, wx_ref, bx_ref, lam_ref, hc_scr, cc_scr, ra_scr, rb_scr,
                       rh_scr, rp_scr, ocs)
    for bi, (h_last, tail) in enumerate(rg_out):
        hl_ref[bi] = h_last
        cbuf_ref[bi] = tail
    wins = _swa_tiles(hs, sink_ref, layer, first, gq_ref, gk_ref, b256_ref, b128_ref, dup_ref, dupt_ref, kp_scr,
                      vp_scr, ocs)
    for bi, (k_win, v_win) in enumerate(wins):
        ko_ref[bi] = k_win
        vo_ref[bi] = v_win
    y = jnp.dot(oc_scr[...].astype(BF16), wo_ref[...], preferred_element_type=F32)
    x1s = [x_ref[bi] + y[bi * tt:(bi + 1) * tt] for bi in range(nb)]
    x2s = _mem_attend_tiles(x1s, gmx_ref[...], wq_ref, mgq_ref[...], b256_ref[...], mk_ref, mv_ref, wmo_ref)
    for bi in range(nb):
        o_ref[bi] = x2s[bi]


def _mixer_prompt(sinks, x3d, gmix3, w_in_p, wg, bg3, gout3, cw, cb3, wa, ba3, wx, bx3, lam3, gq3, gk3, b256, b128,
                  dup, dupt, wo, gmx3, wq, mgq3, mk, mvt, wmo, layer):
    nb, t_len, _ = x3d.shape
    tt = MIX_T
    nt = t_len // tt
    n_mem = mk.shape[2]
    lay3 = lambda t: (layer, 0, 0)
    once = dict(pipeline_mode=pl.Buffered(1))
    vec = lambda w: pl.BlockSpec((None, 1, w), lay3)
    full = lambda a: pl.BlockSpec(a.shape, lambda t: (0,) * a.ndim)
    state = lambda r, w: pl.BlockSpec((nb, r, w), lambda t: (0, 0, 0))
    mem = pl.BlockSpec((None, nb, n_mem, MEM_W), lambda t: (layer, 0, 0, 0))
    memt = pl.BlockSpec((None, nb, MEM_W, n_mem), lambda t: (layer, 0, 0, 0))
    return pl.pallas_call(
        functools.partial(_mixer_body, layer=layer),
        grid=(nt,),
        in_specs=[
            pl.BlockSpec(memory_space=pltpu.SMEM),
            pl.BlockSpec((nb, tt, D_MODEL), lambda t: (0, t, 0)),
            vec(D_MODEL),
            pl.BlockSpec((None, D_MODEL, NP), lay3, **once),
            pl.BlockSpec((None, GLA_QKP, GLA_QKP), lay3), vec(GLA_QKP), vec(GLA_DVP),
            pl.BlockSpec((None, CONV_W, RG_WIDTH), lay3), vec(RG_WIDTH),
            pl.BlockSpec((None, RG_WIDTH, RG_WIDTH), lay3), vec(RG_WIDTH),
            pl.BlockSpec((None, RG_WIDTH, RG_WIDTH), lay3), vec(RG_WIDTH), vec(RG_WIDTH),
            vec(SWA_Q), vec(SWA_KVW), full(b256), full(b128), full(dup), full(dupt),
            pl.BlockSpec((None, MIX_W, D_MODEL), lay3, **once),
            vec(D_MODEL), pl.BlockSpec((None, D_MODEL, MEM_W), lay3), vec(MEM_W),
            mem, memt,
            pl.BlockSpec((None, MEM_W, D_MODEL), lay3),
        ],
        out_specs=[
            pl.BlockSpec((nb, tt, D_MODEL), lambda t: (0, t, 0)),
            state(GLA_VP, GLA_QKP), state(1, RG_WIDTH), state(CONV_W - 1, RG_WIDTH),
            state(WINDOW, SWA_KVW), state(WINDOW, SWA_KVW),
        ],
        out_shape=[
            jax.ShapeDtypeStruct((nb, t_len, D_MODEL), F32),
            jax.ShapeDtypeStruct((nb, GLA_VP, GLA_QKP), F32),
            jax.ShapeDtypeStruct((nb, 1, RG_WIDTH), F32),
            jax.ShapeDtypeStruct((nb, CONV_W - 1, RG_WIDTH), F32),
            jax.ShapeDtypeStruct((nb, WINDOW, SWA_KVW), F32),
            jax.ShapeDtypeStruct((nb, WINDOW, SWA_KVW), F32),
        ],
        scratch_shapes=[
            pltpu.VMEM((nb * tt, NP), F32), pltpu.VMEM((nb * tt, MIX_W), F32),
            pltpu.VMEM((nb, GLA_VP, GLA_QKP), F32), pltpu.VMEM((nb, tt, GLA_QKP), F32),
            pltpu.VMEM((nb, GLA_HEADS, 2, tt // 2, tt // 2), F32),
            pltpu.VMEM((nb, SUBLANE, RG_WIDTH), F32), pltpu.VMEM((nb, SUBLANE, RG_WIDTH), F32),
            *[pltpu.VMEM((nb, RG_WIDTH // LANE, SUBLANE * _rg_pitch(tt), LANE), F32) for _ in range(4)],
            pltpu.VMEM((nb, WINDOW, SWA_KVW), F32), pltpu.VMEM((nb, WINDOW, SWA_KVW), F32),
        ],
        compiler_params=_cparams("arbitrary"),
        name="mixer_prompt",
    )(sinks, x3d, gmix3, w_in_p, wg, bg3, gout3, cw, cb3, wa, ba3, wx, bx3, lam3, gq3, gk3, b256, b128, dup, dupt,
      wo, gmx3, wq, mgq3, mk, mvt, wmo)


def _sample_rows_body(h_ref, hst_ref, cbuf_ref, cw_ref, cb_ref, wa_ref, ba_ref, wx_ref, bx_ref, lam_ref, gq_ref,
                      gk_ref, b256_ref, b128_ref, dup_ref,
                      org_ref, hn_ref, cnew_ref, qn_ref, kx_ref, vx_ref, knt_ref, vnt_ref):
    rx = h_ref[:, C_RX:C_RX + RG_WIDTH]
    cw = cw_ref[...]
    xc = cb_ref[...] + cbuf_ref[0] * cw[0:1, :]
    xc = xc + cbuf_ref[1] * cw[1:2, :]
    xc = xc + cbuf_ref[2] * cw[2:3, :]
    xc = xc + rx * cw[3:4, :]
    a, b = _rg_gates(xc, wa_ref, ba_ref[...], wx_ref, bx_ref[...], lam_ref[...])
    hn = a * hst_ref[...] + b
    hn_ref[...] = hn
    org_ref[...] = hn * _gelu_tanh(h_ref[:, C_RY:C_RY + RG_WIDTH])
    cnew_ref[0] = cbuf_ref[1]
    cnew_ref[1] = cbuf_ref[2]
    cnew_ref[2] = rx

    qn_ref[...] = _seg_rms(h_ref[:, C_SQ:C_SQ + SWA_Q], b256_ref[...], HEAD_DIM, gq_ref[...])
    kn = _seg_rms(h_ref[:, C_SK:C_SK + SWA_KVW], b128_ref[...], HEAD_DIM, gk_ref[...])
    v = h_ref[:, C_SV:C_SV + SWA_KVW]
    kx_ref[...] = jnp.dot(kn.astype(BF16), dup_ref[...], preferred_element_type=F32)
    vx_ref[...] = jnp.dot(v.astype(BF16), dup_ref[...], preferred_element_type=F32)
    knt = kn.T
    vnt = v.T
    nblk, _, bt = knt_ref.shape
    for i in range(nblk):
        knt_ref[i] = knt[:, i * bt:(i + 1) * bt]
        vnt_ref[i] = vnt[:, i * bt:(i + 1) * bt]


def _sample_rows(hs, hst, cbuf, cw, cb3, wa, ba3, wx, bx3, lam3, gq3, gk3, b256, b128, dup, layer, bt):
    n = hs.shape[0]
    lay3 = lambda i: (layer, 0, 0)
    full = lambda a: pl.BlockSpec(a.shape, lambda i: (0,) * a.ndim)
    vec = lambda w: pl.BlockSpec((None, 1, w), lay3)
    sq = lambda w: pl.BlockSpec((None, w, w), lay3)
    taps = pl.BlockSpec((None, CONV_W - 1, n, RG_WIDTH), lambda i: (layer, 0, 0, 0))
    shapes = [(n, RG_WIDTH), (n, RG_WIDTH), (CONV_W - 1, n, RG_WIDTH), (n, SWA_Q), (n, SWA_Q), (n, SWA_Q),
              (n // bt, SWA_KVW, bt), (n // bt, SWA_KVW, bt)]
    return pl.pallas_call(
        _sample_rows_body,
        grid=(1,),
        in_specs=[
            full(hs), pl.BlockSpec((None, n, RG_WIDTH), lay3), taps,
            pl.BlockSpec((None, CONV_W, RG_WIDTH), lay3), vec(RG_WIDTH),
            sq(RG_WIDTH), vec(RG_WIDTH), sq(RG_WIDTH), vec(RG_WIDTH), vec(RG_WIDTH),
            vec(SWA_Q), vec(SWA_KVW), full(b256), full(b128), full(dup),
        ],
        out_specs=[pl.BlockSpec(s, lambda i, nd=len(s): (0,) * nd) for s in shapes],
        out_shape=[jax.ShapeDtypeStruct(s, F32) for s in shapes],
        compiler_params=_cparams("arbitrary"),
        name="sample_rows",
    )(hs, hst, cbuf, cw, cb3, wa, ba3, wx, bx3, lam3, gq3, gk3, b256, b128, dup)


def _sample_gla_body(s0_ref, h_ref, wg_ref, bg_ref, goutc_ref, o_ref, sn_ref, qt_scr, kt_scr, et_scr, vt_scr, gt_scr):
    h = pl.program_id(0)
    qraw = h_ref[:, C_GQ:C_GQ + GLA_QKP]
    z = jnp.dot(qraw.astype(BF16), wg_ref[...], preferred_element_type=F32) + bg_ref[...]
    et_scr[...] = jnp.exp(_log_sigmoid(z) * (1.0 / GLA_TAU)).T
    qt_scr[...] = (qraw * (GLA_DK ** -0.5)).T
    kt_scr[...] = h_ref[:, C_GK:C_GK + GLA_QKP].T
    vt_scr[...] = h_ref[:, C_GV:C_GV + GLA_VP].T
    gt_scr[...] = h_ref[:, C_GOG:C_GOG + GLA_VP].T
    r0 = pl.multiple_of(h * GLA_DK, SUBLANE)
    v0 = pl.multiple_of(h * GLA_DVP, GLA_DVP)
    vt = vt_scr[pl.ds(v0, GLA_DV), :]
    o = jnp.zeros(vt.shape, F32)
    for k in range(GLA_DK):
        sn = et_scr[pl.ds(r0 + k, 1), :] * s0_ref[k] + kt_scr[pl.ds(r0 + k, 1), :] * vt
        sn_ref[k] = sn
        o = o + qt_scr[pl.ds(r0 + k, 1), :] * sn
    ms = jnp.sum(o * o, axis=0, keepdims=True) * (1.0 / GLA_DV)
    gate = gt_scr[pl.ds(v0, GLA_DV), :]
    y = ((o * lax.rsqrt(ms + EPS)) * goutc_ref[...]) * (gate * _sigmoid(gate))
    y = jnp.concatenate([y, jnp.zeros((GLA_DVP - GLA_DV, y.shape[1]), F32)], axis=0)
    o_ref[...] = y.T


def _sample_gla(s0t, hs, wg, bg3, goutc, layer):
    n = hs.shape[0]
    lay3 = lambda h: (layer, 0, 0)
    return pl.pallas_call(
        _sample_gla_body,
        grid=(GLA_HEADS,),
        in_specs=[
            pl.BlockSpec((None, None, GLA_DK, GLA_DV, n), lambda h: (layer, h, 0, 0, 0)),
            pl.BlockSpec((n, NP), lambda h: (0, 0)),
            pl.BlockSpec((None, GLA_QKP, GLA_QKP), lay3),
            pl.BlockSpec((None, 1, GLA_QKP), lay3),
            pl.BlockSpec((None, GLA_DV, 1), lay3),
        ],
        out_specs=[pl.BlockSpec((n, GLA_DVP), lambda h: (0, h)),
                   pl.BlockSpec((None, None, GLA_DK, GLA_DV, n), lambda h: (layer, h, 0, 0, 0))],
        out_shape=[jax.ShapeDtypeStruct((n, GLA_VP), F32), jax.ShapeDtypeStruct(s0t.shape, F32)],
        scratch_shapes=[pltpu.VMEM((GLA_QKP, n), F32)] * 3 + [pltpu.VMEM((GLA_VP, n), F32)] * 2,
        input_output_aliases={0: 1},
        compiler_params=_cparams("arbitrary"),
        name="sample_gla",
    )(s0t, hs, wg, bg3, goutc)


def _sample_swa_body(sink_ref, kct_ref, vct_ref, qn_ref, kx_ref, vx_ref, knt_ref, vnt_ref, dupt_ref, o_ref, kot_ref,
                     vot_ref, *, layer):
    bt = kct_ref.shape[0]
    dupt = dupt_ref[...]
    hi = lax.broadcasted_iota(jnp.int32, (SWA_HEADS, SWA_Q), 0)
    li = lax.broadcasted_iota(jnp.int32, (SWA_HEADS, SWA_Q), 1)
    hm4 = ((li >= hi * HEAD_DIM) & (li < (hi + 1) * HEAD_DIM)).astype(F32)
    hrow = lax.broadcasted_iota(jnp.int32, (SWA_HEADS, 1), 0)
    slope = jnp.zeros((SWA_HEADS, 1), F32)
    sink = jnp.zeros((SWA_HEADS, 1), F32)
    for h in range(SWA_HEADS):
        slope = jnp.where(hrow == h, _alibi_slope(h), slope)
        sink = jnp.where(hrow == h, sink_ref[layer, h], sink)
    dist = (WINDOW - lax.broadcasted_iota(jnp.int32, (1, WINDOW), 1)).astype(F32)
    last = lax.broadcasted_iota(jnp.int32, (SWA_KVW, WINDOW), 1) == WINDOW - 1
    nt = (((1,), (1,)), ((), ()))
    for j in range(bt):
        kt = kct_ref[j]
        vt = vct_ref[j]
        kexp = jnp.dot(dupt, kt.astype(BF16), preferred_element_type=F32).astype(BF16)
        vexp = jnp.dot(dupt, vt.astype(BF16), preferred_element_type=F32).astype(BF16)
        q4 = (qn_ref[j:j + 1, :] * hm4).astype(BF16)
        s = jnp.dot(q4, kexp, preferred_element_type=F32)
        s = s * (HEAD_DIM ** -0.5) - slope * dist
        s_new = jnp.sum(q4.astype(F32) * kx_ref[j:j + 1, :], axis=-1, keepdims=True) * (HEAD_DIM ** -0.5)
        m = jnp.maximum(jnp.maximum(jnp.max(s, axis=-1, keepdims=True), s_new), sink)
        e = jnp.exp(s - m)
        e_new = jnp.exp(s_new - m)
        den = jnp.sum(e, axis=-1, keepdims=True) + e_new + jnp.exp(sink - m)
        o4 = lax.dot_general((e / den).astype(BF16), vexp, nt, preferred_element_type=F32)
        o4 = o4 + (e_new / den).astype(BF16).astype(F32) * vx_ref[j:j + 1, :]
        o_ref[j:j + 1, :] = jnp.sum(o4 * hm4, axis=0, keepdims=True)
        kot_ref[j] = jnp.where(last, knt_ref[:, j:j + 1], pltpu.roll(kt, WINDOW - 1, 1))
        vot_ref[j] = jnp.where(last, vnt_ref[:, j:j + 1], pltpu.roll(vt, WINDOW - 1, 1))


def _sample_swa(sinks, kct, vct, qn, kx, vx, knt, vnt, dupt, layer, bt):
    n = qn.shape[0]
    row = lambda w: pl.BlockSpec((bt, w), lambda i: (i, 0))
    cache = pl.BlockSpec((None, bt, SWA_KVW, WINDOW), lambda i: (layer, i, 0, 0))
    col = pl.BlockSpec((None, SWA_KVW, bt), lambda i: (i, 0, 0))
    return pl.pallas_call(
        functools.partial(_sample_swa_body, layer=layer),
        grid=(n // bt,),
        in_specs=[
            pl.BlockSpec(memory_space=pltpu.SMEM), cache, cache,
            row(SWA_Q), row(SWA_Q), row(SWA_Q), col, col,
            pl.BlockSpec(dupt.shape, lambda i: (0, 0)),
        ],
        out_specs=[row(SWA_Q), cache, cache],
        out_shape=[jax.ShapeDtypeStruct((n, SWA_Q), F32), jax.ShapeDtypeStruct(kct.shape, F32),
                   jax.ShapeDtypeStruct(vct.shape, F32)],
        input_output_aliases={1: 1, 2: 2},
        compiler_params=_cparams("parallel"),
        name="sample_swa",
    )(sinks, kct, vct, qn, kx, vx, knt, vnt, dupt)


def _sample_post1_body(x_ref, og_ref, or_ref, os_ref, wo_ref, gx_ref, wq_ref, gq_ref, bones_ref, x1_ref, qn_ref):
    x1 = _mix_out(x_ref[...], og_ref[...], or_ref[...], os_ref[...], wo_ref)
    x1_ref[...] = x1
    qn_ref[...] = _mem_q(x1, gx_ref[...], wq_ref, gq_ref[...], bones_ref[...])


def _sample_post1(x2d, og, orr, osw, wo, gx3, wq, gq3, bones256, layer):
    n = x2d.shape[0]
    lay3 = lambda i: (layer, 0, 0)
    full = lambda a: pl.BlockSpec(a.shape, lambda i: (0,) * a.ndim)
    return pl.pallas_call(
        _sample_post1_body,
        grid=(1,),
        in_specs=[
            full(x2d), full(og), full(orr), full(osw),
            pl.BlockSpec((None, MIX_W, D_MODEL), lay3),
            pl.BlockSpec((None, 1, D_MODEL), lay3),
            pl.BlockSpec((None, D_MODEL, MEM_W), lay3),
            pl.BlockSpec((None, 1, MEM_W), lay3),
            full(bones256),
        ],
        out_specs=[pl.BlockSpec((n, D_MODEL), lambda i: (0, 0)), pl.BlockSpec((n, MEM_W), lambda i: (0, 0))],
        out_shape=[jax.ShapeDtypeStruct((n, D_MODEL), F32), jax.ShapeDtypeStruct((n, MEM_W), F32)],
        compiler_params=_cparams("arbitrary"),
        name="sample_post1",
    )(x2d, og, orr, osw, wo, gx3, wq, gq3, bones256)


def _sample_mem_body(qn_ref, mk_ref, mv_ref, o_ref):
    bt = mk_ref.shape[0]
    hi = lax.broadcasted_iota(jnp.int32, (MEM_HEADS, MEM_W), 0)
    li = lax.broadcasted_iota(jnp.int32, (MEM_HEADS, MEM_W), 1)
    hm4 = ((li >= hi * MEM_HD) & (li < (hi + 1) * MEM_HD)).astype(F32)
    for j in range(bt):
        q4 = (qn_ref[j:j + 1, :] * hm4).astype(BF16)
        s = jnp.dot(q4, mk_ref[j].astype(BF16), preferred_element_type=F32)
        p = _softmax_rows(s * (MEM_HD ** -0.5))
        o4 = lax.dot_general(p.astype(BF16), mv_ref[j].astype(BF16), (((1,), (1,)), ((), ())),
                             preferred_element_type=F32)
        o_ref[j:j + 1, :] = jnp.sum(o4 * hm4, axis=0, keepdims=True)


def _sample_mem(qn, mk, mv, layer, bt):
    n = qn.shape[0]
    n_mem = mk.shape[3]
    cache = pl.BlockSpec((None, bt, MEM_W, n_mem), lambda i: (layer, i, 0, 0))
    return pl.pallas_call(
        _sample_mem_body,
        grid=(n // bt,),
        in_specs=[pl.BlockSpec((bt, MEM_W), lambda i: (i, 0)), cache, cache],
        out_specs=pl.BlockSpec((bt, MEM_W), lambda i: (i, 0)),
        out_shape=jax.ShapeDtypeStruct((n, MEM_W), F32),
        compiler_params=_cparams("parallel"),
        name="sample_mem",
    )(qn, mk, mv)


def _ffn_res_body(x_ref, o_ref_in, wmo_ref, g_ref, w1_ref, w2_ref, out_ref):
    x = x_ref[...] + jnp.dot(o_ref_in[...].astype(BF16), wmo_ref[...], preferred_element_type=F32)
    xn = _rms(x, g_ref[...]).astype(BF16)
    acc = x
    for c in range(D_FF // FF_CHUNK):
        h = jnp.dot(xn, w1_ref[:, c * FF_CHUNK:(c + 1) * FF_CHUNK], preferred_element_type=F32)
        h = jnp.square(jnp.maximum(h, 0.0))
        acc = acc + jnp.dot(h.astype(BF16), w2_ref[c * FF_CHUNK:(c + 1) * FF_CHUNK, :], preferred_element_type=F32)
    out_ref[...] = acc


def _sample_ffn(x1, o, wmo, g3, w1, w2, layer):
    n = x1.shape[0]
    lay3 = lambda i: (layer, 0, 0)
    const = dict(pipeline_mode=pl.Buffered(1))
    return pl.pallas_call(
        _ffn_res_body,
        grid=(1,),
        in_specs=[
            pl.BlockSpec((n, D_MODEL), lambda i: (0, 0)),
            pl.BlockSpec((n, MEM_W), lambda i: (0, 0)),
            pl.BlockSpec((None, MEM_W, D_MODEL), lay3),
            pl.BlockSpec((None, 1, D_MODEL), lay3),
            pl.BlockSpec((None, D_MODEL, D_FF), lay3, **const),
            pl.BlockSpec((None, D_FF, D_MODEL), lay3, **const),
        ],
        out_specs=pl.BlockSpec((n, D_MODEL), lambda i: (0, 0)),
        out_shape=jax.ShapeDtypeStruct((n, D_MODEL), F32),
        compiler_params=_cparams("arbitrary"),
        name="sample_ffn",
    )(x1, o, wmo, g3, w1, w2)


def _pad_heads(a, axis, heads, width, padded):
    shp = a.shape
    a = a.reshape(shp[:axis] + (heads, width) + shp[axis + 1:])
    pad = [(0, 0)] * a.ndim
    pad[axis + 1] = (0, padded - width)
    a = jnp.pad(a, pad)
    return a.reshape(shp[:axis] + (heads * padded,) + shp[axis + 1:])


def _w_in_pieces(w):
    offs = [0]
    for n in (GLA_QK, GLA_QK, GLA_V, GLA_RANK, GLA_V, RG_WIDTH, RG_WIDTH, SWA_Q, SWA_KVW, SWA_KVW):
        offs.append(offs[-1] + n)
    gq, gk, gv, glr, gog, rx, ry, sq, sk, sv = [w[..., offs[i]:offs[i + 1]] for i in range(10)]
    zeros = lambda n: jnp.zeros(w.shape[:-1] + (n,), w.dtype)

    def heads(a):
        out = []
        for h in range(GLA_HEADS):
            out += [a[..., h * GLA_DV:(h + 1) * GLA_DV], zeros(GLA_DVP - GLA_DV)]
        return out

    return ([rx, ry, gq, glr, zeros(GLA_QKP - GLA_QK - GLA_RANK), gk, zeros(GLA_QKP - GLA_QK), sq]
            + heads(gv) + heads(gog) + [sk, sv])


def _prep_w_in_body(w_ref, o_ref):
    o_ref[...] = jnp.concatenate(_w_in_pieces(w_ref[...]), axis=-1).astype(BF16)


def _prep_w_in(w_in):
    depth, d, width = w_in.shape
    tr = 256
    return pl.pallas_call(
        _prep_w_in_body,
        grid=(depth, d // tr),
        in_specs=[pl.BlockSpec((None, tr, width), lambda l, r: (l, r, 0))],
        out_specs=pl.BlockSpec((None, tr, NP), lambda l, r: (l, r, 0)),
        out_shape=jax.ShapeDtypeStruct((depth, d, NP), BF16),
        compiler_params=_cparams("parallel", "parallel"),
        name="prep_w_in",
    )(w_in)


def _block_diag(w):
    depth, nb, bw, _ = w.shape
    eye = jnp.eye(nb, dtype=w.dtype)
    return jnp.einsum("lncd,nm->lncmd", w, eye).reshape(depth, nb * bw, nb * bw)


def _block_ones(n, seg):
    i = jnp.arange(n) // seg
    return (i[:, None] == i[None, :]).astype(BF16)


def _dup_matrix():
    src = jnp.arange(SWA_Q)
    src = (src // HEAD_DIM // SWA_G) * HEAD_DIM + src % HEAD_DIM
    return (jnp.arange(SWA_KVW)[:, None] == src[None, :]).astype(BF16)


def kernel(x_prompt, x_sample, mem_prompt, state_gla, state_rg_h, state_rg_conv, cache_swa_k, cache_swa_v, cache_mem_k, cache_mem_v, g_mix, w_in, gla_w_gate2, gla_b_gate, gla_g_out, rg_conv_w, rg_conv_b, rg_w_a, rg_b_a, rg_w_x, rg_b_x, rg_lam, swa_g_q, swa_g_k, swa_sinks, w_out, g_mem_x, g_mem_m, mem_w_q, mem_w_kv, mem_g_q, mem_g_k, mem_w_o, g_ffn, ffn_w1, ffn_w2):
    depth = w_in.shape[0]
    nb, t_len, _ = x_prompt.shape
    ns = x_sample.shape[0]
    n_mem = mem_prompt.shape[1]
    assert t_len % ROW_T == 0 and t_len % MIX_T == 0 and x_sample.shape[1] == 1 and ns % 8 == 0
    row3 = lambda a: a.reshape(depth, 1, a.shape[-1])

    w_in_p = _prep_w_in(w_in)
    wg = jnp.zeros((depth, GLA_QKP, GLA_QKP), F32).at[:, GLR_LANE:GLR_LANE + GLA_RANK, :GLA_QK].set(gla_w_gate2)
    wg = wg.astype(BF16)
    bg3 = row3(jnp.pad(gla_b_gate, ((0, 0), (0, GLA_QKP - GLA_QK))))
    gout3 = row3(jnp.pad(gla_g_out, ((0, 0), (0, GLA_DVP - GLA_DV))))
    wa = _block_diag(rg_w_a).astype(BF16)
    wx = _block_diag(rg_w_x).astype(BF16)
    ba3, bx3, lam3, cb3 = row3(rg_b_a), row3(rg_b_x), row3(rg_lam), row3(rg_conv_b)
    gq3 = row3(jnp.tile(swa_g_q, (1, SWA_HEADS)))
    gk3 = row3(jnp.tile(swa_g_k, (1, SWA_KV)))
    wo = jnp.concatenate([_pad_heads(w_out[:, :GLA_V], 1, GLA_HEADS, GLA_DV, GLA_DVP), w_out[:, GLA_V:]], axis=1)
    wo = wo.astype(BF16)
    gmx3, gmm3, gffn3, gmix3 = row3(g_mem_x), row3(g_mem_m), row3(g_ffn), row3(g_mix)
    mgq3 = row3(jnp.tile(mem_g_q, (1, MEM_HEADS)))
    mgk3 = row3(jnp.tile(mem_g_k, (1, MEM_HEADS)))
    wq = mem_w_q.astype(BF16)
    wkv = mem_w_kv.astype(BF16)
    wmo = mem_w_o.astype(BF16)
    w1 = ffn_w1.astype(BF16)
    w2 = ffn_w2.astype(BF16)
    b256 = _block_ones(SWA_Q, HEAD_DIM)
    b128 = _block_ones(SWA_KVW, HEAD_DIM)
    dup = _dup_matrix()

    dupt = dup.T
    mk_all, mv_all, mvt_all = _mem_kv(mem_prompt.reshape(nb * n_mem, D_MODEL), gmm3, wkv, mgk3, b256, nb)
    mk_all = mk_all.reshape(depth, nb, n_mem, MEM_W)
    mv_all = mv_all.reshape(depth, nb, n_mem, MEM_W)

    xp = x_prompt.reshape(nb * t_len, D_MODEL)
    xs = x_sample.reshape(ns, D_MODEL)
    rg_conv_t = state_rg_conv.transpose(0, 2, 1, 3)
    gla_t = state_gla.transpose(0, 2, 3, 4, 1)
    swk = cache_swa_k.transpose(0, 1, 3, 4, 2).reshape(depth, ns, SWA_KVW, WINDOW)
    swv = cache_swa_v.transpose(0, 1, 3, 4, 2).reshape(depth, ns, SWA_KVW, WINDOW)
    cmk = cache_mem_k.transpose(0, 1, 3, 4, 2).reshape(depth, ns, MEM_W, n_mem)
    cmv = cache_mem_v.transpose(0, 1, 3, 4, 2).reshape(depth, ns, MEM_W, n_mem)
    goutc = gla_g_out.reshape(depth, GLA_DV, 1)
    sbt = min(ns, 32)

    gla_p, rgh_p, rgh_s, rgc_p, rgc_s, swk_p, swv_p = [], [], [], [], [], [], []
    ts = min(ns, 128)
    for l in range(depth):
        xp, st, hl, cbuf, kb, vb = _mixer_prompt(
            swa_sinks, xp.reshape(nb, t_len, D_MODEL), gmix3, w_in_p, wg, bg3, gout3, rg_conv_w, cb3, wa, ba3, wx,
            bx3, lam3, gq3, gk3, b256, b128, dup, dupt, wo, gmx3, wq, mgq3, mk_all, mvt_all, wmo, l)
        xp = _ffn(xp.reshape(nb * t_len, D_MODEL), gffn3, w1, w2, l, ROW_T)
        st = st.reshape(nb, GLA_HEADS, GLA_DVP, GLA_QKP)
        gla_p.append(jnp.stack(
            [st[:, h, :GLA_DV, h * GLA_DK:(h + 1) * GLA_DK].transpose(0, 2, 1) for h in range(GLA_HEADS)], axis=1))
        rgh_p.append(hl.reshape(nb, RG_WIDTH))
        rgc_p.append(cbuf)
        swk_p.append(kb.reshape(nb, WINDOW, SWA_KV, HEAD_DIM))
        swv_p.append(vb.reshape(nb, WINDOW, SWA_KV, HEAD_DIM))

        hs = _in_proj(xs, gmix3, w_in_p, l, ts)
        org_s, hn, cnew, qn, kx, vx, knt, vnt = _sample_rows(
            hs, state_rg_h, rg_conv_t, rg_conv_w, cb3, wa, ba3, wx, bx3, lam3, gq3, gk3, b256, b128, dup, l, sbt)
        og_s, gla_t = _sample_gla(gla_t, hs, wg, bg3, goutc, l)
        osw_s, swk, swv = _sample_swa(swa_sinks, swk, swv, qn, kx, vx, knt, vnt, dupt, l, sbt)
        x1, qm = _sample_post1(xs, og_s, org_s, osw_s, wo, gmx3, wq, mgq3, b256, l)
        om = _sample_mem(qm, cmk, cmv, l, min(ns, 16))
        xs = _sample_ffn(x1, om, wmo, gffn3, w1, w2, l)
        rgh_s.append(hn)
        rgc_s.append(cnew)

    to_cache = lambda a: a.reshape(depth, ns, SWA_KV, HEAD_DIM, WINDOW).transpose(0, 1, 4, 2, 3)
    return (xp.reshape(nb, t_len, D_MODEL), xs.reshape(ns, 1, D_MODEL),
            jnp.stack(gla_p), gla_t.transpose(0, 4, 1, 2, 3), jnp.stack(rgh_p), jnp.stack(rgh_s),
            jnp.stack(rgc_p), jnp.stack(rgc_s).transpose(0, 2, 1, 3), jnp.stack(swk_p), to_cache(swk),
            jnp.stack(swv_p), to_cache(swv),
            mk_all.reshape(depth, nb, n_mem, MEM_HEADS, MEM_HD), mv_all.reshape(depth, nb, n_mem, MEM_HEADS, MEM_HD))
```

```python
import functools
import math

import jax
import jax.numpy as jnp
from jax import lax
from jax.experimental import pallas as pl
from jax.experimental.pallas import tpu as pltpu

F32 = jnp.float32
BF16 = jnp.bfloat16

D_MODEL = 1024
EPS = 1e-6
GLA_HEADS, GLA_DK, GLA_DV, GLA_RANK, GLA_TAU = 4, 48, 96, 16, 16.0
GLA_QK = GLA_HEADS * GLA_DK
GLA_V = GLA_HEADS * GLA_DV
RG_WIDTH, RG_BLOCKS, RG_C, CONV_W = 384, 4, 8.0, 4
RG_BW = RG_WIDTH // RG_BLOCKS
SWA_HEADS, SWA_KV, HEAD_DIM, WINDOW = 4, 2, 64, 128
SWA_G = SWA_HEADS // SWA_KV
SWA_Q = SWA_HEADS * HEAD_DIM
SWA_KVW = SWA_KV * HEAD_DIM
MEM_HEADS, MEM_HD = 4, 64
MEM_W = MEM_HEADS * MEM_HD
D_FF = 4 * D_MODEL

LANE = 128
SUBLANE = 8
GLA_DVP = LANE
GLA_VP = GLA_HEADS * GLA_DVP
GLA_QKP = 2 * LANE
MIX_W = GLA_VP + RG_WIDTH + SWA_Q

C_RX, C_RY, C_GQ, C_GK, C_SQ, C_GV, C_GOG, C_SK, C_SV = 0, 384, 768, 1024, 1280, 1536, 2048, 2560, 2688
NP = 2816
GLR_LANE = GLA_QK

VMEM_LIMIT = 56 * 1024 * 1024
NEG = -1e30

MIX_T = 256
ROW_T = 512
FF_CHUNK = 1024


def _cparams(*sem):
    return pltpu.CompilerParams(dimension_semantics=sem, vmem_limit_bytes=VMEM_LIMIT)


def _rms(x, g):
    ms = jnp.mean(x * x, axis=-1, keepdims=True)
    return (x * lax.rsqrt(ms + EPS)) * g


def _dot(a, b):
    return jnp.dot(a.astype(BF16), b.astype(BF16), preferred_element_type=F32)


def _dot_nt(a, b):
    return lax.dot_general(a.astype(BF16), b.astype(BF16), (((1,), (1,)), ((), ())), preferred_element_type=F32)


def _split2_dot(x, m):
    hi = x.astype(BF16)
    lo = (x - hi.astype(F32)).astype(BF16)
    return jnp.dot(hi, m, preferred_element_type=F32) + jnp.dot(lo, m, preferred_element_type=F32)


def _split3(x):
    x1 = x.astype(BF16)
    r1 = x - x1.astype(F32)
    x2 = r1.astype(BF16)
    x3 = (r1 - x2.astype(F32)).astype(BF16)
    return x1, x2, x3


def _split3_dot_left(m, x):
    d = functools.partial(jnp.dot, preferred_element_type=F32)
    x1, x2, x3 = _split3(x)
    return d(m, x1) + d(m, x2) + d(m, x3)


def _seg_rms(x, bones, seg, g):
    ms = _split2_dot(x * x, bones) * (1.0 / seg)
    return (x * lax.rsqrt(ms + EPS)) * g


def _sigmoid(x):
    return 1.0 / (1.0 + jnp.exp(-x))


def _softplus(x):
    return jnp.maximum(x, 0.0) + jnp.log1p(jnp.exp(-jnp.abs(x)))


def _log_sigmoid(z):
    return -(jnp.maximum(-z, 0.0) + jnp.log(1.0 + jnp.exp(-jnp.abs(z))))


def _gelu_tanh(x):
    return x * (0.5 * (1.0 + jnp.tanh(math.sqrt(2.0 / math.pi) * (x + 0.044715 * (x * x * x)))))


def _lane_head_mask(width, seg, h, dtype=F32):
    lane = lax.broadcasted_iota(jnp.int32, (1, width), 1)
    return ((lane >= seg * h) & (lane < seg * (h + 1))).astype(dtype)


def _softmax_rows(s):
    m = jnp.max(s, axis=-1, keepdims=True)
    e = jnp.exp(s - m)
    return e / jnp.sum(e, axis=-1, keepdims=True)


def _alibi_slope(h):
    return 2.0 ** (-8.0 * (h + 1) / SWA_HEADS)


def _rows(x, starts, size):
    parts = [x[s:s + size] for s in starts]
    return parts[0] if len(parts) == 1 else jnp.concatenate(parts, axis=0)


def _in_proj_body(x_ref, g_ref, w_ref, o_ref):
    xn = _rms(x_ref[...], g_ref[...])
    o_ref[...] = jnp.dot(xn.astype(BF16), w_ref[...], preferred_element_type=F32)


def _in_proj(x2d, g3, w3, layer, tm):
    m = x2d.shape[0]
    return pl.pallas_call(
        _in_proj_body,
        grid=(m // tm,),
        in_specs=[
            pl.BlockSpec((tm, D_MODEL), lambda i: (i, 0)),
            pl.BlockSpec((None, 1, D_MODEL), lambda i: (layer, 0, 0)),
            pl.BlockSpec((None, D_MODEL, NP), lambda i: (layer, 0, 0)),
        ],
        out_specs=pl.BlockSpec((tm, NP), lambda i: (i, 0)),
        out_shape=jax.ShapeDtypeStruct((m, NP), F32),
        compiler_params=_cparams("parallel"),
        name="in_proj",
    )(x2d, g3, w3)


def _ffn_body(x_ref, g_ref, w1_ref, w2_ref, o_ref):
    x = x_ref[...]
    xn = _rms(x, g_ref[...]).astype(BF16)
    acc = x
    for c in range(D_FF // FF_CHUNK):
        h = jnp.dot(xn, w1_ref[:, c * FF_CHUNK:(c + 1) * FF_CHUNK], preferred_element_type=F32)
        h = jnp.square(jnp.maximum(h, 0.0))
        acc = acc + jnp.dot(h.astype(BF16), w2_ref[c * FF_CHUNK:(c + 1) * FF_CHUNK, :], preferred_element_type=F32)
    o_ref[...] = acc


def _ffn(x2d, g3, w1, w2, layer, tm):
    m = x2d.shape[0]
    const = dict(pipeline_mode=pl.Buffered(1))
    return pl.pallas_call(
        _ffn_body,
        grid=(m // tm,),
        in_specs=[
            pl.BlockSpec((tm, D_MODEL), lambda i: (i, 0)),
            pl.BlockSpec((None, 1, D_MODEL), lambda i: (layer, 0, 0)),
            pl.BlockSpec((None, D_MODEL, D_FF), lambda i: (layer, 0, 0), **const),
            pl.BlockSpec((None, D_FF, D_MODEL), lambda i: (layer, 0, 0), **const),
        ],
        out_specs=pl.BlockSpec((tm, D_MODEL), lambda i: (i, 0)),
        out_shape=jax.ShapeDtypeStruct((m, D_MODEL), F32),
        compiler_params=_cparams("parallel"),
        name="ffn",
    )(x2d, g3, w1, w2)


def _mem_kv_body(mem_ref, g_ref, w_ref, gk_ref, bones_ref, k_ref, v_ref, vt_ref):
    kv = jnp.dot(_rms(mem_ref[...], g_ref[...]).astype(BF16), w_ref[...], preferred_element_type=F32)
    k_ref[...] = _seg_rms(kv[:, :MEM_W], bones_ref[...], MEM_HD, gk_ref[...])
    v = kv[:, MEM_W:]
    v_ref[...] = v
    nb, _, n_mem = vt_ref.shape
    for b in range(nb):
        vt_ref[b] = v[b * n_mem:(b + 1) * n_mem].T


def _mem_kv(mem2d, g3, wkv, gk3, bones256, nb):
    depth = wkv.shape[0]
    m = mem2d.shape[0]
    return pl.pallas_call(
        _mem_kv_body,
        grid=(depth,),
        in_specs=[
            pl.BlockSpec((m, D_MODEL), lambda l: (0, 0)),
            pl.BlockSpec((None, 1, D_MODEL), lambda l: (l, 0, 0)),
            pl.BlockSpec((None, D_MODEL, 2 * MEM_W), lambda l: (l, 0, 0)),
            pl.BlockSpec((None, 1, MEM_W), lambda l: (l, 0, 0)),
            pl.BlockSpec((MEM_W, MEM_W), lambda l: (0, 0)),
        ],
        out_specs=[pl.BlockSpec((None, m, MEM_W), lambda l: (l, 0, 0))] * 2
        + [pl.BlockSpec((None, nb, MEM_W, m // nb), lambda l: (l, 0, 0, 0))],
        out_shape=[jax.ShapeDtypeStruct((depth, m, MEM_W), F32)] * 2
        + [jax.ShapeDtypeStruct((depth, nb, MEM_W, m // nb), F32)],
        compiler_params=_cparams("parallel"),
        name="mem_kv",
    )(mem2d, g3, wkv, gk3, bones256)


def _mix_out(x, og, orr, osw, wo_ref):
    y = _dot(og, wo_ref[0:GLA_VP, :])
    y = y + _dot(orr, wo_ref[GLA_VP:GLA_VP + RG_WIDTH, :])
    y = y + _dot(osw, wo_ref[GLA_VP + RG_WIDTH:MIX_W, :])
    return x + y


def _mem_q(x1, gx, wq_ref, gq, bones):
    q = jnp.dot(_rms(x1, gx).astype(BF16), wq_ref[...], preferred_element_type=F32)
    return _seg_rms(q, bones, MEM_HD, gq)


def _gla_tile(h_scr, wg_ref, bg_ref, gout_ref, st_scr, cum_scr, d_scr, oc_scr, st_out):
    tt = h_scr.shape[0]
    hf = tt // 2
    qraw = h_scr[:, C_GQ:C_GQ + GLA_QKP]
    k = h_scr[:, C_GK:C_GK + GLA_QKP]
    z = jnp.dot(qraw.astype(BF16), wg_ref[...], preferred_element_type=F32) + bg_ref[...]
    g = _log_sigmoid(z) * (1.0 / GLA_TAU)
    q = qraw * (GLA_DK ** -0.5)

    row = lax.broadcasted_iota(jnp.int32, (tt, tt), 0)
    col = lax.broadcasted_iota(jnp.int32, (tt, tt), 1)
    cum = _split3_dot_left((row >= col).astype(BF16), g)
    cum_scr[...] = cum
    rowi = lax.broadcasted_iota(jnp.int32, (tt, 1), 0)

    def boundary(b, n):
        return jnp.concatenate(
            [jnp.broadcast_to(cum_scr[pl.ds(gi * 2 * b + b - 1, 1), :], (n, GLA_QKP)) for gi in range(tt // (2 * b))],
            axis=0)

    rd1 = pltpu.roll(cum, 1, 0)
    rd2 = pltpu.roll(cum, 2, 0)
    ru1 = pltpu.roll(cum, tt - 1, 0)
    m4 = rowi & 3
    bounds = {
        1: jnp.where((rowi & 1) == 0, cum, rd1),
        2: jnp.where(m4 == 0, ru1, jnp.where(m4 == 1, cum, jnp.where(m4 == 2, rd1, rd2))),
        4: boundary(4, 2 * 4),
    }
    low = [(q.astype(BF16), k.astype(BF16), 0)]
    for b in (1, 2, 4):
        second = (rowi & (2 * b - 1)) >= b
        c = bounds[b]
        qs = q * jnp.exp(jnp.where(second, cum - c, NEG))
        ks = k * jnp.exp(jnp.where(second, NEG, c - cum))
        low.append((qs.astype(BF16), ks.astype(BF16), int(math.log2(2 * b))))

    mid = []
    b = SUBLANE
    while b <= hf:
        ng = tt // (2 * b)
        c = boundary(b, b)
        firsts = [gi * 2 * b for gi in range(ng)]
        seconds = [gi * 2 * b + b for gi in range(ng)]
        qs2 = _rows(q, seconds, b) * jnp.exp(_rows(cum, seconds, b) - c)
        ks1 = _rows(k, firsts, b) * jnp.exp(c - _rows(cum, firsts, b))
        if b < hf:
            zb = jnp.zeros((b, GLA_QKP), F32)
            ks1 = jnp.concatenate([piece for gi in range(ng) for piece in (ks1[gi * b:(gi + 1) * b], zb)], axis=0)
        mid.append((b, qs2.astype(BF16), ks1.astype(BF16)))
        b *= 2

    st = st_scr[...]
    o_inter = _dot_nt(q * jnp.exp(cum), st)
    last = cum_scr[pl.ds(tt - 1, 1), :]
    kd = k * jnp.exp(last - cum)
    v_all = h_scr[:, C_GV:C_GV + GLA_VP]
    upd = jnp.dot(v_all.T.astype(BF16), kd.astype(BF16), preferred_element_type=F32)
    vrow = lax.broadcasted_iota(jnp.int32, (GLA_VP, 1), 0) // GLA_DVP
    lane = lax.broadcasted_iota(jnp.int32, (1, GLA_QKP), 1)
    khead = ((lane >= GLA_DK).astype(jnp.int32) + (lane >= 2 * GLA_DK).astype(jnp.int32)
             + (lane >= 3 * GLA_DK).astype(jnp.int32) + (lane >= 4 * GLA_DK).astype(jnp.int32))
    st_new = jnp.where(vrow == khead, st * jnp.exp(last) + upd, 0.0)
    st_scr[...] = st_new
    st_out[...] = st_new

    nh = GLA_HEADS
    hq = hf // 2
    c128 = lax.broadcasted_iota(jnp.int32, (1, hf), 1)
    r_low = lax.broadcasted_iota(jnp.int32, (nh * hf, 1), 0) & (hf - 1)
    r_mid = lax.broadcasted_iota(jnp.int32, (nh * hq, 1), 0) & (hq - 1)
    low_masks = [(r_low >> s) == (c128 >> s) for (_, _, s) in low]
    mid_masks = {bb: (r_mid >> int(math.log2(bb))) == (c128 >> int(math.log2(2 * bb)))
                 for (bb, _, _) in mid if 2 * bb < hf}
    nt = (((1,), (1,)), ((), ()))
    hms = [_lane_head_mask(GLA_QKP, GLA_DK, h, BF16) for h in range(nh)]
    stack_heads = lambda xb: jnp.concatenate([xb * hm for hm in hms], axis=0)
    for half in range(2):
        rs = slice(half * hf, (half + 1) * hf)
        acc = None
        for (qb, kb, _), mk in zip(low, low_masks):
            p = lax.dot_general(stack_heads(qb[rs]), kb[rs], nt, preferred_element_type=F32)
            p = jnp.where(mk, p, 0.0)
            acc = p if acc is None else acc + p
        for h in range(nh):
            d_scr[h, half] = acc[h * hf:(h + 1) * hf]
        for bb, qb2, kb1 in mid:
            if bb == hf:
                continue
            cs = slice(half * hq, (half + 1) * hq)
            p = lax.dot_general(stack_heads(qb2[cs]), kb1[rs], nt, preferred_element_type=F32)
            if bb in mid_masks:
                p = jnp.where(mid_masks[bb], p, 0.0)
            for h in range(nh):
                for gi in range(hf // (2 * bb)):
                    dst = pl.ds(gi * 2 * bb + bb, bb)
                    src = slice(h * hq + gi * bb, h * hq + (gi + 1) * bb)
                    d_scr[h, half, dst, :] = d_scr[h, half, dst, :] + p[src, :]
    _, qb2, kb1 = mid[-1]
    off_all = lax.dot_general(stack_heads(qb2), kb1, nt, preferred_element_type=F32)
    gout = gout_ref[...]
    for h in range(nh):
        off = off_all[h * hf:(h + 1) * hf]
        sl = slice(h * GLA_DVP, (h + 1) * GLA_DVP)
        v_h = v_all[:, sl].astype(BF16)
        o0 = jnp.dot(d_scr[h, 0].astype(BF16), v_h[0:hf], preferred_element_type=F32)
        a1 = jnp.concatenate([off, d_scr[h, 1]], axis=1).astype(BF16)
        o1 = jnp.dot(a1, v_h, preferred_element_type=F32)
        o = jnp.concatenate([o0, o1], axis=0) + o_inter[:, sl]
        ms = jnp.sum(o * o, axis=-1, keepdims=True) * (1.0 / GLA_DV)
        gate = h_scr[:, C_GOG + h * GLA_DVP:C_GOG + (h + 1) * GLA_DVP]
        oc_scr[:, sl] = ((o * lax.rsqrt(ms + EPS)) * gout) * (gate * _sigmoid(gate))


def _rg_gates(xc, wa_ref, ba, wx_ref, bx, lam):
    r = _sigmoid(jnp.dot(xc.astype(BF16), wa_ref[...], preferred_element_type=F32) + ba)
    i = _sigmoid(jnp.dot(xc.astype(BF16), wx_ref[...], preferred_element_type=F32) + bx)
    log_a = (-RG_C * r) * _softplus(-lam)
    a = jnp.exp(log_a)
    b = jnp.sqrt((1.0 + a * a) * jnp.tanh(-log_a)) * (i * xc)
    return a, b


def _rg_pitch(tt):
    return tt // SUBLANE + SUBLANE


def _rg_tiles(h_scr, cw_ref, cb_ref, wa_ref, ba_ref, wx_ref, bx_ref, lam_ref, hc_scr, cc_scr, ra_scr, rb_scr, rh_scr,
              rp_scr, oc_scr):
    nb, tt = len(h_scr), h_scr[0].shape[0]
    rowi = lax.broadcasted_iota(jnp.int32, (tt, 1), 0)
    nl = RG_WIDTH // LANE
    seg = tt // SUBLANE
    pitch = _rg_pitch(tt)
    cw = cw_ref[...]
    for bi in range(nb):
        rx = h_scr[bi][:,C_RX:C_RX + RG_WIDTH]
        p0, p1, p2 = cc_scr[bi, 0:1, :], cc_scr[bi, 1:2, :], cc_scr[bi, 2:3, :]
        x1 = jnp.where(rowi == 0, p2, pltpu.roll(rx, 1, 0))
        x2 = jnp.where(rowi == 0, p1, jnp.where(rowi == 1, p2, pltpu.roll(rx, 2, 0)))
        x3 = jnp.where(rowi == 0, p0, jnp.where(rowi == 1, p1, jnp.where(rowi == 2, p2, pltpu.roll(rx, 3, 0))))
        xc = cb_ref[...] + x3 * cw[0:1, :]
        xc = xc + x2 * cw[1:2, :]
        xc = xc + x1 * cw[2:3, :]
        xc = xc + rx * cw[3:4, :]
        a, b = _rg_gates(xc, wa_ref, ba_ref[...], wx_ref, bx_ref[...], lam_ref[...])
        for c in range(nl):
            for j in range(SUBLANE):
                dst = pl.ds(j * pitch, seg)
                ra_scr[bi, c, dst, :] = a[j * seg:(j + 1) * seg, c * LANE:(c + 1) * LANE]
                rb_scr[bi, c, dst, :] = b[j * seg:(j + 1) * seg, c * LANE:(c + 1) * LANE]

    chains = [(bi, c) for bi in range(nb) for c in range(nl)]
    hloc = {ch: jnp.zeros((SUBLANE, LANE), F32) for ch in chains}
    pc = {ch: jnp.ones((SUBLANE, LANE), F32) for ch in chains}
    for i in range(seg):
        idx = pl.ds(i, SUBLANE, stride=pitch)
        for ch in chains:
            bi, c = ch
            ai = ra_scr[bi, c, idx, :]
            hloc[ch] = ai * hloc[ch] + rb_scr[bi, c, idx, :]
            pc[ch] = ai * pc[ch]
            rh_scr[bi, c, idx, :] = hloc[ch]
            rp_scr[bi, c, idx, :] = pc[ch]
    outs = []
    for bi in range(nb):
        hl = jnp.concatenate([hloc[(bi, c)] for c in range(nl)], axis=1)
        pl_ = jnp.concatenate([pc[(bi, c)] for c in range(nl)], axis=1)
        hprev = hc_scr[bi, 0:1, :]
        carries = []
        for j in range(SUBLANE):
            carries.append(jnp.broadcast_to(hprev, (seg, RG_WIDTH)))
            hprev = hl[j:j + 1, :] + pl_[j:j + 1, :] * hprev
        hc_scr[bi, 0:1, :] = hprev
        carry = jnp.concatenate(carries, axis=0)
        unpitch = lambda r, c: jnp.concatenate([r[bi, c, pl.ds(j * pitch, seg), :] for j in range(SUBLANE)], axis=0)
        h = jnp.concatenate(
            [unpitch(rh_scr, c) + unpitch(rp_scr, c) * carry[:, c * LANE:(c + 1) * LANE] for c in range(nl)], axis=1)
        oc_scr[bi][:, GLA_VP:GLA_VP + RG_WIDTH] = h * _gelu_tanh(h_scr[bi][:,C_RY:C_RY + RG_WIDTH])
        tail = h_scr[bi][pl.ds(tt - (CONV_W - 1), CONV_W - 1), C_RX:C_RX + RG_WIDTH]
        cc_scr[bi, 0:CONV_W - 1, :] = tail
        outs.append((hprev, tail))
    return outs


def _stack_heads(x, masks):
    return jnp.concatenate([x * m for m in masks], axis=0)


def _unstack_heads_t(ot, seg, n):
    heads = ot.shape[0] // seg
    blocks = [ot[h * seg:(h + 1) * seg, h * n:(h + 1) * n] for h in range(heads)]
    return jnp.concatenate(blocks, axis=0).T


def _softmax_cols(s, sink=None):
    m = jnp.max(s, axis=0, keepdims=True)
    if sink is not None:
        m = jnp.maximum(m, sink)
    e = jnp.exp(s - m)
    den = jnp.sum(e, axis=0, keepdims=True)
    if sink is not None:
        den = den + jnp.exp(sink - m)
    return e * (1.0 / den)


def _swa_tiles(h_scr, sink_ref, layer, first, gq_ref, gk_ref, b256_ref, b128_ref, dup_ref, dupt_ref, kp_scr, vpt_scr,
               oc_scr):
    nb, tt = len(h_scr), h_scr[0].shape[0]
    w = WINDOW
    nh = SWA_HEADS
    hmb = [_lane_head_mask(SWA_Q, HEAD_DIM, h, BF16) for h in range(nh)]
    kj = lax.broadcasted_iota(jnp.int32, (2 * w, w), 0)
    qi = lax.broadcasted_iota(jnp.int32, (2 * w, w), 1)
    dist_i = qi + w - kj
    in_window = (dist_i >= 0) & (dist_i <= WINDOW)
    dist = dist_i.astype(F32)
    bias = jnp.concatenate([jnp.where(in_window, -_alibi_slope(h) * dist, NEG) for h in range(nh)], axis=1)
    kj4 = lax.broadcasted_iota(jnp.int32, (2 * w, nh * w), 0)
    bias_first = jnp.where(jnp.logical_and(first, kj4 < w), NEG, bias)
    sink_row = jnp.concatenate([jnp.full((1, w), sink_ref[layer, h], F32) for h in range(nh)], axis=1)
    dup = dup_ref[...]
    dupt = dupt_ref[...]
    nt = (((1,), (1,)), ((), ()))
    s_parts, vt_parts, wins = [], [], []
    for bi in range(nb):
        qn = _seg_rms(h_scr[bi][:, C_SQ:C_SQ + SWA_Q], b256_ref[...], HEAD_DIM, gq_ref[...])
        qn = (qn * (HEAD_DIM ** -0.5)).astype(BF16)
        kn = _seg_rms(h_scr[bi][:, C_SK:C_SK + SWA_KVW], b128_ref[...], HEAD_DIM, gk_ref[...])
        v = h_scr[bi][:, C_SV:C_SV + SWA_KVW]
        vt = v.T
        kcat = jnp.concatenate([kp_scr[bi], kn], axis=0).astype(BF16)
        vtcat = jnp.concatenate([vpt_scr[bi], vt], axis=1).astype(BF16)
        kexp = jnp.dot(kcat, dup, preferred_element_type=F32).astype(BF16)
        vexpt = jnp.dot(dupt, vtcat, preferred_element_type=F32).astype(BF16)
        for blk in range(tt // w):
            q4 = _stack_heads(qn[blk * w:(blk + 1) * w], hmb)
            s = lax.dot_general(kexp[blk * w:(blk + 2) * w], q4, nt, preferred_element_type=F32)
            s_parts.append(s + (bias_first if blk == 0 else bias))
            vt_parts.append(vexpt[:, blk * w:(blk + 2) * w])
        k_win = kn[tt - w:tt]
        kp_scr[bi] = k_win
        vpt_scr[bi] = vt[:, tt - w:tt]
        wins.append((k_win, v[tt - w:tt]))
    p_all = _softmax_cols(jnp.concatenate(s_parts, axis=1),
                          jnp.concatenate([sink_row] * len(s_parts), axis=1)).astype(BF16)
    n4 = nh * w
    for i, vtb in enumerate(vt_parts):
        bi, blk = divmod(i, tt // w)
        ot = jnp.dot(vtb, p_all[:, i * n4:(i + 1) * n4], preferred_element_type=F32)
        oc_scr[bi][blk * w:(blk + 1) * w, GLA_VP + RG_WIDTH:MIX_W] = _unstack_heads_t(ot, HEAD_DIM, w)
    return wins


def _mem_attend_tiles(x1s, gx, wq_ref, gq, bones, mk_ref, mvt_ref, wmo_ref):
    nh = MEM_HEADS
    nb, tt = len(x1s), x1s[0].shape[0]
    hmb = [_lane_head_mask(MEM_W, MEM_HD, h, BF16) for h in range(nh)]
    nt = (((1,), (1,)), ((), ()))
    xn = jnp.concatenate([_rms(x1, gx).astype(BF16) for x1 in x1s], axis=0)
    q = jnp.dot(xn, wq_ref[...], preferred_element_type=F32)
    qn = (_seg_rms(q, bones, MEM_HD, gq) * (MEM_HD ** -0.5)).astype(BF16)
    s_parts = [lax.dot_general(mk_ref[bi].astype(BF16), _stack_heads(qn[bi * tt:(bi + 1) * tt], hmb), nt,
                               preferred_element_type=F32) for bi in range(nb)]
    p_all = _softmax_cols(jnp.concatenate(s_parts, axis=1)).astype(BF16)
    n4 = nh * tt
    o = jnp.concatenate(
        [_unstack_heads_t(jnp.dot(mvt_ref[bi].astype(BF16), p_all[:, bi * n4:(bi + 1) * n4],
                                  preferred_element_type=F32), MEM_HD, tt) for bi in range(nb)], axis=0)
    y = jnp.dot(o.astype(BF16), wmo_ref[...], preferred_element_type=F32)
    return [x1s[bi] + y[bi * tt:(bi + 1) * tt] for bi in range(nb)]


def _mixer_body(sink_ref, x_ref, gmix_ref, win_ref, wg_ref, bg_ref, gout_ref, cw_ref, cb_ref, wa_ref,
                ba_ref, wx_ref, bx_ref, lam_ref, gq_ref, gk_ref, b256_ref, b128_ref, dup_ref, dupt_ref, wo_ref,
                gmx_ref, wq_ref, mgq_ref, mk_ref, mv_ref, wmo_ref,
                o_ref, st_ref, hl_ref, cbuf_ref, ko_ref, vo_ref,
                h_scr, oc_scr, st_scr, cum_scr, d_scr, hc_scr, cc_scr, ra_scr, rb_scr, rh_scr, rp_scr, kp_scr, vp_scr,
                *, layer):
    nb, tt = x_ref.shape[0], x_ref.shape[1]
    first = pl.program_id(0) == 0

    @pl.when(first)
    def _():
        for scr in (st_scr, hc_scr, cc_scr, kp_scr, vp_scr):
            scr[...] = jnp.zeros_like(scr)

    xn = jnp.concatenate([_rms(x_ref[bi], gmix_ref[...]).astype(BF16) for bi in range(nb)], axis=0)
    h_scr[...] = jnp.dot(xn, win_ref[...], preferred_element_type=F32)
    hs = [h_scr.at[pl.ds(bi * tt, tt)] for bi in range(nb)]
    ocs = [oc_scr.at[pl.ds(bi * tt, tt)] for bi in range(nb)]
    for bi in range(nb):
        _gla_tile(hs[bi], wg_ref, bg_ref, gout_ref, st_scr.at[bi], cum_scr.at[bi], d_scr.at[bi], ocs[bi],
                  st_ref.at[bi])
    rg_out = _rg_tiles(hs, cw_ref, cb_ref, wa_ref, ba_ref, wx_ref, bx_ref, lam_ref, hc_scr, cc_scr, ra_scr, rb_scr,
                       rh_scr, rp_scr, ocs)
    for bi, (h_last, tail) in enumerate(rg_out):
        hl_ref[bi] = h_last
        cbuf_ref[bi] = tail
    wins = _swa_tiles(hs, sink_ref, layer, first, gq_ref, gk_ref, b256_ref, b128_ref, dup_ref, dupt_ref, kp_scr,
                      vp_scr, ocs)
    for bi, (k_win, v_win) in enumerate(wins):
        ko_ref[bi] = k_win
        vo_ref[bi] = v_win
    y = jnp.dot(oc_scr[...].astype(BF16), wo_ref[...], preferred_element_type=F32)
    x1s = [x_ref[bi] + y[bi * tt:(bi + 1) * tt] for bi in range(nb)]
    x2s = _mem_attend_tiles(x1s, gmx_ref[...], wq_ref, mgq_ref[...], b256_ref[...], mk_ref, mv_ref, wmo_ref)
    for bi in range(nb):
        o_ref[bi] = x2s[bi]


def _mixer_prompt(sinks, x3d, gmix3, w_in_p, wg, bg3, gout3, cw, cb3, wa, ba3, wx, bx3, lam3, gq3, gk3, b256, b128,
                  dup, dupt, wo, gmx3, wq, mgq3, mk, mvt, wmo, layer):
    nb, t_len, _ = x3d.shape
    tt = MIX_T
    nt = t_len // tt
    n_mem = mk.shape[2]
    lay3 = lambda t: (layer, 0, 0)
    once = dict(pipeline_mode=pl.Buffered(1))
    vec = lambda w: pl.BlockSpec((None, 1, w), lay3)
    full = lambda a: pl.BlockSpec(a.shape, lambda t: (0,) * a.ndim)
    state = lambda r, w: pl.BlockSpec((nb, r, w), lambda t: (0, 0, 0))
    mem = pl.BlockSpec((None, nb, n_mem, MEM_W), lambda t: (layer, 0, 0, 0))
    memt = pl.BlockSpec((None, nb, MEM_W, n_mem), lambda t: (layer, 0, 0, 0))
    return pl.pallas_call(
        functools.partial(_mixer_body, layer=layer),
        grid=(nt,),
        in_specs=[
            pl.BlockSpec(memory_space=pltpu.SMEM),
            pl.BlockSpec((nb, tt, D_MODEL), lambda t: (0, t, 0)),
            vec(D_MODEL),
            pl.BlockSpec((None, D_MODEL, NP), lay3, **once),
            pl.BlockSpec((None, GLA_QKP, GLA_QKP), lay3), vec(GLA_QKP), vec(GLA_DVP),
            pl.BlockSpec((None, CONV_W, RG_WIDTH), lay3), vec(RG_WIDTH),
            pl.BlockSpec((None, RG_WIDTH, RG_WIDTH), lay3), vec(RG_WIDTH),
            pl.BlockSpec((None, RG_WIDTH, RG_WIDTH), lay3), vec(RG_WIDTH), vec(RG_WIDTH),
            vec(SWA_Q), vec(SWA_KVW), full(b256), full(b128), full(dup), full(dupt),
            pl.BlockSpec((None, MIX_W, D_MODEL), lay3, **once),
            vec(D_MODEL), pl.BlockSpec((None, D_MODEL, MEM_W), lay3), vec(MEM_W),
            mem, memt,
            pl.BlockSpec((None, MEM_W, D_MODEL), lay3),
        ],
        out_specs=[
            pl.BlockSpec((nb, tt, D_MODEL), lambda t: (0, t, 0)),
            state(GLA_VP, GLA_QKP), state(1, RG_WIDTH), state(CONV_W - 1, RG_WIDTH),
            state(WINDOW, SWA_KVW), state(WINDOW, SWA_KVW),
        ],
        out_shape=[
            jax.ShapeDtypeStruct((nb, t_len, D_MODEL), F32),
            jax.ShapeDtypeStruct((nb, GLA_VP, GLA_QKP), F32),
            jax.ShapeDtypeStruct((nb, 1, RG_WIDTH), F32),
            jax.ShapeDtypeStruct((nb, CONV_W - 1, RG_WIDTH), F32),
            jax.ShapeDtypeStruct((nb, WINDOW, SWA_KVW), F32),
            jax.ShapeDtypeStruct((nb, WINDOW, SWA_KVW), F32),
        ],
        scratch_shapes=[
            pltpu.VMEM((nb * tt, NP), F32), pltpu.VMEM((nb * tt, MIX_W), F32),
            pltpu.VMEM((nb, GLA_VP, GLA_QKP), F32), pltpu.VMEM((nb, tt, GLA_QKP), F32),
            pltpu.VMEM((nb, GLA_HEADS, 2, tt // 2, tt // 2), F32),
            pltpu.VMEM((nb, SUBLANE, RG_WIDTH), F32), pltpu.VMEM((nb, SUBLANE, RG_WIDTH), F32),
            *[pltpu.VMEM((nb, RG_WIDTH // LANE, SUBLANE * _rg_pitch(tt), LANE), F32) for _ in range(4)],
            pltpu.VMEM((nb, WINDOW, SWA_KVW), F32), pltpu.VMEM((nb, WINDOW, SWA_KVW), F32),
        ],
        compiler_params=_cparams("arbitrary"),
        name="mixer_prompt",
    )(sinks, x3d, gmix3, w_in_p, wg, bg3, gout3, cw, cb3, wa, ba3, wx, bx3, lam3, gq3, gk3, b256, b128, dup, dupt,
      wo, gmx3, wq, mgq3, mk, mvt, wmo)


def _sample_rows_body(h_ref, hst_ref, cbuf_ref, cw_ref, cb_ref, wa_ref, ba_ref, wx_ref, bx_ref, lam_ref, gq_ref,
                      gk_ref, b256_ref, b128_ref, dup_ref,
                      org_ref, hn_ref, cnew_ref, qn_ref, kx_ref, vx_ref, knt_ref, vnt_ref):
    rx = h_ref[:, C_RX:C_RX + RG_WIDTH]
    cw = cw_ref[...]
    xc = cb_ref[...] + cbuf_ref[0] * cw[0:1, :]
    xc = xc + cbuf_ref[1] * cw[1:2, :]
    xc = xc + cbuf_ref[2] * cw[2:3, :]
    xc = xc + rx * cw[3:4, :]
    a, b = _rg_gates(xc, wa_ref, ba_ref[...], wx_ref, bx_ref[...], lam_ref[...])
    hn = a * hst_ref[...] + b
    hn_ref[...] = hn
    org_ref[...] = hn * _gelu_tanh(h_ref[:, C_RY:C_RY + RG_WIDTH])
    cnew_ref[0] = cbuf_ref[1]
    cnew_ref[1] = cbuf_ref[2]
    cnew_ref[2] = rx

    qn_ref[...] = _seg_rms(h_ref[:, C_SQ:C_SQ + SWA_Q], b256_ref[...], HEAD_DIM, gq_ref[...])
    kn = _seg_rms(h_ref[:, C_SK:C_SK + SWA_KVW], b128_ref[...], HEAD_DIM, gk_ref[...])
    v = h_ref[:, C_SV:C_SV + SWA_KVW]
    kx_ref[...] = jnp.dot(kn.astype(BF16), dup_ref[...], preferred_element_type=F32)
    vx_ref[...] = jnp.dot(v.astype(BF16), dup_ref[...], preferred_element_type=F32)
    knt = kn.T
    vnt = v.T
    nblk, _, bt = knt_ref.shape
    for i in range(nblk):
        knt_ref[i] = knt[:, i * bt:(i + 1) * bt]
        vnt_ref[i] = vnt[:, i * bt:(i + 1) * bt]


def _sample_rows(hs, hst, cbuf, cw, cb3, wa, ba3, wx, bx3, lam3, gq3, gk3, b256, b128, dup, layer, bt):
    n = hs.shape[0]
    lay3 = lambda i: (layer, 0, 0)
    full = lambda a: pl.BlockSpec(a.shape, lambda i: (0,) * a.ndim)
    vec = lambda w: pl.BlockSpec((None, 1, w), lay3)
    sq = lambda w: pl.BlockSpec((None, w, w), lay3)
    taps = pl.BlockSpec((None, CONV_W - 1, n, RG_WIDTH), lambda i: (layer, 0, 0, 0))
    shapes = [(n, RG_WIDTH), (n, RG_WIDTH), (CONV_W - 1, n, RG_WIDTH), (n, SWA_Q), (n, SWA_Q), (n, SWA_Q),
              (n // bt, SWA_KVW, bt), (n // bt, SWA_KVW, bt)]
    return pl.pallas_call(
        _sample_rows_body,
        grid=(1,),
        in_specs=[
            full(hs), pl.BlockSpec((None, n, RG_WIDTH), lay3), taps,
            pl.BlockSpec((None, CONV_W, RG_WIDTH), lay3), vec(RG_WIDTH),
            sq(RG_WIDTH), vec(RG_WIDTH), sq(RG_WIDTH), vec(RG_WIDTH), vec(RG_WIDTH),
            vec(SWA_Q), vec(SWA_KVW), full(b256), full(b128), full(dup),
        ],
        out_specs=[pl.BlockSpec(s, lambda i, nd=len(s): (0,) * nd) for s in shapes],
        out_shape=[jax.ShapeDtypeStruct(s, F32) for s in shapes],
        compiler_params=_cparams("arbitrary"),
        name="sample_rows",
    )(hs, hst, cbuf, cw, cb3, wa, ba3, wx, bx3, lam3, gq3, gk3, b256, b128, dup)


def _sample_gla_body(s0_ref, h_ref, wg_ref, bg_ref, goutc_ref, o_ref, sn_ref, qt_scr, kt_scr, et_scr, vt_scr, gt_scr):
    h = pl.program_id(0)
    qraw = h_ref[:, C_GQ:C_GQ + GLA_QKP]
    z = jnp.dot(qraw.astype(BF16), wg_ref[...], preferred_element_type=F32) + bg_ref[...]
    et_scr[...] = jnp.exp(_log_sigmoid(z) * (1.0 / GLA_TAU)).T
    qt_scr[...] = (qraw * (GLA_DK ** -0.5)).T
    kt_scr[...] = h_ref[:, C_GK:C_GK + GLA_QKP].T
    vt_scr[...] = h_ref[:, C_GV:C_GV + GLA_VP].T
    gt_scr[...] = h_ref[:, C_GOG:C_GOG + GLA_VP].T
    r0 = pl.multiple_of(h * GLA_DK, SUBLANE)
    v0 = pl.multiple_of(h * GLA_DVP, GLA_DVP)
    vt = vt_scr[pl.ds(v0, GLA_DV), :]
    o = jnp.zeros(vt.shape, F32)
    for k in range(GLA_DK):
        sn = et_scr[pl.ds(r0 + k, 1), :] * s0_ref[k] + kt_scr[pl.ds(r0 + k, 1), :] * vt
        sn_ref[k] = sn
        o = o + qt_scr[pl.ds(r0 + k, 1), :] * sn
    ms = jnp.sum(o * o, axis=0, keepdims=True) * (1.0 / GLA_DV)
    gate = gt_scr[pl.ds(v0, GLA_DV), :]
    y = ((o * lax.rsqrt(ms + EPS)) * goutc_ref[...]) * (gate * _sigmoid(gate))
    y = jnp.concatenate([y, jnp.zeros((GLA_DVP - GLA_DV, y.shape[1]), F32)], axis=0)
    o_ref[...] = y.T


def _sample_gla(s0t, hs, wg, bg3, goutc, layer):
    n = hs.shape[0]
    lay3 = lambda h: (layer, 0, 0)
    return pl.pallas_call(
        _sample_gla_body,
        grid=(GLA_HEADS,),
        in_specs=[
            pl.BlockSpec((None, None, GLA_DK, GLA_DV, n), lambda h: (layer, h, 0, 0, 0)),
            pl.BlockSpec((n, NP), lambda h: (0, 0)),
            pl.BlockSpec((None, GLA_QKP, GLA_QKP), lay3),
            pl.BlockSpec((None, 1, GLA_QKP), lay3),
            pl.BlockSpec((None, GLA_DV, 1), lay3),
        ],
        out_specs=[pl.BlockSpec((n, GLA_DVP), lambda h: (0, h)),
                   pl.BlockSpec((None, None, GLA_DK, GLA_DV, n), lambda h: (layer, h, 0, 0, 0))],
        out_shape=[jax.ShapeDtypeStruct((n, GLA_VP), F32), jax.ShapeDtypeStruct(s0t.shape, F32)],
        scratch_shapes=[pltpu.VMEM((GLA_QKP, n), F32)] * 3 + [pltpu.VMEM((GLA_VP, n), F32)] * 2,
        input_output_aliases={0: 1},
        compiler_params=_cparams("arbitrary"),
        name="sample_gla",
    )(s0t, hs, wg, bg3, goutc)


def _sample_swa_body(sink_ref, kct_ref, vct_ref, qn_ref, kx_ref, vx_ref, knt_ref, vnt_ref, dupt_ref, o_ref, kot_ref,
                     vot_ref, *, layer):
    bt = kct_ref.shape[0]
    rp = SUBLANE
    dupt = dupt_ref[...]
    hi = lax.broadcasted_iota(jnp.int32, (rp, SWA_Q), 0)
    li = lax.broadcasted_iota(jnp.int32, (rp, SWA_Q), 1)
    hm = ((li >= hi * HEAD_DIM) & (li < (hi + 1) * HEAD_DIM)).astype(F32)
    hrow = lax.broadcasted_iota(jnp.int32, (rp, 1), 0)
    slope = jnp.zeros((rp, 1), F32)
    sink = jnp.zeros((rp, 1), F32)
    for h in range(SWA_HEADS):
        slope = jnp.where(hrow == h, _alibi_slope(h), slope)
        sink = jnp.where(hrow == h, sink_ref[layer, h], sink)
    tile = lambda a: jnp.concatenate([a] * bt, axis=0)
    dist = (WINDOW - lax.broadcasted_iota(jnp.int32, (1, WINDOW), 1)).astype(F32)
    last = lax.broadcasted_iota(jnp.int32, (SWA_KVW, WINDOW), 1) == WINDOW - 1
    nt = (((1,), (1,)), ((), ()))
    kexp = [jnp.dot(dupt, kct_ref[j].astype(BF16), preferred_element_type=F32).astype(BF16) for j in range(bt)]
    vexp = [jnp.dot(dupt, vct_ref[j].astype(BF16), preferred_element_type=F32).astype(BF16) for j in range(bt)]
    q = [(qn_ref[j:j + 1, :] * hm).astype(BF16) for j in range(bt)]
    s = jnp.concatenate([jnp.dot(q[j], kexp[j], preferred_element_type=F32) for j in range(bt)], axis=0)
    s = s * (HEAD_DIM ** -0.5) - tile(slope) * dist
    kx = jnp.concatenate([jnp.broadcast_to(kx_ref[j:j + 1, :], (rp, SWA_Q)) for j in range(bt)], axis=0)
    s_new = jnp.sum(jnp.concatenate(q, axis=0).astype(F32) * kx, axis=-1, keepdims=True) * (HEAD_DIM ** -0.5)
    sink = tile(sink)
    m = jnp.maximum(jnp.maximum(jnp.max(s, axis=-1, keepdims=True), s_new), sink)
    e = jnp.exp(s - m)
    e_new = jnp.exp(s_new - m)
    den = jnp.sum(e, axis=-1, keepdims=True) + e_new + jnp.exp(sink - m)
    p = (e / den).astype(BF16)
    p_new = (e_new / den).astype(BF16).astype(F32)
    for j in range(bt):
        rows = slice(j * rp, (j + 1) * rp)
        o = lax.dot_general(p[rows], vexp[j], nt, preferred_element_type=F32) + p_new[rows] * vx_ref[j:j + 1, :]
        o_ref[j:j + 1, :] = jnp.sum(o * hm, axis=0, keepdims=True)
        kot_ref[j] = jnp.where(last, knt_ref[:, j:j + 1], pltpu.roll(kct_ref[j], WINDOW - 1, 1))
        vot_ref[j] = jnp.where(last, vnt_ref[:, j:j + 1], pltpu.roll(vct_ref[j], WINDOW - 1, 1))


def _sample_swa(sinks, kct, vct, qn, kx, vx, knt, vnt, dupt, layer, bt):
    n = qn.shape[0]
    row = lambda w: pl.BlockSpec((bt, w), lambda i: (i, 0))
    cache = pl.BlockSpec((None, bt, SWA_KVW, WINDOW), lambda i: (layer, i, 0, 0))
    col = pl.BlockSpec((None, SWA_KVW, bt), lambda i: (i, 0, 0))
    return pl.pallas_call(
        functools.partial(_sample_swa_body, layer=layer),
        grid=(n // bt,),
        in_specs=[
            pl.BlockSpec(memory_space=pltpu.SMEM), cache, cache,
            row(SWA_Q), row(SWA_Q), row(SWA_Q), col, col,
            pl.BlockSpec(dupt.shape, lambda i: (0, 0)),
        ],
        out_specs=[row(SWA_Q), cache, cache],
        out_shape=[jax.ShapeDtypeStruct((n, SWA_Q), F32), jax.ShapeDtypeStruct(kct.shape, F32),
                   jax.ShapeDtypeStruct(vct.shape, F32)],
        input_output_aliases={1: 1, 2: 2},
        compiler_params=_cparams("parallel"),
        name="sample_swa",
    )(sinks, kct, vct, qn, kx, vx, knt, vnt, dupt)


def _sample_post1_body(x_ref, og_ref, or_ref, os_ref, wo_ref, gx_ref, wq_ref, gq_ref, bones_ref, x1_ref, qn_ref):
    x1 = _mix_out(x_ref[...], og_ref[...], or_ref[...], os_ref[...], wo_ref)
    x1_ref[...] = x1
    qn_ref[...] = _mem_q(x1, gx_ref[...], wq_ref, gq_ref[...], bones_ref[...])


def _sample_post1(x2d, og, orr, osw, wo, gx3, wq, gq3, bones256, layer):
    n = x2d.shape[0]
    lay3 = lambda i: (layer, 0, 0)
    full = lambda a: pl.BlockSpec(a.shape, lambda i: (0,) * a.ndim)
    return pl.pallas_call(
        _sample_post1_body,
        grid=(1,),
        in_specs=[
            full(x2d), full(og), full(orr), full(osw),
            pl.BlockSpec((None, MIX_W, D_MODEL), lay3),
            pl.BlockSpec((None, 1, D_MODEL), lay3),
            pl.BlockSpec((None, D_MODEL, MEM_W), lay3),
            pl.BlockSpec((None, 1, MEM_W), lay3),
            full(bones256),
        ],
        out_specs=[pl.BlockSpec((n, D_MODEL), lambda i: (0, 0)), pl.BlockSpec((n, MEM_W), lambda i: (0, 0))],
        out_shape=[jax.ShapeDtypeStruct((n, D_MODEL), F32), jax.ShapeDtypeStruct((n, MEM_W), F32)],
        compiler_params=_cparams("arbitrary"),
        name="sample_post1",
    )(x2d, og, orr, osw, wo, gx3, wq, gq3, bones256)


def _sample_mem_body(qn_ref, mk_ref, mv_ref, o_ref):
    bt = mk_ref.shape[0]
    rp = SUBLANE
    hi = lax.broadcasted_iota(jnp.int32, (rp, MEM_W), 0)
    li = lax.broadcasted_iota(jnp.int32, (rp, MEM_W), 1)
    hm = ((li >= hi * MEM_HD) & (li < (hi + 1) * MEM_HD)).astype(F32)
    s = jnp.concatenate(
        [jnp.dot((qn_ref[j:j + 1, :] * hm).astype(BF16), mk_ref[j].astype(BF16), preferred_element_type=F32)
         for j in range(bt)], axis=0)
    p = _softmax_rows(s * (MEM_HD ** -0.5)).astype(BF16)
    for j in range(bt):
        o = lax.dot_general(p[j * rp:(j + 1) * rp], mv_ref[j].astype(BF16), (((1,), (1,)), ((), ())),
                            preferred_element_type=F32)
        o_ref[j:j + 1, :] = jnp.sum(o * hm, axis=0, keepdims=True)


def _sample_mem(qn, mk, mv, layer, bt):
    n = qn.shape[0]
    n_mem = mk.shape[3]
    cache = pl.BlockSpec((None, bt, MEM_W, n_mem), lambda i: (layer, i, 0, 0))
    return pl.pallas_call(
        _sample_mem_body,
        grid=(n // bt,),
        in_specs=[pl.BlockSpec((bt, MEM_W), lambda i: (i, 0)), cache, cache],
        out_specs=pl.BlockSpec((bt, MEM_W), lambda i: (i, 0)),
        out_shape=jax.ShapeDtypeStruct((n, MEM_W), F32),
        compiler_params=_cparams("parallel"),
        name="sample_mem",
    )(qn, mk, mv)


def _ffn_res_body(x_ref, o_ref_in, wmo_ref, g_ref, w1_ref, w2_ref, out_ref):
    x = x_ref[...] + jnp.dot(o_ref_in[...].astype(BF16), wmo_ref[...], preferred_element_type=F32)
    xn = _rms(x, g_ref[...]).astype(BF16)
    acc = x
    for c in range(D_FF // FF_CHUNK):
        h = jnp.dot(xn, w1_ref[:, c * FF_CHUNK:(c + 1) * FF_CHUNK], preferred_element_type=F32)
        h = jnp.square(jnp.maximum(h, 0.0))
        acc = acc + jnp.dot(h.astype(BF16), w2_ref[c * FF_CHUNK:(c + 1) * FF_CHUNK, :], preferred_element_type=F32)
    out_ref[...] = acc


def _sample_ffn(x1, o, wmo, g3, w1, w2, layer):
    n = x1.shape[0]
    lay3 = lambda i: (layer, 0, 0)
    const = dict(pipeline_mode=pl.Buffered(1))
    return pl.pallas_call(
        _ffn_res_body,
        grid=(1,),
        in_specs=[
            pl.BlockSpec((n, D_MODEL), lambda i: (0, 0)),
            pl.BlockSpec((n, MEM_W), lambda i: (0, 0)),
            pl.BlockSpec((None, MEM_W, D_MODEL), lay3),
            pl.BlockSpec((None, 1, D_MODEL), lay3),
            pl.BlockSpec((None, D_MODEL, D_FF), lay3, **const),
            pl.BlockSpec((None, D_FF, D_MODEL), lay3, **const),
        ],
        out_specs=pl.BlockSpec((n, D_MODEL), lambda i: (0, 0)),
        out_shape=jax.ShapeDtypeStruct((n, D_MODEL), F32),
        compiler_params=_cparams("arbitrary"),
        name="sample_ffn",
    )(x1, o, wmo, g3, w1, w2)


def _pad_heads(a, axis, heads, width, padded):
    shp = a.shape
    a = a.reshape(shp[:axis] + (heads, width) + shp[axis + 1:])
    pad = [(0, 0)] * a.ndim
    pad[axis + 1] = (0, padded - width)
    a = jnp.pad(a, pad)
    return a.reshape(shp[:axis] + (heads * padded,) + shp[axis + 1:])


def _w_in_pieces(w):
    offs = [0]
    for n in (GLA_QK, GLA_QK, GLA_V, GLA_RANK, GLA_V, RG_WIDTH, RG_WIDTH, SWA_Q, SWA_KVW, SWA_KVW):
        offs.append(offs[-1] + n)
    gq, gk, gv, glr, gog, rx, ry, sq, sk, sv = [w[..., offs[i]:offs[i + 1]] for i in range(10)]
    zeros = lambda n: jnp.zeros(w.shape[:-1] + (n,), w.dtype)

    def heads(a):
        out = []
        for h in range(GLA_HEADS):
            out += [a[..., h * GLA_DV:(h + 1) * GLA_DV], zeros(GLA_DVP - GLA_DV)]
        return out

    return ([rx, ry, gq, glr, zeros(GLA_QKP - GLA_QK - GLA_RANK), gk, zeros(GLA_QKP - GLA_QK), sq]
            + heads(gv) + heads(gog) + [sk, sv])


def _prep_w_in_body(w_ref, o_ref):
    o_ref[...] = jnp.concatenate(_w_in_pieces(w_ref[...]), axis=-1).astype(BF16)


def _prep_w_in(w_in):
    depth, d, width = w_in.shape
    tr = 256
    return pl.pallas_call(
        _prep_w_in_body,
        grid=(depth, d // tr),
        in_specs=[pl.BlockSpec((None, tr, width), lambda l, r: (l, r, 0))],
        out_specs=pl.BlockSpec((None, tr, NP), lambda l, r: (l, r, 0)),
        out_shape=jax.ShapeDtypeStruct((depth, d, NP), BF16),
        compiler_params=_cparams("parallel", "parallel"),
        name="prep_w_in",
    )(w_in)


def _block_diag(w):
    depth, nb, bw, _ = w.shape
    eye = jnp.eye(nb, dtype=w.dtype)
    return jnp.einsum("lncd,nm->lncmd", w, eye).reshape(depth, nb * bw, nb * bw)


def _block_ones(n, seg):
    i = jnp.arange(n) // seg
    return (i[:, None] == i[None, :]).astype(BF16)


def _dup_matrix():
    src = jnp.arange(SWA_Q)
    src = (src // HEAD_DIM // SWA_G) * HEAD_DIM + src % HEAD_DIM
    return (jnp.arange(SWA_KVW)[:, None] == src[None, :]).astype(BF16)


def kernel(x_prompt, x_sample, mem_prompt, state_gla, state_rg_h, state_rg_conv, cache_swa_k, cache_swa_v, cache_mem_k, cache_mem_v, g_mix, w_in, gla_w_gate2, gla_b_gate, gla_g_out, rg_conv_w, rg_conv_b, rg_w_a, rg_b_a, rg_w_x, rg_b_x, rg_lam, swa_g_q, swa_g_k, swa_sinks, w_out, g_mem_x, g_mem_m, mem_w_q, mem_w_kv, mem_g_q, mem_g_k, mem_w_o, g_ffn, ffn_w1, ffn_w2):
    depth = w_in.shape[0]
    nb, t_len, _ = x_prompt.shape
    ns = x_sample.shape[0]
    n_mem = mem_prompt.shape[1]
    assert t_len % ROW_T == 0 and t_len % MIX_T == 0 and x_sample.shape[1] == 1 and ns % 8 == 0
    row3 = lambda a: a.reshape(depth, 1, a.shape[-1])

    w_in_p = _prep_w_in(w_in)
    wg = jnp.zeros((depth, GLA_QKP, GLA_QKP), F32).at[:, GLR_LANE:GLR_LANE + GLA_RANK, :GLA_QK].set(gla_w_gate2)
    wg = wg.astype(BF16)
    bg3 = row3(jnp.pad(gla_b_gate, ((0, 0), (0, GLA_QKP - GLA_QK))))
    gout3 = row3(jnp.pad(gla_g_out, ((0, 0), (0, GLA_DVP - GLA_DV))))
    wa = _block_diag(rg_w_a).astype(BF16)
    wx = _block_diag(rg_w_x).astype(BF16)
    ba3, bx3, lam3, cb3 = row3(rg_b_a), row3(rg_b_x), row3(rg_lam), row3(rg_conv_b)
    gq3 = row3(jnp.tile(swa_g_q, (1, SWA_HEADS)))
    gk3 = row3(jnp.tile(swa_g_k, (1, SWA_KV)))
    wo = jnp.concatenate([_pad_heads(w_out[:, :GLA_V], 1, GLA_HEADS, GLA_DV, GLA_DVP), w_out[:, GLA_V:]], axis=1)
    wo = wo.astype(BF16)
    gmx3, gmm3, gffn3, gmix3 = row3(g_mem_x), row3(g_mem_m), row3(g_ffn), row3(g_mix)
    mgq3 = row3(jnp.tile(mem_g_q, (1, MEM_HEADS)))
    mgk3 = row3(jnp.tile(mem_g_k, (1, MEM_HEADS)))
    wq = mem_w_q.astype(BF16)
    wkv = mem_w_kv.astype(BF16)
    wmo = mem_w_o.astype(BF16)
    w1 = ffn_w1.astype(BF16)
    w2 = ffn_w2.astype(BF16)
    b256 = _block_ones(SWA_Q, HEAD_DIM)
    b128 = _block_ones(SWA_KVW, HEAD_DIM)
    dup = _dup_matrix()

    dupt = dup.T
    mk_all, mv_all, mvt_all = _mem_kv(mem_prompt.reshape(nb * n_mem, D_MODEL), gmm3, wkv, mgk3, b256, nb)
    mk_all = mk_all.reshape(depth, nb, n_mem, MEM_W)
    mv_all = mv_all.reshape(depth, nb, n_mem, MEM_W)

    xp = x_prompt.reshape(nb * t_len, D_MODEL)
    xs = x_sample.reshape(ns, D_MODEL)
    rg_conv_t = state_rg_conv.transpose(0, 2, 1, 3)
    gla_t = state_gla.transpose(0, 2, 3, 4, 1)
    swk = cache_swa_k.transpose(0, 1, 3, 4, 2).reshape(depth, ns, SWA_KVW, WINDOW)
    swv = cache_swa_v.transpose(0, 1, 3, 4, 2).reshape(depth, ns, SWA_KVW, WINDOW)
    cmk = cache_mem_k.transpose(0, 1, 3, 4, 2).reshape(depth, ns, MEM_W, n_mem)
    cmv = cache_mem_v.transpose(0, 1, 3, 4, 2).reshape(depth, ns, MEM_W, n_mem)
    goutc = gla_g_out.reshape(depth, GLA_DV, 1)
    sbt = min(ns, 32)

    gla_p, rgh_p, rgh_s, rgc_p, rgc_s, swk_p, swv_p = [], [], [], [], [], [], []
    ts = min(ns, 128)
    for l in range(depth):
        xp, st, hl, cbuf, kb, vb = _mixer_prompt(
            swa_sinks, xp.reshape(nb, t_len, D_MODEL), gmix3, w_in_p, wg, bg3, gout3, rg_conv_w, cb3, wa, ba3, wx,
            bx3, lam3, gq3, gk3, b256, b128, dup, dupt, wo, gmx3, wq, mgq3, mk_all, mvt_all, wmo, l)
        xp = _ffn(xp.reshape(nb * t_len, D_MODEL), gffn3, w1, w2, l, ROW_T)
        st = st.reshape(nb, GLA_HEADS, GLA_DVP, GLA_QKP)
        gla_p.append(jnp.stack(
            [st[:, h, :GLA_DV, h * GLA_DK:(h + 1) * GLA_DK].transpose(0, 2, 1) for h in range(GLA_HEADS)], axis=1))
        rgh_p.append(hl.reshape(nb, RG_WIDTH))
        rgc_p.append(cbuf)
        swk_p.append(kb.reshape(nb, WINDOW, SWA_KV, HEAD_DIM))
        swv_p.append(vb.reshape(nb, WINDOW, SWA_KV, HEAD_DIM))

        hs = _in_proj(xs, gmix3, w_in_p, l, ts)
        org_s, hn, cnew, qn, kx, vx, knt, vnt = _sample_rows(
            hs, state_rg_h, rg_conv_t, rg_conv_w, cb3, wa, ba3, wx, bx3, lam3, gq3, gk3, b256, b128, dup, l, sbt)
        og_s, gla_t = _sample_gla(gla_t, hs, wg, bg3, goutc, l)
        osw_s, swk, swv = _sample_swa(swa_sinks, swk, swv, qn, kx, vx, knt, vnt, dupt, l, sbt)
        x1, qm = _sample_post1(xs, og_s, org_s, osw_s, wo, gmx3, wq, mgq3, b256, l)
        om = _sample_mem(qm, cmk, cmv, l, min(ns, 16))
        xs = _sample_ffn(x1, om, wmo, gffn3, w1, w2, l)
        rgh_s.append(hn)
        rgc_s.append(cnew)

    to_cache = lambda a: a.reshape(depth, ns, SWA_KV, HEAD_DIM, WINDOW).transpose(0, 1, 4, 2, 3)
    return (xp.reshape(nb, t_len, D_MODEL), xs.reshape(ns, 1, D_MODEL),
            jnp.stack(gla_p), gla_t.transpose(0, 4, 1, 2, 3), jnp.stack(rgh_p), jnp.stack(rgh_s),
            jnp.stack(rgc_p), jnp.stack(rgc_s).transpose(0, 2, 1, 3), jnp.stack(swk_p), to_cache(swk),
            jnp.stack(swv_p), to_cache(swv),
            mk_all.reshape(depth, nb, n_mem, MEM_HEADS, MEM_HD), mv_all.reshape(depth, nb, n_mem, MEM_HEADS, MEM_HD))
```

```python
import functools
import math

import jax
import jax.numpy as jnp
from jax import lax
from jax.experimental import pallas as pl
from jax.experimental.pallas import tpu as pltpu

F32 = jnp.float32
BF16 = jnp.bfloat16

D_MODEL = 1024
EPS = 1e-6
GLA_HEADS, GLA_DK, GLA_DV, GLA_RANK, GLA_TAU = 4, 48, 96, 16, 16.0
GLA_QK = GLA_HEADS * GLA_DK
GLA_V = GLA_HEADS * GLA_DV
RG_WIDTH, RG_BLOCKS, RG_C, CONV_W = 384, 4, 8.0, 4
RG_BW = RG_WIDTH // RG_BLOCKS
SWA_HEADS, SWA_KV, HEAD_DIM, WINDOW = 4, 2, 64, 128
SWA_G = SWA_HEADS // SWA_KV
SWA_Q = SWA_HEADS * HEAD_DIM
SWA_KVW = SWA_KV * HEAD_DIM
MEM_HEADS, MEM_HD = 4, 64
MEM_W = MEM_HEADS * MEM_HD
D_FF = 4 * D_MODEL

LANE = 128
SUBLANE = 8
GLA_DVP = LANE
GLA_VP = GLA_HEADS * GLA_DVP
GLA_QKP = 2 * LANE
MIX_W = GLA_VP + RG_WIDTH + SWA_Q

C_RX, C_RY, C_GQ, C_GK, C_SQ, C_GV, C_GOG, C_SK, C_SV = 0, 384, 768, 1024, 1280, 1536, 2048, 2560, 2688
NP = 2816
GLR_LANE = GLA_QK

VMEM_LIMIT = 56 * 1024 * 1024
NEG = -1e30

MIX_T = 256
ROW_T = 512
FF_CHUNK = 1024


def _cparams(*sem):
    return pltpu.CompilerParams(dimension_semantics=sem, vmem_limit_bytes=VMEM_LIMIT)


def _rms(x, g):
    ms = jnp.mean(x * x, axis=-1, keepdims=True)
    return (x * lax.rsqrt(ms + EPS)) * g


def _dot(a, b):
    return jnp.dot(a.astype(BF16), b.astype(BF16), preferred_element_type=F32)


def _dot_nt(a, b):
    return lax.dot_general(a.astype(BF16), b.astype(BF16), (((1,), (1,)), ((), ())), preferred_element_type=F32)


def _split2_dot(x, m):
    hi = x.astype(BF16)
    lo = (x - hi.astype(F32)).astype(BF16)
    return jnp.dot(hi, m, preferred_element_type=F32) + jnp.dot(lo, m, preferred_element_type=F32)


def _split3(x):
    x1 = x.astype(BF16)
    r1 = x - x1.astype(F32)
    x2 = r1.astype(BF16)
    x3 = (r1 - x2.astype(F32)).astype(BF16)
    return x1, x2, x3


def _split3_dot_left(m, x):
    d = functools.partial(jnp.dot, preferred_element_type=F32)
    x1, x2, x3 = _split3(x)
    return d(m, x1) + d(m, x2) + d(m, x3)


def _seg_rms(x, bones, seg, g):
    ms = _split2_dot(x * x, bones) * (1.0 / seg)
    return (x * lax.rsqrt(ms + EPS)) * g


def _sigmoid(x):
    return 1.0 / (1.0 + jnp.exp(-x))


def _softplus(x):
    return jnp.maximum(x, 0.0) + jnp.log1p(jnp.exp(-jnp.abs(x)))


def _log_sigmoid(z):
    return -(jnp.maximum(-z, 0.0) + jnp.log(1.0 + jnp.exp(-jnp.abs(z))))


def _gelu_tanh(x):
    return x * (0.5 * (1.0 + jnp.tanh(math.sqrt(2.0 / math.pi) * (x + 0.044715 * (x * x * x)))))


def _lane_head_mask(width, seg, h, dtype=F32):
    lane = lax.broadcasted_iota(jnp.int32, (1, width), 1)
    return ((lane >= seg * h) & (lane < seg * (h + 1))).astype(dtype)


def _softmax_rows(s):
    m = jnp.max(s, axis=-1, keepdims=True)
    e = jnp.exp(s - m)
    return e / jnp.sum(e, axis=-1, keepdims=True)


def _alibi_slope(h):
    return 2.0 ** (-8.0 * (h + 1) / SWA_HEADS)


def _rows(x, starts, size):
    parts = [x[s:s + size] for s in starts]
    return parts[0] if len(parts) == 1 else jnp.concatenate(parts, axis=0)


def _in_proj_body(x_ref, g_ref, w_ref, o_ref):
    xn = _rms(x_ref[...], g_ref[...])
    o_ref[...] = jnp.dot(xn.astype(BF16), w_ref[...], preferred_element_type=F32)


def _in_proj(x2d, g3, w3, layer, tm):
    m = x2d.shape[0]
    return pl.pallas_call(
        _in_proj_body,
        grid=(m // tm,),
        in_specs=[
            pl.BlockSpec((tm, D_MODEL), lambda i: (i, 0)),
            pl.BlockSpec((None, 1, D_MODEL), lambda i: (layer, 0, 0)),
            pl.BlockSpec((None, D_MODEL, NP), lambda i: (layer, 0, 0)),
        ],
        out_specs=pl.BlockSpec((tm, NP), lambda i: (i, 0)),
        out_shape=jax.ShapeDtypeStruct((m, NP), F32),
        compiler_params=_cparams("parallel"),
        name="in_proj",
    )(x2d, g3, w3)


def _ffn_body(x_ref, g_ref, w1_ref, w2_ref, o_ref):
    x = x_ref[...]
    xn = _rms(x, g_ref[...]).astype(BF16)
    acc = x
    for c in range(D_FF // FF_CHUNK):
        h = jnp.dot(xn, w1_ref[:, c * FF_CHUNK:(c + 1) * FF_CHUNK], preferred_element_type=F32)
        h = jnp.square(jnp.maximum(h, 0.0))
        acc = acc + jnp.dot(h.astype(BF16), w2_ref[c * FF_CHUNK:(c + 1) * FF_CHUNK, :], preferred_element_type=F32)
    o_ref[...] = acc


def _ffn(x2d, g3, w1, w2, layer, tm):
    m = x2d.shape[0]
    const = dict(pipeline_mode=pl.Buffered(1))
    return pl.pallas_call(
        _ffn_body,
        grid=(m // tm,),
        in_specs=[
            pl.BlockSpec((tm, D_MODEL), lambda i: (i, 0)),
            pl.BlockSpec((None, 1, D_MODEL), lambda i: (layer, 0, 0)),
            pl.BlockSpec((None, D_MODEL, D_FF), lambda i: (layer, 0, 0), **const),
            pl.BlockSpec((None, D_FF, D_MODEL), lambda i: (layer, 0, 0), **const),
        ],
        out_specs=pl.BlockSpec((tm, D_MODEL), lambda i: (i, 0)),
        out_shape=jax.ShapeDtypeStruct((m, D_MODEL), F32),
        compiler_params=_cparams("parallel"),
        name="ffn",
    )(x2d, g3, w1, w2)


def _mem_kv_body(mem_ref, g_ref, w_ref, gk_ref, bones_ref, k_ref, v_ref, vt_ref):
    kv = jnp.dot(_rms(mem_ref[...], g_ref[...]).astype(BF16), w_ref[...], preferred_element_type=F32)
    k_ref[...] = _seg_rms(kv[:, :MEM_W], bones_ref[...], MEM_HD, gk_ref[...])
    v = kv[:, MEM_W:]
    v_ref[...] = v
    nb, _, n_mem = vt_ref.shape
    for b in range(nb):
        vt_ref[b] = v[b * n_mem:(b + 1) * n_mem].T


def _mem_kv(mem2d, g3, wkv, gk3, bones256, nb):
    depth = wkv.shape[0]
    m = mem2d.shape[0]
    return pl.pallas_call(
        _mem_kv_body,
        grid=(depth,),
        in_specs=[
            pl.BlockSpec((m, D_MODEL), lambda l: (0, 0)),
            pl.BlockSpec((None, 1, D_MODEL), lambda l: (l, 0, 0)),
            pl.BlockSpec((None, D_MODEL, 2 * MEM_W), lambda l: (l, 0, 0)),
            pl.BlockSpec((None, 1, MEM_W), lambda l: (l, 0, 0)),
            pl.BlockSpec((MEM_W, MEM_W), lambda l: (0, 0)),
        ],
        out_specs=[pl.BlockSpec((None, m, MEM_W), lambda l: (l, 0, 0))] * 2
        + [pl.BlockSpec((None, nb, MEM_W, m // nb), lambda l: (l, 0, 0, 0))],
        out_shape=[jax.ShapeDtypeStruct((depth, m, MEM_W), F32)] * 2
        + [jax.ShapeDtypeStruct((depth, nb, MEM_W, m // nb), F32)],
        compiler_params=_cparams("parallel"),
        name="mem_kv",
    )(mem2d, g3, wkv, gk3, bones256)


def _mix_out(x, og, orr, osw, wo_ref):
    y = _dot(og, wo_ref[0:GLA_VP, :])
    y = y + _dot(orr, wo_ref[GLA_VP:GLA_VP + RG_WIDTH, :])
    y = y + _dot(osw, wo_ref[GLA_VP + RG_WIDTH:MIX_W, :])
    return x + y


def _mem_q(x1, gx, wq_ref, gq, bones):
    q = jnp.dot(_rms(x1, gx).astype(BF16), wq_ref[...], preferred_element_type=F32)
    return _seg_rms(q, bones, MEM_HD, gq)


def _gla_tile(h_scr, wg_ref, bg_ref, gout_ref, st_scr, cum_scr, d_scr, oc_scr, st_out):
    tt = h_scr.shape[0]
    hf = tt // 2
    qraw = h_scr[:, C_GQ:C_GQ + GLA_QKP]
    k = h_scr[:, C_GK:C_GK + GLA_QKP]
    z = jnp.dot(qraw.astype(BF16), wg_ref[...], preferred_element_type=F32) + bg_ref[...]
    g = _log_sigmoid(z) * (1.0 / GLA_TAU)
    q = qraw * (GLA_DK ** -0.5)

    row = lax.broadcasted_iota(jnp.int32, (tt, tt), 0)
    col = lax.broadcasted_iota(jnp.int32, (tt, tt), 1)
    cum = _split3_dot_left((row >= col).astype(BF16), g)
    cum_scr[...] = cum
    rowi = lax.broadcasted_iota(jnp.int32, (tt, 1), 0)

    def boundary(b, n):
        return jnp.concatenate(
            [jnp.broadcast_to(cum_scr[pl.ds(gi * 2 * b + b - 1, 1), :], (n, GLA_QKP)) for gi in range(tt // (2 * b))],
            axis=0)

    rd1 = pltpu.roll(cum, 1, 0)
    rd2 = pltpu.roll(cum, 2, 0)
    ru1 = pltpu.roll(cum, tt - 1, 0)
    m4 = rowi & 3
    bounds = {
        1: jnp.where((rowi & 1) == 0, cum, rd1),
        2: jnp.where(m4 == 0, ru1, jnp.where(m4 == 1, cum, jnp.where(m4 == 2, rd1, rd2))),
        4: boundary(4, 2 * 4),
    }
    low = [(q.astype(BF16), k.astype(BF16), 0)]
    for b in (1, 2, 4):
        second = (rowi & (2 * b - 1)) >= b
        c = bounds[b]
        qs = q * jnp.exp(jnp.where(second, cum - c, NEG))
        ks = k * jnp.exp(jnp.where(second, NEG, c - cum))
        low.append((qs.astype(BF16), ks.astype(BF16), int(math.log2(2 * b))))

    mid = []
    b = SUBLANE
    while b <= hf:
        ng = tt // (2 * b)
        c = boundary(b, b)
        firsts = [gi * 2 * b for gi in range(ng)]
        seconds = [gi * 2 * b + b for gi in range(ng)]
        qs2 = _rows(q, seconds, b) * jnp.exp(_rows(cum, seconds, b) - c)
        ks1 = _rows(k, firsts, b) * jnp.exp(c - _rows(cum, firsts, b))
        if b < hf:
            zb = jnp.zeros((b, GLA_QKP), F32)
            ks1 = jnp.concatenate([piece for gi in range(ng) for piece in (ks1[gi * b:(gi + 1) * b], zb)], axis=0)
        mid.append((b, qs2.astype(BF16), ks1.astype(BF16)))
        b *= 2

    st = st_scr[...]
    o_inter = _dot_nt(q * jnp.exp(cum), st)
    last = cum_scr[pl.ds(tt - 1, 1), :]
    kd = k * jnp.exp(last - cum)
    v_all = h_scr[:, C_GV:C_GV + GLA_VP]
    upd = jnp.dot(v_all.T.astype(BF16), kd.astype(BF16), preferred_element_type=F32)
    vrow = lax.broadcasted_iota(jnp.int32, (GLA_VP, 1), 0) // GLA_DVP
    lane = lax.broadcasted_iota(jnp.int32, (1, GLA_QKP), 1)
    khead = ((lane >= GLA_DK).astype(jnp.int32) + (lane >= 2 * GLA_DK).astype(jnp.int32)
             + (lane >= 3 * GLA_DK).astype(jnp.int32) + (lane >= 4 * GLA_DK).astype(jnp.int32))
    st_new = jnp.where(vrow == khead, st * jnp.exp(last) + upd, 0.0)
    st_scr[...] = st_new
    st_out[...] = st_new

    nh = GLA_HEADS
    hq = hf // 2
    c128 = lax.broadcasted_iota(jnp.int32, (1, hf), 1)
    r_low = lax.broadcasted_iota(jnp.int32, (nh * hf, 1), 0) & (hf - 1)
    r_mid = lax.broadcasted_iota(jnp.int32, (nh * hq, 1), 0) & (hq - 1)
    low_masks = [(r_low >> s) == (c128 >> s) for (_, _, s) in low]
    mid_masks = {bb: (r_mid >> int(math.log2(bb))) == (c128 >> int(math.log2(2 * bb)))
                 for (bb, _, _) in mid if 2 * bb < hf}
    nt = (((1,), (1,)), ((), ()))
    hms = [_lane_head_mask(GLA_QKP, GLA_DK, h, BF16) for h in range(nh)]
    stack_heads = lambda xb: jnp.concatenate([xb * hm for hm in hms], axis=0)
    for half in range(2):
        rs = slice(half * hf, (half + 1) * hf)
        acc = None
        for (qb, kb, _), mk in zip(low, low_masks):
            p = lax.dot_general(stack_heads(qb[rs]), kb[rs], nt, preferred_element_type=F32)
            p = jnp.where(mk, p, 0.0)
            acc = p if acc is None else acc + p
        for h in range(nh):
            d_scr[h, half] = acc[h * hf:(h + 1) * hf]
        for bb, qb2, kb1 in mid:
            if bb == hf:
                continue
            cs = slice(half * hq, (half + 1) * hq)
            p = lax.dot_general(stack_heads(qb2[cs]), kb1[rs], nt, preferred_element_type=F32)
            if bb in mid_masks:
                p = jnp.where(mid_masks[bb], p, 0.0)
            for h in range(nh):
                for gi in range(hf // (2 * bb)):
                    dst = pl.ds(gi * 2 * bb + bb, bb)
                    src = slice(h * hq + gi * bb, h * hq + (gi + 1) * bb)
                    d_scr[h, half, dst, :] = d_scr[h, half, dst, :] + p[src, :]
    _, qb2, kb1 = mid[-1]
    off_all = lax.dot_general(stack_heads(qb2), kb1, nt, preferred_element_type=F32)
    gout = gout_ref[...]
    for h in range(nh):
        off = off_all[h * hf:(h + 1) * hf]
        sl = slice(h * GLA_DVP, (h + 1) * GLA_DVP)
        v_h = v_all[:, sl].astype(BF16)
        o0 = jnp.dot(d_scr[h, 0].astype(BF16), v_h[0:hf], preferred_element_type=F32)
        a1 = jnp.concatenate([off, d_scr[h, 1]], axis=1).astype(BF16)
        o1 = jnp.dot(a1, v_h, preferred_element_type=F32)
        o = jnp.concatenate([o0, o1], axis=0) + o_inter[:, sl]
        ms = jnp.sum(o * o, axis=-1, keepdims=True) * (1.0 / GLA_DV)
        gate = h_scr[:, C_GOG + h * GLA_DVP:C_GOG + (h + 1) * GLA_DVP]
        oc_scr[:, sl] = ((o * lax.rsqrt(ms + EPS)) * gout) * (gate * _sigmoid(gate))


def _rg_gates(xc, wa_ref, ba, wx_ref, bx, lam):
    r = _sigmoid(jnp.dot(xc.astype(BF16), wa_ref[...], preferred_element_type=F32) + ba)
    i = _sigmoid(jnp.dot(xc.astype(BF16), wx_ref[...], preferred_element_type=F32) + bx)
    log_a = (-RG_C * r) * _softplus(-lam)
    a = jnp.exp(log_a)
    b = jnp.sqrt((1.0 + a * a) * jnp.tanh(-log_a)) * (i * xc)
    return a, b


def _rg_pitch(tt):
    return tt // SUBLANE + SUBLANE


def _rg_tiles(h_scr, cw_ref, cb_ref, wa_ref, ba_ref, wx_ref, bx_ref, lam_ref, hc_scr, cc_scr, ra_scr, rb_scr, rh_scr,
              rp_scr, oc_scr):
    nb, tt = len(h_scr), h_scr[0].shape[0]
    rowi = lax.broadcasted_iota(jnp.int32, (tt, 1), 0)
    nl = RG_WIDTH // LANE
    seg = tt // SUBLANE
    pitch = _rg_pitch(tt)
    cw = cw_ref[...]
    for bi in range(nb):
        rx = h_scr[bi][:,C_RX:C_RX + RG_WIDTH]
        p0, p1, p2 = cc_scr[bi, 0:1, :], cc_scr[bi, 1:2, :], cc_scr[bi, 2:3, :]
        x1 = jnp.where(rowi == 0, p2, pltpu.roll(rx, 1, 0))
        x2 = jnp.where(rowi == 0, p1, jnp.where(rowi == 1, p2, pltpu.roll(rx, 2, 0)))
        x3 = jnp.where(rowi == 0, p0, jnp.where(rowi == 1, p1, jnp.where(rowi == 2, p2, pltpu.roll(rx, 3, 0))))
        xc = cb_ref[...] + x3 * cw[0:1, :]
        xc = xc + x2 * cw[1:2, :]
        xc = xc + x1 * cw[2:3, :]
        xc = xc + rx * cw[3:4, :]
        a, b = _rg_gates(xc, wa_ref, ba_ref[...], wx_ref, bx_ref[...], lam_ref[...])
        for c in range(nl):
            for j in range(SUBLANE):
                dst = pl.ds(j * pitch, seg)
                ra_scr[bi, c, dst, :] = a[j * seg:(j + 1) * seg, c * LANE:(c + 1) * LANE]
                rb_scr[bi, c, dst, :] = b[j * seg:(j + 1) * seg, c * LANE:(c + 1) * LANE]

    chains = [(bi, c) for bi in range(nb) for c in range(nl)]
    hloc = {ch: jnp.zeros((SUBLANE, LANE), F32) for ch in chains}
    pc = {ch: jnp.ones((SUBLANE, LANE), F32) for ch in chains}
    for i in range(seg):
        idx = pl.ds(i, SUBLANE, stride=pitch)
        for ch in chains:
            bi, c = ch
            ai = ra_scr[bi, c, idx, :]
            hloc[ch] = ai * hloc[ch] + rb_scr[bi, c, idx, :]
            pc[ch] = ai * pc[ch]
            rh_scr[bi, c, idx, :] = hloc[ch]
            rp_scr[bi, c, idx, :] = pc[ch]
    outs = []
    for bi in range(nb):
        hl = jnp.concatenate([hloc[(bi, c)] for c in range(nl)], axis=1)
        pl_ = jnp.concatenate([pc[(bi, c)] for c in range(nl)], axis=1)
        hprev = hc_scr[bi, 0:1, :]
        carries = []
        for j in range(SUBLANE):
            carries.append(jnp.broadcast_to(hprev, (seg, RG_WIDTH)))
            hprev = hl[j:j + 1, :] + pl_[j:j + 1, :] * hprev
        hc_scr[bi, 0:1, :] = hprev
        carry = jnp.concatenate(carries, axis=0)
        unpitch = lambda r, c: jnp.concatenate([r[bi, c, pl.ds(j * pitch, seg), :] for j in range(SUBLANE)], axis=0)
        h = jnp.concatenate(
            [unpitch(rh_scr, c) + unpitch(rp_scr, c) * carry[:, c * LANE:(c + 1) * LANE] for c in range(nl)], axis=1)
        oc_scr[bi][:, GLA_VP:GLA_VP + RG_WIDTH] = h * _gelu_tanh(h_scr[bi][:,C_RY:C_RY + RG_WIDTH])
        tail = h_scr[bi][pl.ds(tt - (CONV_W - 1), CONV_W - 1), C_RX:C_RX + RG_WIDTH]
        cc_scr[bi, 0:CONV_W - 1, :] = tail
        outs.append((hprev, tail))
    return outs


def _stack_heads(x, masks):
    return jnp.concatenate([x * m for m in masks], axis=0)


def _unstack_heads_t(ot, seg, n):
    heads = ot.shape[0] // seg
    blocks = [ot[h * seg:(h + 1) * seg, h * n:(h + 1) * n] for h in range(heads)]
    return jnp.concatenate(blocks, axis=0).T


def _softmax_cols(s, sink=None):
    m = jnp.max(s, axis=0, keepdims=True)
    if sink is not None:
        m = jnp.maximum(m, sink)
    e = jnp.exp(s - m)
    den = jnp.sum(e, axis=0, keepdims=True)
    if sink is not None:
        den = den + jnp.exp(sink - m)
    return e * (1.0 / den)


def _swa_tiles(h_scr, sink_ref, layer, first, gq_ref, gk_ref, b256_ref, b128_ref, dup_ref, dupt_ref, kp_scr, vpt_scr,
               oc_scr):
    nb, tt = len(h_scr), h_scr[0].shape[0]
    w = WINDOW
    nh = SWA_HEADS
    hmb = [_lane_head_mask(SWA_Q, HEAD_DIM, h, BF16) for h in range(nh)]
    kj = lax.broadcasted_iota(jnp.int32, (2 * w, w), 0)
    qi = lax.broadcasted_iota(jnp.int32, (2 * w, w), 1)
    dist_i = qi + w - kj
    in_window = (dist_i >= 0) & (dist_i <= WINDOW)
    dist = dist_i.astype(F32)
    bias = jnp.concatenate([jnp.where(in_window, -_alibi_slope(h) * dist, NEG) for h in range(nh)], axis=1)
    kj4 = lax.broadcasted_iota(jnp.int32, (2 * w, nh * w), 0)
    bias_first = jnp.where(jnp.logical_and(first, kj4 < w), NEG, bias)
    sink_row = jnp.concatenate([jnp.full((1, w), sink_ref[layer, h], F32) for h in range(nh)], axis=1)
    dup = dup_ref[...]
    dupt = dupt_ref[...]
    nt = (((1,), (1,)), ((), ()))
    s_parts, vt_parts, wins = [], [], []
    for bi in range(nb):
        qn = _seg_rms(h_scr[bi][:, C_SQ:C_SQ + SWA_Q], b256_ref[...], HEAD_DIM, gq_ref[...])
        qn = (qn * (HEAD_DIM ** -0.5)).astype(BF16)
        kn = _seg_rms(h_scr[bi][:, C_SK:C_SK + SWA_KVW], b128_ref[...], HEAD_DIM, gk_ref[...])
        v = h_scr[bi][:, C_SV:C_SV + SWA_KVW]
        vt = v.T
        kcat = jnp.concatenate([kp_scr[bi], kn], axis=0).astype(BF16)
        vtcat = jnp.concatenate([vpt_scr[bi], vt], axis=1).astype(BF16)
        kexp = jnp.dot(kcat, dup, preferred_element_type=F32).astype(BF16)
        vexpt = jnp.dot(dupt, vtcat, preferred_element_type=F32).astype(BF16)
        for blk in range(tt // w):
            q4 = _stack_heads(qn[blk * w:(blk + 1) * w], hmb)
            s = lax.dot_general(kexp[blk * w:(blk + 2) * w], q4, nt, preferred_element_type=F32)
            s_parts.append(s + (bias_first if blk == 0 else bias))
            vt_parts.append(vexpt[:, blk * w:(blk + 2) * w])
        k_win = kn[tt - w:tt]
        kp_scr[bi] = k_win
        vpt_scr[bi] = vt[:, tt - w:tt]
        wins.append((k_win, v[tt - w:tt]))
    p_all = _softmax_cols(jnp.concatenate(s_parts, axis=1),
                          jnp.concatenate([sink_row] * len(s_parts), axis=1)).astype(BF16)
    n4 = nh * w
    for i, vtb in enumerate(vt_parts):
        bi, blk = divmod(i, tt // w)
        ot = jnp.dot(vtb, p_all[:, i * n4:(i + 1) * n4], preferred_element_type=F32)
        oc_scr[bi][blk * w:(blk + 1) * w, GLA_VP + RG_WIDTH:MIX_W] = _unstack_heads_t(ot, HEAD_DIM, w)
    return wins


def _mem_attend_tiles(x1s, gx, wq_ref, gq, bones, mk_ref, mvt_ref, wmo_ref):
    nh = MEM_HEADS
    nb, tt = len(x1s), x1s[0].shape[0]
    hmb = [_lane_head_mask(MEM_W, MEM_HD, h, BF16) for h in range(nh)]
    nt = (((1,), (1,)), ((), ()))
    xn = jnp.concatenate([_rms(x1, gx).astype(BF16) for x1 in x1s], axis=0)
    q = jnp.dot(xn, wq_ref[...], preferred_element_type=F32)
    qn = (_seg_rms(q, bones, MEM_HD, gq) * (MEM_HD ** -0.5)).astype(BF16)
    s_parts = [lax.dot_general(mk_ref[bi].astype(BF16), _stack_heads(qn[bi * tt:(bi + 1) * tt], hmb), nt,
                               preferred_element_type=F32) for bi in range(nb)]
    p_all = _softmax_cols(jnp.concatenate(s_parts, axis=1)).astype(BF16)
    n4 = nh * tt
    o = jnp.concatenate(
        [_unstack_heads_t(jnp.dot(mvt_ref[bi].astype(BF16), p_all[:, bi * n4:(bi + 1) * n4],
                                  preferred_element_type=F32), MEM_HD, tt) for bi in range(nb)], axis=0)
    y = jnp.dot(o.astype(BF16), wmo_ref[...], preferred_element_type=F32)
    return [x1s[bi] + y[bi * tt:(bi + 1) * tt] for bi in range(nb)]


def _mixer_body(sink_ref, x_ref, gmix_ref, win_ref, wg_ref, bg_ref, gout_ref, cw_ref, cb_ref, wa_ref,
                ba_ref, wx_ref, bx_ref, lam_ref, gq_ref, gk_ref, b256_ref, b128_ref, dup_ref, dupt_ref, wo_ref,
                gmx_ref, wq_ref, mgq_ref, mk_ref, mv_ref, wmo_ref,
                o_ref, st_ref, hl_ref, cbuf_ref, ko_ref, vo_ref,
                h_scr, oc_scr, st_scr, cum_scr, d_scr, hc_scr, cc_scr, ra_scr, rb_scr, rh_scr, rp_scr, kp_scr, vp_scr,
                *, layer):
    nb, tt = x_ref.shape[0], x_ref.shape[1]
    first = pl.program_id(0) == 0

    @pl.when(first)
    def _():
        for scr in (st_scr, hc_scr, cc_scr, kp_scr, vp_scr):
            scr[...] = jnp.zeros_like(scr)

    xn = jnp.concatenate([_rms(x_ref[bi], gmix_ref[...]).astype(BF16) for bi in range(nb)], axis=0)
    h_scr[...] = jnp.dot(xn, win_ref[...], preferred_element_type=F32)
    hs = [h_scr.at[pl.ds(bi * tt, tt)] for bi in range(nb)]
    ocs = [oc_scr.at[pl.ds(bi * tt, tt)] for bi in range(nb)]
    for bi in range(nb):
        _gla_tile(hs[bi], wg_ref, bg_ref, gout_ref, st_scr.at[bi], cum_scr.at[bi], d_scr.at[bi], ocs[bi],
                  st_ref.at[bi])
    rg_out = _rg_tiles(hs, cw_ref, cb_ref, wa_ref, ba_ref, wx_ref, bx_ref, lam_ref, hc_scr, cc_scr, ra_scr, rb_scr,
                       rh_scr, rp_scr, ocs)
    for bi, (h_last, tail) in enumerate(rg_out):
        hl_ref[bi] = h_last
        cbuf_ref[bi] = tail
    wins = _swa_tiles(hs, sink_ref, layer, first, gq_ref, gk_ref, b256_ref, b128_ref, dup_ref, dupt_ref, kp_scr,
                      vp_scr, ocs)
    for bi, (k_win, v_win) in enumerate(wins):
        ko_ref[bi] = k_win
        vo_ref[bi] = v_win
    y = jnp.dot(oc_scr[...].astype(BF16), wo_ref[...], preferred_element_type=F32)
    x1s = [x_ref[bi] + y[bi * tt:(bi + 1) * tt] for bi in range(nb)]
    x2s = _mem_attend_tiles(x1s, gmx_ref[...], wq_ref, mgq_ref[...], b256_ref[...], mk_ref, mv_ref, wmo_ref)
    for bi in range(nb):
        o_ref[bi] = x2s[bi]


def _mixer_prompt(sinks, x3d, gmix3, w_in_p, wg, bg3, gout3, cw, cb3, wa, ba3, wx, bx3, lam3, gq3, gk3, b256, b128,
                  dup, dupt, wo, gmx3, wq, mgq3, mk, mvt, wmo, layer):
    nb, t_len, _ = x3d.shape
    tt = MIX_T
    nt = t_len // tt
    n_mem = mk.shape[2]
    lay3 = lambda t: (layer, 0, 0)
    once = dict(pipeline_mode=pl.Buffered(1))
    vec = lambda w: pl.BlockSpec((None, 1, w), lay3)
    full = lambda a: pl.BlockSpec(a.shape, lambda t: (0,) * a.ndim)
    state = lambda r, w: pl.BlockSpec((nb, r, w), lambda t: (0, 0, 0))
    mem = pl.BlockSpec((None, nb, n_mem, MEM_W), lambda t: (layer, 0, 0, 0))
    memt = pl.BlockSpec((None, nb, MEM_W, n_mem), lambda t: (layer, 0, 0, 0))
    return pl.pallas_call(
        functools.partial(_mixer_body, layer=layer),
        grid=(nt,),
        in_specs=[
            pl.BlockSpec(memory_space=pltpu.SMEM),
            pl.BlockSpec((nb, tt, D_MODEL), lambda t: (0, t, 0)),
            vec(D_MODEL),
            pl.BlockSpec((None, D_MODEL, NP), lay3, **once),
            pl.BlockSpec((None, GLA_QKP, GLA_QKP), lay3), vec(GLA_QKP), vec(GLA_DVP),
            pl.BlockSpec((None, CONV_W, RG_WIDTH), lay3), vec(RG_WIDTH),
            pl.BlockSpec((None, RG_WIDTH, RG_WIDTH), lay3), vec(RG_WIDTH),
            pl.BlockSpec((None, RG_WIDTH, RG_WIDTH), lay3), vec(RG_WIDTH), vec(RG_WIDTH),
            vec(SWA_Q), vec(SWA_KVW), full(b256), full(b128), full(dup), full(dupt),
            pl.BlockSpec((None, MIX_W, D_MODEL), lay3, **once),
            vec(D_MODEL), pl.BlockSpec((None, D_MODEL, MEM_W), lay3), vec(MEM_W),
            mem, memt,
            pl.BlockSpec((None, MEM_W, D_MODEL), lay3),
        ],
        out_specs=[
            pl.BlockSpec((nb, tt, D_MODEL), lambda t: (0, t, 0)),
            state(GLA_VP, GLA_QKP), state(1, RG_WIDTH), state(CONV_W - 1, RG_WIDTH),
            state(WINDOW, SWA_KVW), state(WINDOW, SWA_KVW),
        ],
        out_shape=[
            jax.ShapeDtypeStruct((nb, t_len, D_MODEL), F32),
            jax.ShapeDtypeStruct((nb, GLA_VP, GLA_QKP), F32),
            jax.ShapeDtypeStruct((nb, 1, RG_WIDTH), F32),
            jax.ShapeDtypeStruct((nb, CONV_W - 1, RG_WIDTH), F32),
            jax.ShapeDtypeStruct((nb, WINDOW, SWA_KVW), F32),
            jax.ShapeDtypeStruct((nb, WINDOW, SWA_KVW), F32),
        ],
        scratch_shapes=[
            pltpu.VMEM((nb * tt, NP), F32), pltpu.VMEM((nb * tt, MIX_W), F32),
            pltpu.VMEM((nb, GLA_VP, GLA_QKP), F32), pltpu.VMEM((nb, tt, GLA_QKP), F32),
            pltpu.VMEM((nb, GLA_HEADS, 2, tt // 2, tt // 2), F32),
            pltpu.VMEM((nb, SUBLANE, RG_WIDTH), F32), pltpu.VMEM((nb, SUBLANE, RG_WIDTH), F32),
            *[pltpu.VMEM((nb, RG_WIDTH // LANE, SUBLANE * _rg_pitch(tt), LANE), F32) for _ in range(4)],
            pltpu.VMEM((nb, WINDOW, SWA_KVW), F32), pltpu.VMEM((nb, WINDOW, SWA_KVW), F32),
        ],
        compiler_params=_cparams("arbitrary"),
        name="mixer_prompt",
    )(sinks, x3d, gmix3, w_in_p, wg, bg3, gout3, cw, cb3, wa, ba3, wx, bx3, lam3, gq3, gk3, b256, b128, dup, dupt,
      wo, gmx3, wq, mgq3, mk, mvt, wmo)


def _sample_rows_body(h_ref, hst_ref, cbuf_ref, cw_ref, cb_ref, wa_ref, ba_ref, wx_ref, bx_ref, lam_ref, gq_ref,
                      gk_ref, b256_ref, b128_ref, dup_ref,
                      org_ref, hn_ref, cnew_ref, qn_ref, kx_ref, vx_ref, knt_ref, vnt_ref):
    rx = h_ref[:, C_RX:C_RX + RG_WIDTH]
    cw = cw_ref[...]
    xc = cb_ref[...] + cbuf_ref[0] * cw[0:1, :]
    xc = xc + cbuf_ref[1] * cw[1:2, :]
    xc = xc + cbuf_ref[2] * cw[2:3, :]
    xc = xc + rx * cw[3:4, :]
    a, b = _rg_gates(xc, wa_ref, ba_ref[...], wx_ref, bx_ref[...], lam_ref[...])
    hn = a * hst_ref[...] + b
    hn_ref[...] = hn
    org_ref[...] = hn * _gelu_tanh(h_ref[:, C_RY:C_RY + RG_WIDTH])
    cnew_ref[0] = cbuf_ref[1]
    cnew_ref[1] = cbuf_ref[2]
    cnew_ref[2] = rx

    qn_ref[...] = _seg_rms(h_ref[:, C_SQ:C_SQ + SWA_Q], b256_ref[...], HEAD_DIM, gq_ref[...])
    kn = _seg_rms(h_ref[:, C_SK:C_SK + SWA_KVW], b128_ref[...], HEAD_DIM, gk_ref[...])
    v = h_ref[:, C_SV:C_SV + SWA_KVW]
    kx_ref[...] = jnp.dot(kn.astype(BF16), dup_ref[...], preferred_element_type=F32)
    vx_ref[...] = jnp.dot(v.astype(BF16), dup_ref[...], preferred_element_type=F32)
    knt = kn.T
    vnt = v.T
    nblk, _, bt = knt_ref.shape
    for i in range(nblk):
        knt_ref[i] = knt[:, i * bt:(i + 1) * bt]
        vnt_ref[i] = vnt[:, i * bt:(i + 1) * bt]


def _sample_rows(hs, hst, cbuf, cw, cb3, wa, ba3, wx, bx3, lam3, gq3, gk3, b256, b128, dup, layer, bt):
    n = hs.shape[0]
    lay3 = lambda i: (layer, 0, 0)
    full = lambda a: pl.BlockSpec(a.shape, lambda i: (0,) * a.ndim)
    vec = lambda w: pl.BlockSpec((None, 1, w), lay3)
    sq = lambda w: pl.BlockSpec((None, w, w), lay3)
    taps = pl.BlockSpec((None, CONV_W - 1, n, RG_WIDTH), lambda i: (layer, 0, 0, 0))
    shapes = [(n, RG_WIDTH), (n, RG_WIDTH), (CONV_W - 1, n, RG_WIDTH), (n, SWA_Q), (n, SWA_Q), (n, SWA_Q),
              (n // bt, SWA_KVW, bt), (n // bt, SWA_KVW, bt)]
    return pl.pallas_call(
        _sample_rows_body,
        grid=(1,),
        in_specs=[
            full(hs), pl.BlockSpec((None, n, RG_WIDTH), lay3), taps,
            pl.BlockSpec((None, CONV_W, RG_WIDTH), lay3), vec(RG_WIDTH),
            sq(RG_WIDTH), vec(RG_WIDTH), sq(RG_WIDTH), vec(RG_WIDTH), vec(RG_WIDTH),
            vec(SWA_Q), vec(SWA_KVW), full(b256), full(b128), full(dup),
        ],
        out_specs=[pl.BlockSpec(s, lambda i, nd=len(s): (0,) * nd) for s in shapes],
        out_shape=[jax.ShapeDtypeStruct(s, F32) for s in shapes],
        compiler_params=_cparams("arbitrary"),
        name="sample_rows",
    )(hs, hst, cbuf, cw, cb3, wa, ba3, wx, bx3, lam3, gq3, gk3, b256, b128, dup)


def _sample_gla_body(s0_ref, h_ref, wg_ref, bg_ref, goutc_ref, o_ref, sn_ref, qt_scr, kt_scr, et_scr, vt_scr, gt_scr):
    h = pl.program_id(0)
    qraw = h_ref[:, C_GQ:C_GQ + GLA_QKP]
    z = jnp.dot(qraw.astype(BF16), wg_ref[...], preferred_element_type=F32) + bg_ref[...]
    et_scr[...] = jnp.exp(_log_sigmoid(z) * (1.0 / GLA_TAU)).T
    qt_scr[...] = (qraw * (GLA_DK ** -0.5)).T
    kt_scr[...] = h_ref[:, C_GK:C_GK + GLA_QKP].T
    vt_scr[...] = h_ref[:, C_GV:C_GV + GLA_VP].T
    gt_scr[...] = h_ref[:, C_GOG:C_GOG + GLA_VP].T
    r0 = pl.multiple_of(h * GLA_DK, SUBLANE)
    v0 = pl.multiple_of(h * GLA_DVP, GLA_DVP)
    vt = vt_scr[pl.ds(v0, GLA_DV), :]
    o = jnp.zeros(vt.shape, F32)
    for k in range(GLA_DK):
        sn = et_scr[pl.ds(r0 + k, 1), :] * s0_ref[k] + kt_scr[pl.ds(r0 + k, 1), :] * vt
        sn_ref[k] = sn
        o = o + qt_scr[pl.ds(r0 + k, 1), :] * sn
    ms = jnp.sum(o * o, axis=0, keepdims=True) * (1.0 / GLA_DV)
    gate = gt_scr[pl.ds(v0, GLA_DV), :]
    y = ((o * lax.rsqrt(ms + EPS)) * goutc_ref[...]) * (gate * _sigmoid(gate))
    y = jnp.concatenate([y, jnp.zeros((GLA_DVP - GLA_DV, y.shape[1]), F32)], axis=0)
    o_ref[...] = y.T


def _sample_gla(s0t, hs, wg, bg3, goutc, layer):
    n = hs.shape[0]
    lay3 = lambda h: (layer, 0, 0)
    return pl.pallas_call(
        _sample_gla_body,
        grid=(GLA_HEADS,),
        in_specs=[
            pl.BlockSpec((None, None, GLA_DK, GLA_DV, n), lambda h: (layer, h, 0, 0, 0)),
            pl.BlockSpec((n, NP), lambda h: (0, 0)),
            pl.BlockSpec((None, GLA_QKP, GLA_QKP), lay3),
            pl.BlockSpec((None, 1, GLA_QKP), lay3),
            pl.BlockSpec((None, GLA_DV, 1), lay3),
        ],
        out_specs=[pl.BlockSpec((n, GLA_DVP), lambda h: (0, h)),
                   pl.BlockSpec((None, None, GLA_DK, GLA_DV, n), lambda h: (layer, h, 0, 0, 0))],
        out_shape=[jax.ShapeDtypeStruct((n, GLA_VP), F32), jax.ShapeDtypeStruct(s0t.shape, F32)],
        scratch_shapes=[pltpu.VMEM((GLA_QKP, n), F32)] * 3 + [pltpu.VMEM((GLA_VP, n), F32)] * 2,
        input_output_aliases={0: 1},
        compiler_params=_cparams("arbitrary"),
        name="sample_gla",
    )(s0t, hs, wg, bg3, goutc)


def _sample_swa_body(sink_ref, kct_ref, vct_ref, qn_ref, kx_ref, vx_ref, knt_ref, vnt_ref, dupt_ref, o_ref, kot_ref,
                     vot_ref, *, layer):
    bt = kct_ref.shape[0]
    rp = SUBLANE
    dupt = dupt_ref[...]
    hi = lax.broadcasted_iota(jnp.int32, (rp, SWA_Q), 0)
    li = lax.broadcasted_iota(jnp.int32, (rp, SWA_Q), 1)
    hm = ((li >= hi * HEAD_DIM) & (li < (hi + 1) * HEAD_DIM)).astype(F32)
    hrow = lax.broadcasted_iota(jnp.int32, (rp, 1), 0)
    slope = jnp.zeros((rp, 1), F32)
    sink = jnp.zeros((rp, 1), F32)
    for h in range(SWA_HEADS):
        slope = jnp.where(hrow == h, _alibi_slope(h), slope)
        sink = jnp.where(hrow == h, sink_ref[layer, h], sink)
    tile = lambda a: jnp.concatenate([a] * bt, axis=0)
    dist = (WINDOW - lax.broadcasted_iota(jnp.int32, (1, WINDOW), 1)).astype(F32)
    last = lax.broadcasted_iota(jnp.int32, (SWA_KVW, WINDOW), 1) == WINDOW - 1
    nt = (((1,), (1,)), ((), ()))
    kexp = [jnp.dot(dupt, kct_ref[j].astype(BF16), preferred_element_type=F32).astype(BF16) for j in range(bt)]
    vexp = [jnp.dot(dupt, vct_ref[j].astype(BF16), preferred_element_type=F32).astype(BF16) for j in range(bt)]
    q = [(qn_ref[j:j + 1, :] * hm).astype(BF16) for j in range(bt)]
    s = jnp.concatenate([jnp.dot(q[j], kexp[j], preferred_element_type=F32) for j in range(bt)], axis=0)
    s = s * (HEAD_DIM ** -0.5) - tile(slope) * dist
    kx = jnp.concatenate([jnp.broadcast_to(kx_ref[j:j + 1, :], (rp, SWA_Q)) for j in range(bt)], axis=0)
    s_new = jnp.sum(jnp.concatenate(q, axis=0).astype(F32) * kx, axis=-1, keepdims=True) * (HEAD_DIM ** -0.5)
    sink = tile(sink)
    m = jnp.maximum(jnp.maximum(jnp.max(s, axis=-1, keepdims=True), s_new), sink)
    e = jnp.exp(s - m)
    e_new = jnp.exp(s_new - m)
    den = jnp.sum(e, axis=-1, keepdims=True) + e_new + jnp.exp(sink - m)
    p = (e / den).astype(BF16)
    p_new = (e_new / den).astype(BF16).astype(F32)
    for j in range(bt):
        rows = slice(j * rp, (j + 1) * rp)
        o = lax.dot_general(p[rows], vexp[j], nt, preferred_element_type=F32) + p_new[rows] * vx_ref[j:j + 1, :]
        o_ref[j:j + 1, :] = jnp.sum(o * hm, axis=0, keepdims=True)
        kot_ref[j] = jnp.where(last, knt_ref[:, j:j + 1], pltpu.roll(kct_ref[j], WINDOW - 1, 1))
        vot_ref[j] = jnp.where(last, vnt_ref[:, j:j + 1], pltpu.roll(vct_ref[j], WINDOW - 1, 1))


def _sample_swa(sinks, kct, vct, qn, kx, vx, knt, vnt, dupt, layer, bt):
    n = qn.shape[0]
    row = lambda w: pl.BlockSpec((bt, w), lambda i: (i, 0))
    cache = pl.BlockSpec((None, bt, SWA_KVW, WINDOW), lambda i: (layer, i, 0, 0))
    col = pl.BlockSpec((None, SWA_KVW, bt), lambda i: (i, 0, 0))
    return pl.pallas_call(
        functools.partial(_sample_swa_body, layer=layer),
        grid=(n // bt,),
        in_specs=[
            pl.BlockSpec(memory_space=pltpu.SMEM), cache, cache,
            row(SWA_Q), row(SWA_Q), row(SWA_Q), col, col,
            pl.BlockSpec(dupt.shape, lambda i: (0, 0)),
        ],
        out_specs=[row(SWA_Q), cache, cache],
        out_shape=[jax.ShapeDtypeStruct((n, SWA_Q), F32), jax.ShapeDtypeStruct(kct.shape, F32),
                   jax.ShapeDtypeStruct(vct.shape, F32)],
        input_output_aliases={1: 1, 2: 2},
        compiler_params=_cparams("parallel"),
        name="sample_swa",
    )(sinks, kct, vct, qn, kx, vx, knt, vnt, dupt)


def _sample_post1_body(x_ref, og_ref, or_ref, os_ref, wo_ref, gx_ref, wq_ref, gq_ref, bones_ref, x1_ref, qn_ref):
    x1 = _mix_out(x_ref[...], og_ref[...], or_ref[...], os_ref[...], wo_ref)
    x1_ref[...] = x1
    qn_ref[...] = _mem_q(x1, gx_ref[...], wq_ref, gq_ref[...], bones_ref[...])


def _sample_post1(x2d, og, orr, osw, wo, gx3, wq, gq3, bones256, layer):
    n = x2d.shape[0]
    lay3 = lambda i: (layer, 0, 0)
    full = lambda a: pl.BlockSpec(a.shape, lambda i: (0,) * a.ndim)
    return pl.pallas_call(
        _sample_post1_body,
        grid=(1,),
        in_specs=[
            full(x2d), full(og), full(orr), full(osw),
            pl.BlockSpec((None, MIX_W, D_MODEL), lay3),
            pl.BlockSpec((None, 1, D_MODEL), lay3),
            pl.BlockSpec((None, D_MODEL, MEM_W), lay3),
            pl.BlockSpec((None, 1, MEM_W), lay3),
            full(bones256),
        ],
        out_specs=[pl.BlockSpec((n, D_MODEL), lambda i: (0, 0)), pl.BlockSpec((n, MEM_W), lambda i: (0, 0))],
        out_shape=[jax.ShapeDtypeStruct((n, D_MODEL), F32), jax.ShapeDtypeStruct((n, MEM_W), F32)],
        compiler_params=_cparams("arbitrary"),
        name="sample_post1",
    )(x2d, og, orr, osw, wo, gx3, wq, gq3, bones256)


def _sample_mem_body(qn_ref, mk_ref, mv_ref, o_ref):
    bt = mk_ref.shape[0]
    rp = SUBLANE
    hi = lax.broadcasted_iota(jnp.int32, (rp, MEM_W), 0)
    li = lax.broadcasted_iota(jnp.int32, (rp, MEM_W), 1)
    hm = ((li >= hi * MEM_HD) & (li < (hi + 1) * MEM_HD)).astype(F32)
    s = jnp.concatenate(
        [jnp.dot((qn_ref[j:j + 1, :] * hm).astype(BF16), mk_ref[j].astype(BF16), preferred_element_type=F32)
         for j in range(bt)], axis=0)
    p = _softmax_rows(s * (MEM_HD ** -0.5)).astype(BF16)
    for j in range(bt):
        o = lax.dot_general(p[j * rp:(j + 1) * rp], mv_ref[j].astype(BF16), (((1,), (1,)), ((), ())),
                            preferred_element_type=F32)
        o_ref[j:j + 1, :] = jnp.sum(o * hm, axis=0, keepdims=True)


def _sample_mem(qn, mk, mv, layer, bt):
    n = qn.shape[0]
    n_mem = mk.shape[3]
    cache = pl.BlockSpec((None, bt, MEM_W, n_mem), lambda i: (layer, i, 0, 0))
    return pl.pallas_call(
        _sample_mem_body,
        grid=(n // bt,),
        in_specs=[pl.BlockSpec((bt, MEM_W), lambda i: (i, 0)), cache, cache],
        out_specs=pl.BlockSpec((bt, MEM_W), lambda i: (i, 0)),
        out_shape=jax.ShapeDtypeStruct((n, MEM_W), F32),
        compiler_params=_cparams("parallel"),
        name="sample_mem",
    )(qn, mk, mv)


def _ffn_res_body(x_ref, o_ref_in, wmo_ref, g_ref, w1_ref, w2_ref, out_ref):
    x = x_ref[...] + jnp.dot(o_ref_in[...].astype(BF16), wmo_ref[...], preferred_element_type=F32)
    xn = _rms(x, g_ref[...]).astype(BF16)
    acc = x
    for c in range(D_FF // FF_CHUNK):
        h = jnp.dot(xn, w1_ref[:, c * FF_CHUNK:(c + 1) * FF_CHUNK], preferred_element_type=F32)
        h = jnp.square(jnp.maximum(h, 0.0))
        acc = acc + jnp.dot(h.astype(BF16), w2_ref[c * FF_CHUNK:(c + 1) * FF_CHUNK, :], preferred_element_type=F32)
    out_ref[...] = acc


def _sample_ffn(x1, o, wmo, g3, w1, w2, layer):
    n = x1.shape[0]
    lay3 = lambda i: (layer, 0, 0)
    const = dict(pipeline_mode=pl.Buffered(1))
    return pl.pallas_call(
        _ffn_res_body,
        grid=(1,),
        in_specs=[
            pl.BlockSpec((n, D_MODEL), lambda i: (0, 0)),
            pl.BlockSpec((n, MEM_W), lambda i: (0, 0)),
            pl.BlockSpec((None, MEM_W, D_MODEL), lay3),
            pl.BlockSpec((None, 1, D_MODEL), lay3),
            pl.BlockSpec((None, D_MODEL, D_FF), lay3, **const),
            pl.BlockSpec((None, D_FF, D_MODEL), lay3, **const),
        ],
        out_specs=pl.BlockSpec((n, D_MODEL), lambda i: (0, 0)),
        out_shape=jax.ShapeDtypeStruct((n, D_MODEL), F32),
        compiler_params=_cparams("arbitrary"),
        name="sample_ffn",
    )(x1, o, wmo, g3, w1, w2)


def _pad_heads(a, axis, heads, width, padded):
    shp = a.shape
    a = a.reshape(shp[:axis] + (heads, width) + shp[axis + 1:])
    pad = [(0, 0)] * a.ndim
    pad[axis + 1] = (0, padded - width)
    a = jnp.pad(a, pad)
    return a.reshape(shp[:axis] + (heads * padded,) + shp[axis + 1:])


def _w_in_pieces(wt):
    offs = [0]
    for n in (GLA_QK, GLA_QK, GLA_V, GLA_RANK, GLA_V, RG_WIDTH, RG_WIDTH, SWA_Q, SWA_KVW, SWA_KVW):
        offs.append(offs[-1] + n)
    gq, gk, gv, glr, gog, rx, ry, sq, sk, sv = [wt[offs[i]:offs[i + 1]] for i in range(10)]
    zeros = lambda n: jnp.zeros((n, wt.shape[1]), wt.dtype)

    def heads(a):
        out = []
        for h in range(GLA_HEADS):
            out += [a[h * GLA_DV:(h + 1) * GLA_DV], zeros(GLA_DVP - GLA_DV)]
        return out

    return ([rx, ry, gq, glr, zeros(GLA_QKP - GLA_QK - GLA_RANK), gk, zeros(GLA_QKP - GLA_QK), sq]
            + heads(gv) + heads(gog) + [sk, sv])


def _prep_w_in_body(wt_ref, o_ref):
    o_ref[...] = jnp.concatenate(_w_in_pieces(wt_ref[...]), axis=0).T.astype(BF16)


def _prep_w_in(w_in):
    depth, d, width = w_in.shape
    tr = 256
    return pl.pallas_call(
        _prep_w_in_body,
        grid=(depth, d // tr),
        in_specs=[pl.BlockSpec((None, width, tr), lambda l, r: (l, 0, r))],
        out_specs=pl.BlockSpec((None, tr, NP), lambda l, r: (l, r, 0)),
        out_shape=jax.ShapeDtypeStruct((depth, d, NP), BF16),
        compiler_params=_cparams("parallel", "parallel"),
        name="prep_w_in",
    )(w_in.transpose(0, 2, 1))


def _block_diag(w):
    depth, nb, bw, _ = w.shape
    eye = jnp.eye(nb, dtype=w.dtype)
    return jnp.einsum("lncd,nm->lncmd", w, eye).reshape(depth, nb * bw, nb * bw)


def _block_ones(n, seg):
    i = jnp.arange(n) // seg
    return (i[:, None] == i[None, :]).astype(BF16)


def _dup_matrix():
    src = jnp.arange(SWA_Q)
    src = (src // HEAD_DIM // SWA_G) * HEAD_DIM + src % HEAD_DIM
    return (jnp.arange(SWA_KVW)[:, None] == src[None, :]).astype(BF16)


def kernel(x_prompt, x_sample, mem_prompt, state_gla, state_rg_h, state_rg_conv, cache_swa_k, cache_swa_v, cache_mem_k, cache_mem_v, g_mix, w_in, gla_w_gate2, gla_b_gate, gla_g_out, rg_conv_w, rg_conv_b, rg_w_a, rg_b_a, rg_w_x, rg_b_x, rg_lam, swa_g_q, swa_g_k, swa_sinks, w_out, g_mem_x, g_mem_m, mem_w_q, mem_w_kv, mem_g_q, mem_g_k, mem_w_o, g_ffn, ffn_w1, ffn_w2):
    depth = w_in.shape[0]
    nb, t_len, _ = x_prompt.shape
    ns = x_sample.shape[0]
    n_mem = mem_prompt.shape[1]
    assert t_len % ROW_T == 0 and t_len % MIX_T == 0 and x_sample.shape[1] == 1 and ns % 8 == 0
    row3 = lambda a: a.reshape(depth, 1, a.shape[-1])

    w_in_p = _prep_w_in(w_in)
    wg = jnp.zeros((depth, GLA_QKP, GLA_QKP), F32).at[:, GLR_LANE:GLR_LANE + GLA_RANK, :GLA_QK].set(gla_w_gate2)
    wg = wg.astype(BF16)
    bg3 = row3(jnp.pad(gla_b_gate, ((0, 0), (0, GLA_QKP - GLA_QK))))
    gout3 = row3(jnp.pad(gla_g_out, ((0, 0), (0, GLA_DVP - GLA_DV))))
    wa = _block_diag(rg_w_a).astype(BF16)
    wx = _block_diag(rg_w_x).astype(BF16)
    ba3, bx3, lam3, cb3 = row3(rg_b_a), row3(rg_b_x), row3(rg_lam), row3(rg_conv_b)
    gq3 = row3(jnp.tile(swa_g_q, (1, SWA_HEADS)))
    gk3 = row3(jnp.tile(swa_g_k, (1, SWA_KV)))
    wo = jnp.concatenate([_pad_heads(w_out[:, :GLA_V], 1, GLA_HEADS, GLA_DV, GLA_DVP), w_out[:, GLA_V:]], axis=1)
    wo = wo.astype(BF16)
    gmx3, gmm3, gffn3, gmix3 = row3(g_mem_x), row3(g_mem_m), row3(g_ffn), row3(g_mix)
    mgq3 = row3(jnp.tile(mem_g_q, (1, MEM_HEADS)))
    mgk3 = row3(jnp.tile(mem_g_k, (1, MEM_HEADS)))
    wq = mem_w_q.astype(BF16)
    wkv = mem_w_kv.astype(BF16)
    wmo = mem_w_o.astype(BF16)
    w1 = ffn_w1.astype(BF16)
    w2 = ffn_w2.astype(BF16)
    b256 = _block_ones(SWA_Q, HEAD_DIM)
    b128 = _block_ones(SWA_KVW, HEAD_DIM)
    dup = _dup_matrix()

    dupt = dup.T
    mk_all, mv_all, mvt_all = _mem_kv(mem_prompt.reshape(nb * n_mem, D_MODEL), gmm3, wkv, mgk3, b256, nb)
    mk_all = mk_all.reshape(depth, nb, n_mem, MEM_W)
    mv_all = mv_all.reshape(depth, nb, n_mem, MEM_W)

    xp = x_prompt.reshape(nb * t_len, D_MODEL)
    xs = x_sample.reshape(ns, D_MODEL)
    rg_conv_t = state_rg_conv.transpose(0, 2, 1, 3)
    gla_t = state_gla.transpose(0, 2, 3, 4, 1)
    swk = cache_swa_k.transpose(0, 1, 3, 4, 2).reshape(depth, ns, SWA_KVW, WINDOW)
    swv = cache_swa_v.transpose(0, 1, 3, 4, 2).reshape(depth, ns, SWA_KVW, WINDOW)
    cmk = cache_mem_k.transpose(0, 1, 3, 4, 2).reshape(depth, ns, MEM_W, n_mem)
    cmv = cache_mem_v.transpose(0, 1, 3, 4, 2).reshape(depth, ns, MEM_W, n_mem)
    goutc = gla_g_out.reshape(depth, GLA_DV, 1)
    sbt = min(ns, 32)

    gla_p, rgh_p, rgh_s, rgc_p, rgc_s, swk_p, swv_p = [], [], [], [], [], [], []
    ts = min(ns, 128)
    for l in range(depth):
        xp, st, hl, cbuf, kb, vb = _mixer_prompt(
            swa_sinks, xp.reshape(nb, t_len, D_MODEL), gmix3, w_in_p, wg, bg3, gout3, rg_conv_w, cb3, wa, ba3, wx,
            bx3, lam3, gq3, gk3, b256, b128, dup, dupt, wo, gmx3, wq, mgq3, mk_all, mvt_all, wmo, l)
        xp = _ffn(xp.reshape(nb * t_len, D_MODEL), gffn3, w1, w2, l, ROW_T)
        st = st.reshape(nb, GLA_HEADS, GLA_DVP, GLA_QKP)
        gla_p.append(jnp.stack(
            [st[:, h, :GLA_DV, h * GLA_DK:(h + 1) * GLA_DK].transpose(0, 2, 1) for h in range(GLA_HEADS)], axis=1))
        rgh_p.append(hl.reshape(nb, RG_WIDTH))
        rgc_p.append(cbuf)
        swk_p.append(kb.reshape(nb, WINDOW, SWA_KV, HEAD_DIM))
        swv_p.append(vb.reshape(nb, WINDOW, SWA_KV, HEAD_DIM))

        hs = _in_proj(xs, gmix3, w_in_p, l, ts)
        org_s, hn, cnew, qn, kx, vx, knt, vnt = _sample_rows(
            hs, state_rg_h, rg_conv_t, rg_conv_w, cb3, wa, ba3, wx, bx3, lam3, gq3, gk3, b256, b128, dup, l, sbt)
        og_s, gla_t = _sample_gla(gla_t, hs, wg, bg3, goutc, l)
        osw_s, swk, swv = _sample_swa(swa_sinks, swk, swv, qn, kx, vx, knt, vnt, dupt, l, sbt)
        x1, qm = _sample_post1(xs, og_s, org_s, osw_s, wo, gmx3, wq, mgq3, b256, l)
        om = _sample_mem(qm, cmk, cmv, l, min(ns, 16))
        xs = _sample_ffn(x1, om, wmo, gffn3, w1, w2, l)
        rgh_s.append(hn)
        rgc_s.append(cnew)

    to_cache = lambda a: a.reshape(depth, ns, SWA_KV, HEAD_DIM, WINDOW).transpose(0, 1, 4, 2, 3)
    return (xp.reshape(nb, t_len, D_MODEL), xs.reshape(ns, 1, D_MODEL),
            jnp.stack(gla_p), gla_t.transpose(0, 4, 1, 2, 3), jnp.stack(rgh_p), jnp.stack(rgh_s),
            jnp.stack(rgc_p), jnp.stack(rgc_s).transpose(0, 2, 1, 3), jnp.stack(swk_p), to_cache(swk),
            jnp.stack(swv_p), to_cache(swv),
            mk_all.reshape(depth, nb, n_mem, MEM_HEADS, MEM_HD), mv_all.reshape(depth, nb, n_mem, MEM_HEADS, MEM_HD))
```

```python
import functools
import math

import jax
import jax.numpy as jnp
from jax import lax
from jax.experimental import pallas as pl
from jax.experimental.pallas import tpu as pltpu

F32 = jnp.float32
BF16 = jnp.bfloat16

D_MODEL = 1024
EPS = 1e-6
GLA_HEADS, GLA_DK, GLA_DV, GLA_RANK, GLA_TAU = 4, 48, 96, 16, 16.0
GLA_QK = GLA_HEADS * GLA_DK
GLA_V = GLA_HEADS * GLA_DV
RG_WIDTH, RG_BLOCKS, RG_C, CONV_W = 384, 4, 8.0, 4
RG_BW = RG_WIDTH // RG_BLOCKS
SWA_HEADS, SWA_KV, HEAD_DIM, WINDOW = 4, 2, 64, 128
SWA_G = SWA_HEADS // SWA_KV
SWA_Q = SWA_HEADS * HEAD_DIM
SWA_KVW = SWA_KV * HEAD_DIM
MEM_HEADS, MEM_HD = 4, 64
MEM_W = MEM_HEADS * MEM_HD
D_FF = 4 * D_MODEL

LANE = 128
SUBLANE = 8
GLA_DVP = LANE
GLA_VP = GLA_HEADS * GLA_DVP
GLA_QKP = 2 * LANE
MIX_W = GLA_VP + RG_WIDTH + SWA_Q

C_RX, C_RY, C_GQ, C_GK, C_SQ, C_GV, C_GOG, C_SK, C_SV = 0, 384, 768, 1024, 1280, 1536, 2048, 2560, 2688
NP = 2816
GLR_LANE = GLA_QK

VMEM_LIMIT = 56 * 1024 * 1024
NEG = -1e30

MIX_T = 256
ROW_T = 512
FF_CHUNK = 1024


def _cparams(*sem):
    return pltpu.CompilerParams(dimension_semantics=sem, vmem_limit_bytes=VMEM_LIMIT)


def _rms(x, g):
    ms = jnp.mean(x * x, axis=-1, keepdims=True)
    return (x * lax.rsqrt(ms + EPS)) * g


def _dot(a, b):
    return jnp.dot(a.astype(BF16), b.astype(BF16), preferred_element_type=F32)


def _dot_nt(a, b):
    return lax.dot_general(a.astype(BF16), b.astype(BF16), (((1,), (1,)), ((), ())), preferred_element_type=F32)


def _split2_dot(x, m):
    hi = x.astype(BF16)
    lo = (x - hi.astype(F32)).astype(BF16)
    return jnp.dot(hi, m, preferred_element_type=F32) + jnp.dot(lo, m, preferred_element_type=F32)


def _split3(x):
    x1 = x.astype(BF16)
    r1 = x - x1.astype(F32)
    x2 = r1.astype(BF16)
    x3 = (r1 - x2.astype(F32)).astype(BF16)
    return x1, x2, x3


def _split3_dot_left(m, x):
    d = functools.partial(jnp.dot, preferred_element_type=F32)
    x1, x2, x3 = _split3(x)
    return d(m, x1) + d(m, x2) + d(m, x3)


def _seg_rms(x, bones, seg, g):
    ms = _split2_dot(x * x, bones) * (1.0 / seg)
    return (x * lax.rsqrt(ms + EPS)) * g


def _sigmoid(x):
    return 1.0 / (1.0 + jnp.exp(-x))


def _softplus(x):
    return jnp.maximum(x, 0.0) + jnp.log1p(jnp.exp(-jnp.abs(x)))


def _log_sigmoid(z):
    return -(jnp.maximum(-z, 0.0) + jnp.log(1.0 + jnp.exp(-jnp.abs(z))))


def _gelu_tanh(x):
    return x * (0.5 * (1.0 + jnp.tanh(math.sqrt(2.0 / math.pi) * (x + 0.044715 * (x * x * x)))))


def _lane_head_mask(width, seg, h, dtype=F32):
    lane = lax.broadcasted_iota(jnp.int32, (1, width), 1)
    return ((lane >= seg * h) & (lane < seg * (h + 1))).astype(dtype)


def _softmax_rows(s):
    m = jnp.max(s, axis=-1, keepdims=True)
    e = jnp.exp(s - m)
    return e / jnp.sum(e, axis=-1, keepdims=True)


def _alibi_slope(h):
    return 2.0 ** (-8.0 * (h + 1) / SWA_HEADS)


def _rows(x, starts, size):
    parts = [x[s:s + size] for s in starts]
    return parts[0] if len(parts) == 1 else jnp.concatenate(parts, axis=0)


def _in_proj_body(x_ref, g_ref, w_ref, o_ref):
    xn = _rms(x_ref[...], g_ref[...])
    o_ref[...] = jnp.dot(xn.astype(BF16), w_ref[...], preferred_element_type=F32)


def _in_proj(x2d, g3, w3, layer, tm):
    m = x2d.shape[0]
    return pl.pallas_call(
        _in_proj_body,
        grid=(m // tm,),
        in_specs=[
            pl.BlockSpec((tm, D_MODEL), lambda i: (i, 0)),
            pl.BlockSpec((None, 1, D_MODEL), lambda i: (layer, 0, 0)),
            pl.BlockSpec((None, D_MODEL, NP), lambda i: (layer, 0, 0)),
        ],
        out_specs=pl.BlockSpec((tm, NP), lambda i: (i, 0)),
        out_shape=jax.ShapeDtypeStruct((m, NP), F32),
        compiler_params=_cparams("parallel"),
        name="in_proj",
    )(x2d, g3, w3)


def _ffn_body(x_ref, g_ref, w1_ref, w2_ref, o_ref):
    x = x_ref[...]
    xn = _rms(x, g_ref[...]).astype(BF16)
    acc = x
    for c in range(D_FF // FF_CHUNK):
        h = jnp.dot(xn, w1_ref[:, c * FF_CHUNK:(c + 1) * FF_CHUNK], preferred_element_type=F32)
        h = jnp.square(jnp.maximum(h, 0.0))
        acc = acc + jnp.dot(h.astype(BF16), w2_ref[c * FF_CHUNK:(c + 1) * FF_CHUNK, :], preferred_element_type=F32)
    o_ref[...] = acc


def _ffn(x2d, g3, w1, w2, layer, tm):
    m = x2d.shape[0]
    const = dict(pipeline_mode=pl.Buffered(1))
    return pl.pallas_call(
        _ffn_body,
        grid=(m // tm,),
        in_specs=[
            pl.BlockSpec((tm, D_MODEL), lambda i: (i, 0)),
            pl.BlockSpec((None, 1, D_MODEL), lambda i: (layer, 0, 0)),
            pl.BlockSpec((None, D_MODEL, D_FF), lambda i: (layer, 0, 0), **const),
            pl.BlockSpec((None, D_FF, D_MODEL), lambda i: (layer, 0, 0), **const),
        ],
        out_specs=pl.BlockSpec((tm, D_MODEL), lambda i: (i, 0)),
        out_shape=jax.ShapeDtypeStruct((m, D_MODEL), F32),
        compiler_params=_cparams("parallel"),
        name="ffn",
    )(x2d, g3, w1, w2)


def _mem_kv_body(mem_ref, g_ref, w_ref, gk_ref, bones_ref, k_ref, v_ref, vt_ref):
    kv = jnp.dot(_rms(mem_ref[...], g_ref[...]).astype(BF16), w_ref[...], preferred_element_type=F32)
    k_ref[...] = _seg_rms(kv[:, :MEM_W], bones_ref[...], MEM_HD, gk_ref[...])
    v = kv[:, MEM_W:]
    v_ref[...] = v
    nb, _, n_mem = vt_ref.shape
    for b in range(nb):
        vt_ref[b] = v[b * n_mem:(b + 1) * n_mem].T


def _mem_kv(mem2d, g3, wkv, gk3, bones256, nb):
    depth = wkv.shape[0]
    m = mem2d.shape[0]
    return pl.pallas_call(
        _mem_kv_body,
        grid=(depth,),
        in_specs=[
            pl.BlockSpec((m, D_MODEL), lambda l: (0, 0)),
            pl.BlockSpec((None, 1, D_MODEL), lambda l: (l, 0, 0)),
            pl.BlockSpec((None, D_MODEL, 2 * MEM_W), lambda l: (l, 0, 0)),
            pl.BlockSpec((None, 1, MEM_W), lambda l: (l, 0, 0)),
            pl.BlockSpec((MEM_W, MEM_W), lambda l: (0, 0)),
        ],
        out_specs=[pl.BlockSpec((None, m, MEM_W), lambda l: (l, 0, 0))] * 2
        + [pl.BlockSpec((None, nb, MEM_W, m // nb), lambda l: (l, 0, 0, 0))],
        out_shape=[jax.ShapeDtypeStruct((depth, m, MEM_W), F32)] * 2
        + [jax.ShapeDtypeStruct((depth, nb, MEM_W, m // nb), F32)],
        compiler_params=_cparams("parallel"),
        name="mem_kv",
    )(mem2d, g3, wkv, gk3, bones256)


def _mix_out(x, og, orr, osw, wo_ref):
    y = _dot(og, wo_ref[0:GLA_VP, :])
    y = y + _dot(orr, wo_ref[GLA_VP:GLA_VP + RG_WIDTH, :])
    y = y + _dot(osw, wo_ref[GLA_VP + RG_WIDTH:MIX_W, :])
    return x + y


def _mem_q(x1, gx, wq_ref, gq, bones):
    q = jnp.dot(_rms(x1, gx).astype(BF16), wq_ref[...], preferred_element_type=F32)
    return _seg_rms(q, bones, MEM_HD, gq)


def _gla_tile(h_scr, wg_ref, bg_ref, gout_ref, st_scr, cum_scr, d_scr, oc_scr, st_out):
    tt = h_scr.shape[0]
    hf = tt // 2
    qraw = h_scr[:, C_GQ:C_GQ + GLA_QKP]
    k = h_scr[:, C_GK:C_GK + GLA_QKP]
    z = jnp.dot(qraw.astype(BF16), wg_ref[...], preferred_element_type=F32) + bg_ref[...]
    g = _log_sigmoid(z) * (1.0 / GLA_TAU)
    q = qraw * (GLA_DK ** -0.5)

    row = lax.broadcasted_iota(jnp.int32, (tt, tt), 0)
    col = lax.broadcasted_iota(jnp.int32, (tt, tt), 1)
    cum = _split3_dot_left((row >= col).astype(BF16), g)
    cum_scr[...] = cum
    rowi = lax.broadcasted_iota(jnp.int32, (tt, 1), 0)

    def boundary(b, n):
        return jnp.concatenate(
            [jnp.broadcast_to(cum_scr[pl.ds(gi * 2 * b + b - 1, 1), :], (n, GLA_QKP)) for gi in range(tt // (2 * b))],
            axis=0)

    rd1 = pltpu.roll(cum, 1, 0)
    rd2 = pltpu.roll(cum, 2, 0)
    ru1 = pltpu.roll(cum, tt - 1, 0)
    m4 = rowi & 3
    bounds = {
        1: jnp.where((rowi & 1) == 0, cum, rd1),
        2: jnp.where(m4 == 0, ru1, jnp.where(m4 == 1, cum, jnp.where(m4 == 2, rd1, rd2))),
        4: boundary(4, 2 * 4),
    }
    low = [(q.astype(BF16), k.astype(BF16), 0)]
    for b in (1, 2, 4):
        second = (rowi & (2 * b - 1)) >= b
        c = bounds[b]
        qs = q * jnp.exp(jnp.where(second, cum - c, NEG))
        ks = k * jnp.exp(jnp.where(second, NEG, c - cum))
        low.append((qs.astype(BF16), ks.astype(BF16), int(math.log2(2 * b))))

    mid = []
    b = SUBLANE
    while b <= hf:
        ng = tt // (2 * b)
        c = boundary(b, b)
        firsts = [gi * 2 * b for gi in range(ng)]
        seconds = [gi * 2 * b + b for gi in range(ng)]
        qs2 = _rows(q, seconds, b) * jnp.exp(_rows(cum, seconds, b) - c)
        ks1 = _rows(k, firsts, b) * jnp.exp(c - _rows(cum, firsts, b))
        if b < hf:
            zb = jnp.zeros((b, GLA_QKP), F32)
            ks1 = jnp.concatenate([piece for gi in range(ng) for piece in (ks1[gi * b:(gi + 1) * b], zb)], axis=0)
        mid.append((b, qs2.astype(BF16), ks1.astype(BF16)))
        b *= 2

    st = st_scr[...]
    o_inter = _dot_nt(q * jnp.exp(cum), st)
    last = cum_scr[pl.ds(tt - 1, 1), :]
    kd = k * jnp.exp(last - cum)
    v_all = h_scr[:, C_GV:C_GV + GLA_VP]
    upd = jnp.dot(v_all.T.astype(BF16), kd.astype(BF16), preferred_element_type=F32)
    vrow = lax.broadcasted_iota(jnp.int32, (GLA_VP, 1), 0) // GLA_DVP
    lane = lax.broadcasted_iota(jnp.int32, (1, GLA_QKP), 1)
    khead = ((lane >= GLA_DK).astype(jnp.int32) + (lane >= 2 * GLA_DK).astype(jnp.int32)
             + (lane >= 3 * GLA_DK).astype(jnp.int32) + (lane >= 4 * GLA_DK).astype(jnp.int32))
    st_new = jnp.where(vrow == khead, st * jnp.exp(last) + upd, 0.0)
    st_scr[...] = st_new
    st_out[...] = st_new

    nh = GLA_HEADS
    hq = hf // 2
    c128 = lax.broadcasted_iota(jnp.int32, (1, hf), 1)
    r_low = lax.broadcasted_iota(jnp.int32, (nh * hf, 1), 0) & (hf - 1)
    r_mid = lax.broadcasted_iota(jnp.int32, (nh * hq, 1), 0) & (hq - 1)
    low_masks = [(r_low >> s) == (c128 >> s) for (_, _, s) in low]
    mid_masks = {bb: (r_mid >> int(math.log2(bb))) == (c128 >> int(math.log2(2 * bb)))
                 for (bb, _, _) in mid if 2 * bb < hf}
    nt = (((1,), (1,)), ((), ()))
    hms = [_lane_head_mask(GLA_QKP, GLA_DK, h, BF16) for h in range(nh)]
    stack_heads = lambda xb: jnp.concatenate([xb * hm for hm in hms], axis=0)
    for half in range(2):
        rs = slice(half * hf, (half + 1) * hf)
        acc = None
        for (qb, kb, _), mk in zip(low, low_masks):
            p = lax.dot_general(stack_heads(qb[rs]), kb[rs], nt, preferred_element_type=F32)
            p = jnp.where(mk, p, 0.0)
            acc = p if acc is None else acc + p
        for h in range(nh):
            d_scr[h, half] = acc[h * hf:(h + 1) * hf]
        for bb, qb2, kb1 in mid:
            if bb == hf:
                continue
            cs = slice(half * hq, (half + 1) * hq)
            p = lax.dot_general(stack_heads(qb2[cs]), kb1[rs], nt, preferred_element_type=F32)
            if bb in mid_masks:
                p = jnp.where(mid_masks[bb], p, 0.0)
            for h in range(nh):
                for gi in range(hf // (2 * bb)):
                    dst = pl.ds(gi * 2 * bb + bb, bb)
                    src = slice(h * hq + gi * bb, h * hq + (gi + 1) * bb)
                    d_scr[h, half, dst, :] = d_scr[h, half, dst, :] + p[src, :]
    _, qb2, kb1 = mid[-1]
    off_all = lax.dot_general(stack_heads(qb2), kb1, nt, preferred_element_type=F32)
    gout = gout_ref[...]
    for h in range(nh):
        off = off_all[h * hf:(h + 1) * hf]
        sl = slice(h * GLA_DVP, (h + 1) * GLA_DVP)
        v_h = v_all[:, sl].astype(BF16)
        o0 = jnp.dot(d_scr[h, 0].astype(BF16), v_h[0:hf], preferred_element_type=F32)
        a1 = jnp.concatenate([off, d_scr[h, 1]], axis=1).astype(BF16)
        o1 = jnp.dot(a1, v_h, preferred_element_type=F32)
        o = jnp.concatenate([o0, o1], axis=0) + o_inter[:, sl]
        ms = jnp.sum(o * o, axis=-1, keepdims=True) * (1.0 / GLA_DV)
        gate = h_scr[:, C_GOG + h * GLA_DVP:C_GOG + (h + 1) * GLA_DVP]
        oc_scr[:, sl] = ((o * lax.rsqrt(ms + EPS)) * gout) * (gate * _sigmoid(gate))


def _rg_gates(xc, wa_ref, ba, wx_ref, bx, lam):
    r = _sigmoid(jnp.dot(xc.astype(BF16), wa_ref[...], preferred_element_type=F32) + ba)
    i = _sigmoid(jnp.dot(xc.astype(BF16), wx_ref[...], preferred_element_type=F32) + bx)
    log_a = (-RG_C * r) * _softplus(-lam)
    a = jnp.exp(log_a)
    b = jnp.sqrt((1.0 + a * a) * jnp.tanh(-log_a)) * (i * xc)
    return a, b


def _rg_pitch(tt):
    return tt // SUBLANE + SUBLANE


def _rg_tiles(h_scr, cw_ref, cb_ref, wa_ref, ba_ref, wx_ref, bx_ref, lam_ref, hc_scr, cc_scr, ra_scr, rb_scr, rh_scr,
              rp_scr, oc_scr):
    nb, tt = len(h_scr), h_scr[0].shape[0]
    rowi = lax.broadcasted_iota(jnp.int32, (tt, 1), 0)
    nl = RG_WIDTH // LANE
    seg = tt // SUBLANE
    pitch = _rg_pitch(tt)
    cw = cw_ref[...]
    for bi in range(nb):
        rx = h_scr[bi][:,C_RX:C_RX + RG_WIDTH]
        p0, p1, p2 = cc_scr[bi, 0:1, :], cc_scr[bi, 1:2, :], cc_scr[bi, 2:3, :]
        x1 = jnp.where(rowi == 0, p2, pltpu.roll(rx, 1, 0))
        x2 = jnp.where(rowi == 0, p1, jnp.where(rowi == 1, p2, pltpu.roll(rx, 2, 0)))
        x3 = jnp.where(rowi == 0, p0, jnp.where(rowi == 1, p1, jnp.where(rowi == 2, p2, pltpu.roll(rx, 3, 0))))
        xc = cb_ref[...] + x3 * cw[0:1, :]
        xc = xc + x2 * cw[1:2, :]
        xc = xc + x1 * cw[2:3, :]
        xc = xc + rx * cw[3:4, :]
        a, b = _rg_gates(xc, wa_ref, ba_ref[...], wx_ref, bx_ref[...], lam_ref[...])
        for c in range(nl):
            for j in range(SUBLANE):
                dst = pl.ds(j * pitch, seg)
                ra_scr[bi, c, dst, :] = a[j * seg:(j + 1) * seg, c * LANE:(c + 1) * LANE]
                rb_scr[bi, c, dst, :] = b[j * seg:(j + 1) * seg, c * LANE:(c + 1) * LANE]

    chains = [(bi, c) for bi in range(nb) for c in range(nl)]
    hloc = {ch: jnp.zeros((SUBLANE, LANE), F32) for ch in chains}
    pc = {ch: jnp.ones((SUBLANE, LANE), F32) for ch in chains}
    for i in range(seg):
        idx = pl.ds(i, SUBLANE, stride=pitch)
        for ch in chains:
            bi, c = ch
            ai = ra_scr[bi, c, idx, :]
            hloc[ch] = ai * hloc[ch] + rb_scr[bi, c, idx, :]
            pc[ch] = ai * pc[ch]
            rh_scr[bi, c, idx, :] = hloc[ch]
            rp_scr[bi, c, idx, :] = pc[ch]
    outs = []
    for bi in range(nb):
        hl = jnp.concatenate([hloc[(bi, c)] for c in range(nl)], axis=1)
        pl_ = jnp.concatenate([pc[(bi, c)] for c in range(nl)], axis=1)
        hprev = hc_scr[bi, 0:1, :]
        carries = []
        for j in range(SUBLANE):
            carries.append(jnp.broadcast_to(hprev, (seg, RG_WIDTH)))
            hprev = hl[j:j + 1, :] + pl_[j:j + 1, :] * hprev
        hc_scr[bi, 0:1, :] = hprev
        carry = jnp.concatenate(carries, axis=0)
        unpitch = lambda r, c: jnp.concatenate([r[bi, c, pl.ds(j * pitch, seg), :] for j in range(SUBLANE)], axis=0)
        h = jnp.concatenate(
            [unpitch(rh_scr, c) + unpitch(rp_scr, c) * carry[:, c * LANE:(c + 1) * LANE] for c in range(nl)], axis=1)
        oc_scr[bi][:, GLA_VP:GLA_VP + RG_WIDTH] = h * _gelu_tanh(h_scr[bi][:,C_RY:C_RY + RG_WIDTH])
        tail = h_scr[bi][pl.ds(tt - (CONV_W - 1), CONV_W - 1), C_RX:C_RX + RG_WIDTH]
        cc_scr[bi, 0:CONV_W - 1, :] = tail
        outs.append((hprev, tail))
    return outs


def _stack_heads(x, masks):
    return jnp.concatenate([x * m for m in masks], axis=0)


def _unstack_heads_t(ot, seg, n):
    heads = ot.shape[0] // seg
    blocks = [ot[h * seg:(h + 1) * seg, h * n:(h + 1) * n] for h in range(heads)]
    return jnp.concatenate(blocks, axis=0).T


def _softmax_cols(s, sink=None):
    m = jnp.max(s, axis=0, keepdims=True)
    if sink is not None:
        m = jnp.maximum(m, sink)
    e = jnp.exp(s - m)
    den = jnp.sum(e, axis=0, keepdims=True)
    if sink is not None:
        den = den + jnp.exp(sink - m)
    return e * (1.0 / den)


def _swa_tiles(h_scr, sink_ref, layer, first, gq_ref, gk_ref, b256_ref, b128_ref, dup_ref, dupt_ref, kp_scr, vpt_scr,
               oc_scr):
    nb, tt = len(h_scr), h_scr[0].shape[0]
    w = WINDOW
    nh = SWA_HEADS
    hmb = [_lane_head_mask(SWA_Q, HEAD_DIM, h, BF16) for h in range(nh)]
    kj = lax.broadcasted_iota(jnp.int32, (2 * w, w), 0)
    qi = lax.broadcasted_iota(jnp.int32, (2 * w, w), 1)
    dist_i = qi + w - kj
    in_window = (dist_i >= 0) & (dist_i <= WINDOW)
    dist = dist_i.astype(F32)
    bias = jnp.concatenate([jnp.where(in_window, -_alibi_slope(h) * dist, NEG) for h in range(nh)], axis=1)
    kj4 = lax.broadcasted_iota(jnp.int32, (2 * w, nh * w), 0)
    bias_first = jnp.where(jnp.logical_and(first, kj4 < w), NEG, bias)
    sink_row = jnp.concatenate([jnp.full((1, w), sink_ref[layer, h], F32) for h in range(nh)], axis=1)
    dup = dup_ref[...]
    dupt = dupt_ref[...]
    nt = (((1,), (1,)), ((), ()))
    s_parts, vt_parts, wins = [], [], []
    for bi in range(nb):
        qn = _seg_rms(h_scr[bi][:, C_SQ:C_SQ + SWA_Q], b256_ref[...], HEAD_DIM, gq_ref[...])
        qn = (qn * (HEAD_DIM ** -0.5)).astype(BF16)
        kn = _seg_rms(h_scr[bi][:, C_SK:C_SK + SWA_KVW], b128_ref[...], HEAD_DIM, gk_ref[...])
        v = h_scr[bi][:, C_SV:C_SV + SWA_KVW]
        vt = v.T
        kcat = jnp.concatenate([kp_scr[bi], kn], axis=0).astype(BF16)
        vtcat = jnp.concatenate([vpt_scr[bi], vt], axis=1).astype(BF16)
        kexp = jnp.dot(kcat, dup, preferred_element_type=F32).astype(BF16)
        vexpt = jnp.dot(dupt, vtcat, preferred_element_type=F32).astype(BF16)
        for blk in range(tt // w):
            q4 = _stack_heads(qn[blk * w:(blk + 1) * w], hmb)
            s = lax.dot_general(kexp[blk * w:(blk + 2) * w], q4, nt, preferred_element_type=F32)
            s_parts.append(s + (bias_first if blk == 0 else bias))
            vt_parts.append(vexpt[:, blk * w:(blk + 2) * w])
        k_win = kn[tt - w:tt]
        kp_scr[bi] = k_win
        vpt_scr[bi] = vt[:, tt - w:tt]
        wins.append((k_win, v[tt - w:tt]))
    p_all = _softmax_cols(jnp.concatenate(s_parts, axis=1),
                          jnp.concatenate([sink_row] * len(s_parts), axis=1)).astype(BF16)
    n4 = nh * w
    for i, vtb in enumerate(vt_parts):
        bi, blk = divmod(i, tt // w)
        ot = jnp.dot(vtb, p_all[:, i * n4:(i + 1) * n4], preferred_element_type=F32)
        oc_scr[bi][blk * w:(blk + 1) * w, GLA_VP + RG_WIDTH:MIX_W] = _unstack_heads_t(ot, HEAD_DIM, w)
    return wins


def _mem_attend_tiles(x1s, gx, wq_ref, gq, bones, mk_ref, mvt_ref, wmo_ref):
    nh = MEM_HEADS
    nb, tt = len(x1s), x1s[0].shape[0]
    hmb = [_lane_head_mask(MEM_W, MEM_HD, h, BF16) for h in range(nh)]
    nt = (((1,), (1,)), ((), ()))
    xn = jnp.concatenate([_rms(x1, gx).astype(BF16) for x1 in x1s], axis=0)
    q = jnp.dot(xn, wq_ref[...], preferred_element_type=F32)
    qn = (_seg_rms(q, bones, MEM_HD, gq) * (MEM_HD ** -0.5)).astype(BF16)
    s_parts = [lax.dot_general(mk_ref[bi].astype(BF16), _stack_heads(qn[bi * tt:(bi + 1) * tt], hmb), nt,
                               preferred_element_type=F32) for bi in range(nb)]
    p_all = _softmax_cols(jnp.concatenate(s_parts, axis=1)).astype(BF16)
    n4 = nh * tt
    o = jnp.concatenate(
        [_unstack_heads_t(jnp.dot(mvt_ref[bi].astype(BF16), p_all[:, bi * n4:(bi + 1) * n4],
                                  preferred_element_type=F32), MEM_HD, tt) for bi in range(nb)], axis=0)
    y = jnp.dot(o.astype(BF16), wmo_ref[...], preferred_element_type=F32)
    return [x1s[bi] + y[bi * tt:(bi + 1) * tt] for bi in range(nb)]


def _mixer_body(sink_ref, x_ref, gmix_ref, win_ref, wg_ref, bg_ref, gout_ref, cw_ref, cb_ref, wa_ref,
                ba_ref, wx_ref, bx_ref, lam_ref, gq_ref, gk_ref, b256_ref, b128_ref, dup_ref, dupt_ref, wo_ref,
                gmx_ref, wq_ref, mgq_ref, mk_ref, mv_ref, wmo_ref,
                o_ref, st_ref, hl_ref, cbuf_ref, ko_ref, vo_ref,
                h_scr, oc_scr, st_scr, cum_scr, d_scr, hc_scr, cc_scr, ra_scr, rb_scr, rh_scr, rp_scr, kp_scr, vp_scr,
                *, layer):
    nb, tt = x_ref.shape[0], x_ref.shape[1]
    first = pl.program_id(0) == 0

    @pl.when(first)
    def _():
        for scr in (st_scr, hc_scr, cc_scr, kp_scr, vp_scr):
            scr[...] = jnp.zeros_like(scr)

    xn = jnp.concatenate([_rms(x_ref[bi], gmix_ref[...]).astype(BF16) for bi in range(nb)], axis=0)
    h_scr[...] = jnp.dot(xn, win_ref[...], preferred_element_type=F32)
    hs = [h_scr.at[pl.ds(bi * tt, tt)] for bi in range(nb)]
    ocs = [oc_scr.at[pl.ds(bi * tt, tt)] for bi in range(nb)]
    for bi in range(nb):
        _gla_tile(hs[bi], wg_ref, bg_ref, gout_ref, st_scr.at[bi], cum_scr.at[bi], d_scr.at[bi], ocs[bi],
                  st_ref.at[bi])
    rg_out = _rg_tiles(hs, cw_ref, cb_ref, wa_ref, ba_ref, wx_ref, bx_ref, lam_ref, hc_scr, cc_scr, ra_scr, rb_scr,
                       rh_scr, rp_scr, ocs)
    for bi, (h_last, tail) in enumerate(rg_out):
        hl_ref[bi] = h_last
        cbuf_ref[bi] = tail
    wins = _swa_tiles(hs, sink_ref, layer, first, gq_ref, gk_ref, b256_ref, b128_ref, dup_ref, dupt_ref, kp_scr,
                      vp_scr, ocs)
    for bi, (k_win, v_win) in enumerate(wins):
        ko_ref[bi] = k_win
        vo_ref[bi] = v_win
    y = jnp.dot(oc_scr[...].astype(BF16), wo_ref[...], preferred_element_type=F32)
    x1s = [x_ref[bi] + y[bi * tt:(bi + 1) * tt] for bi in range(nb)]
    x2s = _mem_attend_tiles(x1s, gmx_ref[...], wq_ref, mgq_ref[...], b256_ref[...], mk_ref, mv_ref, wmo_ref)
    for bi in range(nb):
        o_ref[bi] = x2s[bi]


def _mixer_prompt(sinks, x3d, gmix3, w_in_p, wg, bg3, gout3, cw, cb3, wa, ba3, wx, bx3, lam3, gq3, gk3, b256, b128,
                  dup, dupt, wo, gmx3, wq, mgq3, mk, mvt, wmo, layer):
    nb, t_len, _ = x3d.shape
    tt = MIX_T
    nt = t_len // tt
    n_mem = mk.shape[2]
    lay3 = lambda t: (layer, 0, 0)
    once = dict(pipeline_mode=pl.Buffered(1))
    vec = lambda w: pl.BlockSpec((None, 1, w), lay3)
    full = lambda a: pl.BlockSpec(a.shape, lambda t: (0,) * a.ndim)
    state = lambda r, w: pl.BlockSpec((nb, r, w), lambda t: (0, 0, 0))
    mem = pl.BlockSpec((None, nb, n_mem, MEM_W), lambda t: (layer, 0, 0, 0))
    memt = pl.BlockSpec((None, nb, MEM_W, n_mem), lambda t: (layer, 0, 0, 0))
    return pl.pallas_call(
        functools.partial(_mixer_body, layer=layer),
        grid=(nt,),
        in_specs=[
            pl.BlockSpec(memory_space=pltpu.SMEM),
            pl.BlockSpec((nb, tt, D_MODEL), lambda t: (0, t, 0)),
            vec(D_MODEL),
            pl.BlockSpec((None, D_MODEL, NP), lay3, **once),
            pl.BlockSpec((None, GLA_QKP, GLA_QKP), lay3), vec(GLA_QKP), vec(GLA_DVP),
            pl.BlockSpec((None, CONV_W, RG_WIDTH), lay3), vec(RG_WIDTH),
            pl.BlockSpec((None, RG_WIDTH, RG_WIDTH), lay3), vec(RG_WIDTH),
            pl.BlockSpec((None, RG_WIDTH, RG_WIDTH), lay3), vec(RG_WIDTH), vec(RG_WIDTH),
            vec(SWA_Q), vec(SWA_KVW), full(b256), full(b128), full(dup), full(dupt),
            pl.BlockSpec((None, MIX_W, D_MODEL), lay3, **once),
            vec(D_MODEL), pl.BlockSpec((None, D_MODEL, MEM_W), lay3), vec(MEM_W),
            mem, memt,
            pl.BlockSpec((None, MEM_W, D_MODEL), lay3),
        ],
        out_specs=[
            pl.BlockSpec((nb, tt, D_MODEL), lambda t: (0, t, 0)),
            state(GLA_VP, GLA_QKP), state(1, RG_WIDTH), state(CONV_W - 1, RG_WIDTH),
            state(WINDOW, SWA_KVW), state(WINDOW, SWA_KVW),
        ],
        out_shape=[
            jax.ShapeDtypeStruct((nb, t_len, D_MODEL), F32),
            jax.ShapeDtypeStruct((nb, GLA_VP, GLA_QKP), F32),
            jax.ShapeDtypeStruct((nb, 1, RG_WIDTH), F32),
            jax.ShapeDtypeStruct((nb, CONV_W - 1, RG_WIDTH), F32),
            jax.ShapeDtypeStruct((nb, WINDOW, SWA_KVW), F32),
            jax.ShapeDtypeStruct((nb, WINDOW, SWA_KVW), F32),
        ],
        scratch_shapes=[
            pltpu.VMEM((nb * tt, NP), F32), pltpu.VMEM((nb * tt, MIX_W), F32),
            pltpu.VMEM((nb, GLA_VP, GLA_QKP), F32), pltpu.VMEM((nb, tt, GLA_QKP), F32),
            pltpu.VMEM((nb, GLA_HEADS, 2, tt // 2, tt // 2), F32),
            pltpu.VMEM((nb, SUBLANE, RG_WIDTH), F32), pltpu.VMEM((nb, SUBLANE, RG_WIDTH), F32),
            *[pltpu.VMEM((nb, RG_WIDTH // LANE, SUBLANE * _rg_pitch(tt), LANE), F32) for _ in range(4)],
            pltpu.VMEM((nb, WINDOW, SWA_KVW), F32), pltpu.VMEM((nb, WINDOW, SWA_KVW), F32),
        ],
        compiler_params=_cparams("arbitrary"),
        name="mixer_prompt",
    )(sinks, x3d, gmix3, w_in_p, wg, bg3, gout3, cw, cb3, wa, ba3, wx, bx3, lam3, gq3, gk3, b256, b128, dup, dupt,
      wo, gmx3, wq, mgq3, mk, mvt, wmo)


def _sample_rows_body(h_ref, hst_ref, cbuf_ref, cw_ref, cb_ref, wa_ref, ba_ref, wx_ref, bx_ref, lam_ref, gq_ref,
                      gk_ref, b256_ref, b128_ref, dup_ref,
                      org_ref, hn_ref, cnew_ref, qn_ref, kx_ref, vx_ref, knt_ref, vnt_ref):
    rx = h_ref[:, C_RX:C_RX + RG_WIDTH]
    cw = cw_ref[...]
    xc = cb_ref[...] + cbuf_ref[0] * cw[0:1, :]
    xc = xc + cbuf_ref[1] * cw[1:2, :]
    xc = xc + cbuf_ref[2] * cw[2:3, :]
    xc = xc + rx * cw[3:4, :]
    a, b = _rg_gates(xc, wa_ref, ba_ref[...], wx_ref, bx_ref[...], lam_ref[...])
    hn = a * hst_ref[...] + b
    hn_ref[...] = hn
    org_ref[...] = hn * _gelu_tanh(h_ref[:, C_RY:C_RY + RG_WIDTH])
    cnew_ref[0] = cbuf_ref[1]
    cnew_ref[1] = cbuf_ref[2]
    cnew_ref[2] = rx

    qn_ref[...] = _seg_rms(h_ref[:, C_SQ:C_SQ + SWA_Q], b256_ref[...], HEAD_DIM, gq_ref[...])
    kn = _seg_rms(h_ref[:, C_SK:C_SK + SWA_KVW], b128_ref[...], HEAD_DIM, gk_ref[...])
    v = h_ref[:, C_SV:C_SV + SWA_KVW]
    kx_ref[...] = jnp.dot(kn.astype(BF16), dup_ref[...], preferred_element_type=F32)
    vx_ref[...] = jnp.dot(v.astype(BF16), dup_ref[...], preferred_element_type=F32)
    knt = kn.T
    vnt = v.T
    nblk, _, bt = knt_ref.shape
    for i in range(nblk):
        knt_ref[i] = knt[:, i * bt:(i + 1) * bt]
        vnt_ref[i] = vnt[:, i * bt:(i + 1) * bt]


def _sample_rows(hs, hst, cbuf, cw, cb3, wa, ba3, wx, bx3, lam3, gq3, gk3, b256, b128, dup, layer, bt):
    n = hs.shape[0]
    lay3 = lambda i: (layer, 0, 0)
    full = lambda a: pl.BlockSpec(a.shape, lambda i: (0,) * a.ndim)
    vec = lambda w: pl.BlockSpec((None, 1, w), lay3)
    sq = lambda w: pl.BlockSpec((None, w, w), lay3)
    taps = pl.BlockSpec((None, CONV_W - 1, n, RG_WIDTH), lambda i: (layer, 0, 0, 0))
    shapes = [(n, RG_WIDTH), (n, RG_WIDTH), (CONV_W - 1, n, RG_WIDTH), (n, SWA_Q), (n, SWA_Q), (n, SWA_Q),
              (n // bt, SWA_KVW, bt), (n // bt, SWA_KVW, bt)]
    return pl.pallas_call(
        _sample_rows_body,
        grid=(1,),
        in_specs=[
            full(hs), pl.BlockSpec((None, n, RG_WIDTH), lay3), taps,
            pl.BlockSpec((None, CONV_W, RG_WIDTH), lay3), vec(RG_WIDTH),
            sq(RG_WIDTH), vec(RG_WIDTH), sq(RG_WIDTH), vec(RG_WIDTH), vec(RG_WIDTH),
            vec(SWA_Q), vec(SWA_KVW), full(b256), full(b128), full(dup),
        ],
        out_specs=[pl.BlockSpec(s, lambda i, nd=len(s): (0,) * nd) for s in shapes],
        out_shape=[jax.ShapeDtypeStruct(s, F32) for s in shapes],
        compiler_params=_cparams("arbitrary"),
        name="sample_rows",
    )(hs, hst, cbuf, cw, cb3, wa, ba3, wx, bx3, lam3, gq3, gk3, b256, b128, dup)


def _sample_gla_body(s0_ref, acc_ref, h_ref, wg_ref, bg_ref, goutc_ref, o_ref, sn_ref, qt_scr, kt_scr, et_scr, vt_scr,
                     gt_scr):
    h = pl.program_id(0)
    qraw = h_ref[:, C_GQ:C_GQ + GLA_QKP]
    z = jnp.dot(qraw.astype(BF16), wg_ref[...], preferred_element_type=F32) + bg_ref[...]
    et_scr[...] = jnp.exp(_log_sigmoid(z) * (1.0 / GLA_TAU)).T
    qt_scr[...] = (qraw * (GLA_DK ** -0.5)).T
    kt_scr[...] = h_ref[:, C_GK:C_GK + GLA_QKP].T
    vt_scr[...] = h_ref[:, C_GV:C_GV + GLA_VP].T
    gt_scr[...] = h_ref[:, C_GOG:C_GOG + GLA_VP].T
    r0 = pl.multiple_of(h * GLA_DK, SUBLANE)
    v0 = pl.multiple_of(h * GLA_DVP, GLA_DVP)
    vt = vt_scr[pl.ds(v0, GLA_DV), :]
    o = jnp.zeros(vt.shape, F32)
    for k in range(GLA_DK):
        sn = et_scr[pl.ds(r0 + k, 1), :] * s0_ref[k] + kt_scr[pl.ds(r0 + k, 1), :] * vt
        sn_ref[k] = sn
        o = o + qt_scr[pl.ds(r0 + k, 1), :] * sn
    ms = jnp.sum(o * o, axis=0, keepdims=True) * (1.0 / GLA_DV)
    gate = gt_scr[pl.ds(v0, GLA_DV), :]
    y = ((o * lax.rsqrt(ms + EPS)) * goutc_ref[...]) * (gate * _sigmoid(gate))
    y = jnp.concatenate([y, jnp.zeros((GLA_DVP - GLA_DV, y.shape[1]), F32)], axis=0)
    o_ref[...] = y.T


def _sample_gla(s0t, acc, hs, wg, bg3, goutc, layer):
    n = hs.shape[0]
    lay3 = lambda h: (layer, 0, 0)
    return pl.pallas_call(
        _sample_gla_body,
        grid=(GLA_HEADS,),
        in_specs=[
            pl.BlockSpec((None, None, GLA_DK, GLA_DV, n), lambda h: (layer, h, 0, 0, 0)),
            pl.BlockSpec(memory_space=pl.ANY),
            pl.BlockSpec((n, NP), lambda h: (0, 0)),
            pl.BlockSpec((None, GLA_QKP, GLA_QKP), lay3),
            pl.BlockSpec((None, 1, GLA_QKP), lay3),
            pl.BlockSpec((None, GLA_DV, 1), lay3),
        ],
        out_specs=[pl.BlockSpec((n, GLA_DVP), lambda h: (0, h)),
                   pl.BlockSpec((None, None, GLA_DK, GLA_DV, n), lambda h: (layer, h, 0, 0, 0))],
        out_shape=[jax.ShapeDtypeStruct((n, GLA_VP), F32), jax.ShapeDtypeStruct(s0t.shape, F32)],
        scratch_shapes=[pltpu.VMEM((GLA_QKP, n), F32)] * 3 + [pltpu.VMEM((GLA_VP, n), F32)] * 2,
        input_output_aliases={1: 1},
        compiler_params=_cparams("arbitrary"),
        name="sample_gla",
    )(s0t, acc, hs, wg, bg3, goutc)


def _sample_swa_body(sink_ref, kct_ref, vct_ref, kacc_ref, vacc_ref, qn_ref, kx_ref, vx_ref, knt_ref, vnt_ref, dupt_ref,
                     o_ref, kot_ref, vot_ref, *, layer):
    bt = kct_ref.shape[0]
    rp = SUBLANE
    dupt = dupt_ref[...]
    hi = lax.broadcasted_iota(jnp.int32, (rp, SWA_Q), 0)
    li = lax.broadcasted_iota(jnp.int32, (rp, SWA_Q), 1)
    hm = ((li >= hi * HEAD_DIM) & (li < (hi + 1) * HEAD_DIM)).astype(F32)
    hrow = lax.broadcasted_iota(jnp.int32, (rp, 1), 0)
    slope = jnp.zeros((rp, 1), F32)
    sink = jnp.zeros((rp, 1), F32)
    for h in range(SWA_HEADS):
        slope = jnp.where(hrow == h, _alibi_slope(h), slope)
        sink = jnp.where(hrow == h, sink_ref[layer, h], sink)
    tile = lambda a: jnp.concatenate([a] * bt, axis=0)
    dist = (WINDOW - lax.broadcasted_iota(jnp.int32, (1, WINDOW), 1)).astype(F32)
    last = lax.broadcasted_iota(jnp.int32, (SWA_KVW, WINDOW), 1) == WINDOW - 1
    nt = (((1,), (1,)), ((), ()))
    kexp = [jnp.dot(dupt, kct_ref[j].astype(BF16), preferred_element_type=F32).astype(BF16) for j in range(bt)]
    vexp = [jnp.dot(dupt, vct_ref[j].astype(BF16), preferred_element_type=F32).astype(BF16) for j in range(bt)]
    q = [(qn_ref[j:j + 1, :] * hm).astype(BF16) for j in range(bt)]
    s = jnp.concatenate([jnp.dot(q[j], kexp[j], preferred_element_type=F32) for j in range(bt)], axis=0)
    s = s * (HEAD_DIM ** -0.5) - tile(slope) * dist
    kx = jnp.concatenate([jnp.broadcast_to(kx_ref[j:j + 1, :], (rp, SWA_Q)) for j in range(bt)], axis=0)
    s_new = jnp.sum(jnp.concatenate(q, axis=0).astype(F32) * kx, axis=-1, keepdims=True) * (HEAD_DIM ** -0.5)
    sink = tile(sink)
    m = jnp.maximum(jnp.maximum(jnp.max(s, axis=-1, keepdims=True), s_new), sink)
    e = jnp.exp(s - m)
    e_new = jnp.exp(s_new - m)
    den = jnp.sum(e, axis=-1, keepdims=True) + e_new + jnp.exp(sink - m)
    p = (e / den).astype(BF16)
    p_new = (e_new / den).astype(BF16).astype(F32)
    for j in range(bt):
        rows = slice(j * rp, (j + 1) * rp)
        o = lax.dot_general(p[rows], vexp[j], nt, preferred_element_type=F32) + p_new[rows] * vx_ref[j:j + 1, :]
        o_ref[j:j + 1, :] = jnp.sum(o * hm, axis=0, keepdims=True)
        kot_ref[j] = jnp.where(last, knt_ref[:, j:j + 1], pltpu.roll(kct_ref[j], WINDOW - 1, 1))
        vot_ref[j] = jnp.where(last, vnt_ref[:, j:j + 1], pltpu.roll(vct_ref[j], WINDOW - 1, 1))


def _sample_swa(sinks, kct, vct, kacc, vacc, qn, kx, vx, knt, vnt, dupt, layer, bt):
    n = qn.shape[0]
    anywhere = pl.BlockSpec(memory_space=pl.ANY)
    row = lambda w: pl.BlockSpec((bt, w), lambda i: (i, 0))
    cache = pl.BlockSpec((None, bt, SWA_KVW, WINDOW), lambda i: (layer, i, 0, 0))
    col = pl.BlockSpec((None, SWA_KVW, bt), lambda i: (i, 0, 0))
    return pl.pallas_call(
        functools.partial(_sample_swa_body, layer=layer),
        grid=(n // bt,),
        in_specs=[
            pl.BlockSpec(memory_space=pltpu.SMEM), cache, cache, anywhere, anywhere,
            row(SWA_Q), row(SWA_Q), row(SWA_Q), col, col,
            pl.BlockSpec(dupt.shape, lambda i: (0, 0)),
        ],
        out_specs=[row(SWA_Q), cache, cache],
        out_shape=[jax.ShapeDtypeStruct((n, SWA_Q), F32), jax.ShapeDtypeStruct(kct.shape, F32),
                   jax.ShapeDtypeStruct(vct.shape, F32)],
        input_output_aliases={3: 1, 4: 2},
        compiler_params=_cparams("parallel"),
        name="sample_swa",
    )(sinks, kct, vct, kacc, vacc, qn, kx, vx, knt, vnt, dupt)


def _sample_post1_body(x_ref, og_ref, or_ref, os_ref, wo_ref, gx_ref, wq_ref, gq_ref, bones_ref, x1_ref, qn_ref):
    x1 = _mix_out(x_ref[...], og_ref[...], or_ref[...], os_ref[...], wo_ref)
    x1_ref[...] = x1
    qn_ref[...] = _mem_q(x1, gx_ref[...], wq_ref, gq_ref[...], bones_ref[...])


def _sample_post1(x2d, og, orr, osw, wo, gx3, wq, gq3, bones256, layer):
    n = x2d.shape[0]
    lay3 = lambda i: (layer, 0, 0)
    full = lambda a: pl.BlockSpec(a.shape, lambda i: (0,) * a.ndim)
    return pl.pallas_call(
        _sample_post1_body,
        grid=(1,),
        in_specs=[
            full(x2d), full(og), full(orr), full(osw),
            pl.BlockSpec((None, MIX_W, D_MODEL), lay3),
            pl.BlockSpec((None, 1, D_MODEL), lay3),
            pl.BlockSpec((None, D_MODEL, MEM_W), lay3),
            pl.BlockSpec((None, 1, MEM_W), lay3),
            full(bones256),
        ],
        out_specs=[pl.BlockSpec((n, D_MODEL), lambda i: (0, 0)), pl.BlockSpec((n, MEM_W), lambda i: (0, 0))],
        out_shape=[jax.ShapeDtypeStruct((n, D_MODEL), F32), jax.ShapeDtypeStruct((n, MEM_W), F32)],
        compiler_params=_cparams("arbitrary"),
        name="sample_post1",
    )(x2d, og, orr, osw, wo, gx3, wq, gq3, bones256)


def _sample_mem_body(qn_ref, mk_ref, mv_ref, o_ref):
    bt = mk_ref.shape[0]
    rp = SUBLANE
    hi = lax.broadcasted_iota(jnp.int32, (rp, MEM_W), 0)
    li = lax.broadcasted_iota(jnp.int32, (rp, MEM_W), 1)
    hm = ((li >= hi * MEM_HD) & (li < (hi + 1) * MEM_HD)).astype(F32)
    s = jnp.concatenate(
        [jnp.dot((qn_ref[j:j + 1, :] * hm).astype(BF16), mk_ref[j].astype(BF16), preferred_element_type=F32)
         for j in range(bt)], axis=0)
    p = _softmax_rows(s * (MEM_HD ** -0.5)).astype(BF16)
    for j in range(bt):
        o = lax.dot_general(p[j * rp:(j + 1) * rp], mv_ref[j].astype(BF16), (((1,), (1,)), ((), ())),
                            preferred_element_type=F32)
        o_ref[j:j + 1, :] = jnp.sum(o * hm, axis=0, keepdims=True)


def _sample_mem(qn, mk, mv, layer, bt):
    n = qn.shape[0]
    n_mem = mk.shape[3]
    cache = pl.BlockSpec((None, bt, MEM_W, n_mem), lambda i: (layer, i, 0, 0))
    return pl.pallas_call(
        _sample_mem_body,
        grid=(n // bt,),
        in_specs=[pl.BlockSpec((bt, MEM_W), lambda i: (i, 0)), cache, cache],
        out_specs=pl.BlockSpec((bt, MEM_W), lambda i: (i, 0)),
        out_shape=jax.ShapeDtypeStruct((n, MEM_W), F32),
        compiler_params=_cparams("parallel"),
        name="sample_mem",
    )(qn, mk, mv)


def _ffn_res_body(x_ref, o_ref_in, wmo_ref, g_ref, w1_ref, w2_ref, out_ref):
    x = x_ref[...] + jnp.dot(o_ref_in[...].astype(BF16), wmo_ref[...], preferred_element_type=F32)
    xn = _rms(x, g_ref[...]).astype(BF16)
    acc = x
    for c in range(D_FF // FF_CHUNK):
        h = jnp.dot(xn, w1_ref[:, c * FF_CHUNK:(c + 1) * FF_CHUNK], preferred_element_type=F32)
        h = jnp.square(jnp.maximum(h, 0.0))
        acc = acc + jnp.dot(h.astype(BF16), w2_ref[c * FF_CHUNK:(c + 1) * FF_CHUNK, :], preferred_element_type=F32)
    out_ref[...] = acc


def _sample_ffn(x1, o, wmo, g3, w1, w2, layer):
    n = x1.shape[0]
    lay3 = lambda i: (layer, 0, 0)
    const = dict(pipeline_mode=pl.Buffered(1))
    return pl.pallas_call(
        _ffn_res_body,
        grid=(1,),
        in_specs=[
            pl.BlockSpec((n, D_MODEL), lambda i: (0, 0)),
            pl.BlockSpec((n, MEM_W), lambda i: (0, 0)),
            pl.BlockSpec((None, MEM_W, D_MODEL), lay3),
            pl.BlockSpec((None, 1, D_MODEL), lay3),
            pl.BlockSpec((None, D_MODEL, D_FF), lay3, **const),
            pl.BlockSpec((None, D_FF, D_MODEL), lay3, **const),
        ],
        out_specs=pl.BlockSpec((n, D_MODEL), lambda i: (0, 0)),
        out_shape=jax.ShapeDtypeStruct((n, D_MODEL), F32),
        compiler_params=_cparams("arbitrary"),
        name="sample_ffn",
    )(x1, o, wmo, g3, w1, w2)


def _pad_heads(a, axis, heads, width, padded):
    shp = a.shape
    a = a.reshape(shp[:axis] + (heads, width) + shp[axis + 1:])
    pad = [(0, 0)] * a.ndim
    pad[axis + 1] = (0, padded - width)
    a = jnp.pad(a, pad)
    return a.reshape(shp[:axis] + (heads * padded,) + shp[axis + 1:])


def _w_in_pieces(wt):
    offs = [0]
    for n in (GLA_QK, GLA_QK, GLA_V, GLA_RANK, GLA_V, RG_WIDTH, RG_WIDTH, SWA_Q, SWA_KVW, SWA_KVW):
        offs.append(offs[-1] + n)
    gq, gk, gv, glr, gog, rx, ry, sq, sk, sv = [wt[offs[i]:offs[i + 1]] for i in range(10)]
    zeros = lambda n: jnp.zeros((n, wt.shape[1]), wt.dtype)

    def heads(a):
        out = []
        for h in range(GLA_HEADS):
            out += [a[h * GLA_DV:(h + 1) * GLA_DV], zeros(GLA_DVP - GLA_DV)]
        return out

    return ([rx, ry, gq, glr, zeros(GLA_QKP - GLA_QK - GLA_RANK), gk, zeros(GLA_QKP - GLA_QK), sq]
            + heads(gv) + heads(gog) + [sk, sv])


def _prep_w_in_body(wt_ref, o_ref):
    o_ref[...] = jnp.concatenate(_w_in_pieces(wt_ref[...]), axis=0).T.astype(BF16)


def _prep_w_in(w_in):
    depth, d, width = w_in.shape
    tr = 256
    return pl.pallas_call(
        _prep_w_in_body,
        grid=(depth, d // tr),
        in_specs=[pl.BlockSpec((None, width, tr), lambda l, r: (l, 0, r))],
        out_specs=pl.BlockSpec((None, tr, NP), lambda l, r: (l, r, 0)),
        out_shape=jax.ShapeDtypeStruct((depth, d, NP), BF16),
        compiler_params=_cparams("parallel", "parallel"),
        name="prep_w_in",
    )(w_in.transpose(0, 2, 1))


def _block_diag(w):
    depth, nb, bw, _ = w.shape
    eye = jnp.eye(nb, dtype=w.dtype)
    return jnp.einsum("lncd,nm->lncmd", w, eye).reshape(depth, nb * bw, nb * bw)


def _block_ones(n, seg):
    i = jnp.arange(n) // seg
    return (i[:, None] == i[None, :]).astype(BF16)


def _dup_matrix():
    src = jnp.arange(SWA_Q)
    src = (src // HEAD_DIM // SWA_G) * HEAD_DIM + src % HEAD_DIM
    return (jnp.arange(SWA_KVW)[:, None] == src[None, :]).astype(BF16)


def kernel(x_prompt, x_sample, mem_prompt, state_gla, state_rg_h, state_rg_conv, cache_swa_k, cache_swa_v, cache_mem_k, cache_mem_v, g_mix, w_in, gla_w_gate2, gla_b_gate, gla_g_out, rg_conv_w, rg_conv_b, rg_w_a, rg_b_a, rg_w_x, rg_b_x, rg_lam, swa_g_q, swa_g_k, swa_sinks, w_out, g_mem_x, g_mem_m, mem_w_q, mem_w_kv, mem_g_q, mem_g_k, mem_w_o, g_ffn, ffn_w1, ffn_w2):
    depth = w_in.shape[0]
    nb, t_len, _ = x_prompt.shape
    ns = x_sample.shape[0]
    n_mem = mem_prompt.shape[1]
    assert t_len % ROW_T == 0 and t_len % MIX_T == 0 and x_sample.shape[1] == 1 and ns % 8 == 0
    row3 = lambda a: a.reshape(depth, 1, a.shape[-1])

    w_in_p = _prep_w_in(w_in)
    wg = jnp.zeros((depth, GLA_QKP, GLA_QKP), F32).at[:, GLR_LANE:GLR_LANE + GLA_RANK, :GLA_QK].set(gla_w_gate2)
    wg = wg.astype(BF16)
    bg3 = row3(jnp.pad(gla_b_gate, ((0, 0), (0, GLA_QKP - GLA_QK))))
    gout3 = row3(jnp.pad(gla_g_out, ((0, 0), (0, GLA_DVP - GLA_DV))))
    wa = _block_diag(rg_w_a).astype(BF16)
    wx = _block_diag(rg_w_x).astype(BF16)
    ba3, bx3, lam3, cb3 = row3(rg_b_a), row3(rg_b_x), row3(rg_lam), row3(rg_conv_b)
    gq3 = row3(jnp.tile(swa_g_q, (1, SWA_HEADS)))
    gk3 = row3(jnp.tile(swa_g_k, (1, SWA_KV)))
    wo = jnp.concatenate([_pad_heads(w_out[:, :GLA_V], 1, GLA_HEADS, GLA_DV, GLA_DVP), w_out[:, GLA_V:]], axis=1)
    wo = wo.astype(BF16)
    gmx3, gmm3, gffn3, gmix3 = row3(g_mem_x), row3(g_mem_m), row3(g_ffn), row3(g_mix)
    mgq3 = row3(jnp.tile(mem_g_q, (1, MEM_HEADS)))
    mgk3 = row3(jnp.tile(mem_g_k, (1, MEM_HEADS)))
    wq = mem_w_q.astype(BF16)
    wkv = mem_w_kv.astype(BF16)
    wmo = mem_w_o.astype(BF16)
    w1 = ffn_w1.astype(BF16)
    w2 = ffn_w2.astype(BF16)
    b256 = _block_ones(SWA_Q, HEAD_DIM)
    b128 = _block_ones(SWA_KVW, HEAD_DIM)
    dup = _dup_matrix()

    dupt = dup.T
    mk_all, mv_all, mvt_all = _mem_kv(mem_prompt.reshape(nb * n_mem, D_MODEL), gmm3, wkv, mgk3, b256, nb)
    mk_all = mk_all.reshape(depth, nb, n_mem, MEM_W)
    mv_all = mv_all.reshape(depth, nb, n_mem, MEM_W)

    xp = x_prompt.reshape(nb * t_len, D_MODEL)
    xs = x_sample.reshape(ns, D_MODEL)
    rg_conv_t = state_rg_conv.transpose(0, 2, 1, 3)
    gla_t = state_gla.transpose(0, 2, 3, 4, 1)
    swk = cache_swa_k.transpose(0, 1, 3, 4, 2).reshape(depth, ns, SWA_KVW, WINDOW)
    swv = cache_swa_v.transpose(0, 1, 3, 4, 2).reshape(depth, ns, SWA_KVW, WINDOW)
    cmk = cache_mem_k.transpose(0, 1, 3, 4, 2).reshape(depth, ns, MEM_W, n_mem)
    cmv = cache_mem_v.transpose(0, 1, 3, 4, 2).reshape(depth, ns, MEM_W, n_mem)
    goutc = gla_g_out.reshape(depth, GLA_DV, 1)
    sbt = min(ns, 32)
    gla_new, swk_new, swv_new = jnp.zeros_like(gla_t), jnp.zeros_like(swk), jnp.zeros_like(swv)

    gla_p, rgh_p, rgh_s, rgc_p, rgc_s, swk_p, swv_p = [], [], [], [], [], [], []
    ts = min(ns, 128)
    for l in range(depth):
        xp, st, hl, cbuf, kb, vb = _mixer_prompt(
            swa_sinks, xp.reshape(nb, t_len, D_MODEL), gmix3, w_in_p, wg, bg3, gout3, rg_conv_w, cb3, wa, ba3, wx,
            bx3, lam3, gq3, gk3, b256, b128, dup, dupt, wo, gmx3, wq, mgq3, mk_all, mvt_all, wmo, l)
        xp = _ffn(xp.reshape(nb * t_len, D_MODEL), gffn3, w1, w2, l, ROW_T)
        st = st.reshape(nb, GLA_HEADS, GLA_DVP, GLA_QKP)
        gla_p.append(jnp.stack(
            [st[:, h, :GLA_DV, h * GLA_DK:(h + 1) * GLA_DK].transpose(0, 2, 1) for h in range(GLA_HEADS)], axis=1))
        rgh_p.append(hl.reshape(nb, RG_WIDTH))
        rgc_p.append(cbuf)
        swk_p.append(kb.reshape(nb, WINDOW, SWA_KV, HEAD_DIM))
        swv_p.append(vb.reshape(nb, WINDOW, SWA_KV, HEAD_DIM))

        hs = _in_proj(xs, gmix3, w_in_p, l, ts)
        org_s, hn, cnew, qn, kx, vx, knt, vnt = _sample_rows(
            hs, state_rg_h, rg_conv_t, rg_conv_w, cb3, wa, ba3, wx, bx3, lam3, gq3, gk3, b256, b128, dup, l, sbt)
        og_s, gla_new = _sample_gla(gla_t, gla_new, hs, wg, bg3, goutc, l)
        osw_s, swk_new, swv_new = _sample_swa(swa_sinks, swk, swv, swk_new, swv_new, qn, kx, vx, knt, vnt, dupt, l,
                                              sbt)
        x1, qm = _sample_post1(xs, og_s, org_s, osw_s, wo, gmx3, wq, mgq3, b256, l)
        om = _sample_mem(qm, cmk, cmv, l, min(ns, 16))
        xs = _sample_ffn(x1, om, wmo, gffn3, w1, w2, l)
        rgh_s.append(hn)
        rgc_s.append(cnew)

    to_cache = lambda a: a.reshape(depth, ns, SWA_KV, HEAD_DIM, WINDOW).transpose(0, 1, 4, 2, 3)
    return (xp.reshape(nb, t_len, D_MODEL), xs.reshape(ns, 1, D_MODEL),
            jnp.stack(gla_p), gla_new.transpose(0, 4, 1, 2, 3), jnp.stack(rgh_p), jnp.stack(rgh_s),
            jnp.stack(rgc_p), jnp.stack(rgc_s).transpose(0, 2, 1, 3), jnp.stack(swk_p), to_cache(swk_new),
            jnp.stack(swv_p), to_cache(swv_new),
            mk_all.reshape(depth, nb, n_mem, MEM_HEADS, MEM_HD), mv_all.reshape(depth, nb, n_mem, MEM_HEADS, MEM_HD))
```
